```python
import math
import jax
import jax.numpy as jnp
from jax import lax
import numpy as np

D_MODEL = 1024
BATCH = 8
SEQ = 8192
DEPTH = 4

CTX_LEN = 256
GRID_W = 64
N_MOD = 6
EPS = 1e-6
ROPE_BASE = 10000.0
DA_DIM = 64
DA_VDIM = 2 * DA_DIM
DA_HEADS = D_MODEL // DA_VDIM
DA_QBLOCK = 128
CONV_CH = D_MODEL
CONV_WIDTH = 31
RET_KDIM = 64
RET_VDIM = 2 * RET_KDIM
RET_HEADS = D_MODEL // RET_VDIM
RET_CHUNK = 128
N_BRANCH = 3
N_EXPERTS = 32
TOP_K = 4
D_EXPERT = D_MODEL
SWIGLU_ALPHA = 1.702
SWIGLU_LIMIT = 7.0
MOE_BLOCK = 512

COL_AQ = 0
COL_AK = COL_AQ + DA_HEADS * 2 * DA_DIM
COL_AV = COL_AK + DA_HEADS * 2 * DA_DIM
COL_CONV = COL_AV + DA_HEADS * DA_VDIM
COL_RQ = COL_CONV + 2 * CONV_CH
COL_RK = COL_RQ + RET_HEADS * RET_KDIM
COL_RV = COL_RK + RET_HEADS * RET_KDIM
COL_RGF = COL_RV + RET_HEADS * RET_VDIM
COL_RGB = COL_RGF + RET_HEADS * RET_VDIM
COL_GATE = COL_RGB + RET_HEADS * RET_VDIM
IN_COLS = COL_GATE + N_BRANCH * D_MODEL

kernel_name = "hybrid_diffattn_conformer_retention_moe_dit"


def rmsnorm(x, g):
    xf = x.astype(jnp.float32)
    y = xf * lax.rsqrt(jnp.mean(xf * xf, axis=-1, keepdims=True) + EPS)
    return (y * g.astype(jnp.float32)).astype(x.dtype)


def layernorm(x, g, b):
    xf = x.astype(jnp.float32)
    mu = jnp.mean(xf, axis=-1, keepdims=True)
    var = jnp.mean(jnp.square(xf - mu), axis=-1, keepdims=True)
    y = (xf - mu) * lax.rsqrt(var + EPS) * g.astype(jnp.float32) + b.astype(jnp.float32)
    return y.astype(x.dtype)


def modulation(cvec, w_mod, b_mod):
    m = jax.nn.silu(cvec) @ w_mod + b_mod
    return jnp.split(m, N_MOD, axis=-1)


def modulate(h, shift, scale):
    return h * (1.0 + scale) + shift


def rope_angles(pos, dim):
    inv = ROPE_BASE ** (-jnp.arange(0, dim, 2, dtype=jnp.float32) / dim)
    return pos.astype(jnp.float32)[:, None] * inv[None, :]


def apply_rope(x, ang):
    ang = ang.reshape((ang.shape[0],) + (1,) * (x.ndim - 3) + (ang.shape[1],))
    cos = jnp.cos(ang).astype(x.dtype)
    sin = jnp.sin(ang).astype(x.dtype)
    x1, x2 = jnp.split(x, 2, axis=-1)
    return jnp.concatenate([x1 * cos - x2 * sin, x2 * cos + x1 * sin], axis=-1)


def axial_rope(x, row, col):
    half = x.shape[-1] // 2
    return jnp.concatenate([apply_rope(x[..., :half], rope_angles(row, half)),
                            apply_rope(x[..., half:], rope_angles(col, half))], axis=-1)


def flip(a):
    return jnp.flip(a, axis=1)


def diff_softmax_mix(q, k, v, lam):
    s = jnp.einsum('bqhcd,bkhcd->bchqk', q, k).astype(jnp.float32) * (DA_DIM ** -0.5)
    p = jax.nn.softmax(s, axis=-1)
    w = (p[:, 0] - lam * p[:, 1]).astype(v.dtype)
    return jnp.einsum('bhqk,bkhe->bqhe', w, v)


def diff_attention(hx, hc, w_in, g_q, g_k, lam, lam_init, g_sub, row, col, need_ctx):
    B, L, _ = hx.shape
    qk_shape = (DA_HEADS, 2, DA_DIM)

    def heads(h, a, b, shape):
        return (h @ w_in[:, a:b]).reshape((B, h.shape[1]) + shape)

    qx = axial_rope(rmsnorm(heads(hx, COL_AQ, COL_AK, qk_shape), g_q), row, col)
    kx = axial_rope(rmsnorm(heads(hx, COL_AK, COL_AV, qk_shape), g_k), row, col)
    vx = heads(hx, COL_AV, COL_CONV, (DA_HEADS, DA_VDIM))
    kc = rmsnorm(heads(hc, COL_AK, COL_AV, qk_shape), g_k)
    vc = heads(hc, COL_AV, COL_CONV, (DA_HEADS, DA_VDIM))
    k_all = jnp.concatenate([kx, kc], axis=1)
    v_all = jnp.concatenate([vx, vc], axis=1)
    nb = L // DA_QBLOCK
    qb = qx.reshape(B, nb, DA_QBLOCK, DA_HEADS, 2, DA_DIM).transpose(1, 0, 2, 3, 4, 5)
    ox = lax.map(lambda qblk: diff_softmax_mix(qblk, k_all, v_all, lam), qb)
    ox = ox.transpose(1, 0, 2, 3, 4).reshape(B, L, DA_HEADS, DA_VDIM)

    def finish(o):
        return (rmsnorm(o, g_sub) * (1.0 - lam_init)).reshape(o.shape[:2] + (DA_HEADS * DA_VDIM,))

    oc = None
    if need_ctx:
        qc = rmsnorm(heads(hc, COL_AQ, COL_AK, qk_shape), g_q)
        oc = finish(diff_softmax_mix(qc, kc, vc, lam))
    return finish(ox), oc


def conformer_conv(h, w_in, conv_w, conv_b, ln_g, ln_b):
    u = h @ w_in[:, COL_CONV:COL_RQ]
    a, g = jnp.split(u, 2, axis=-1)
    u = a * jax.nn.sigmoid(g)
    u = lax.conv_general_dilated(u, conv_w[:, None, :].astype(u.dtype), window_strides=(1,),
                                 padding=[(CONV_WIDTH // 2, CONV_WIDTH // 2)],
                                 dimension_numbers=('NWC', 'WIO', 'NWC'),
                                 feature_group_count=CONV_CH) + conv_b
    return jax.nn.silu(layernorm(u, ln_g, ln_b))


def ret_decays(lg):
    pos = jnp.arange(RET_CHUNK, dtype=jnp.float32)
    diff = pos[:, None] - pos[None, :]
    inner = jnp.where(diff[None] >= 0, jnp.exp(jnp.maximum(diff, 0.0)[None] * lg[:, None, None]), 0.0)
    xi = jnp.exp((pos + 1.0)[None, :] * lg[:, None])
    zeta = jnp.exp((RET_CHUNK - 1.0 - pos)[None, :] * lg[:, None])
    g_chunk = jnp.exp(RET_CHUNK * lg)
    return inner, xi, zeta, g_chunk


def retention_scan(q, k, v, s0, lg):
    B, N, H, _ = q.shape
    dv = v.shape[-1]
    n = N // RET_CHUNK
    inner, xi, zeta, gc = [a.astype(q.dtype) for a in ret_decays(lg)]

    def chunks(a):
        return a.reshape(B, n, RET_CHUNK, H, a.shape[-1]).transpose(1, 0, 2, 3, 4)

    def step(s, qkv):
        qc, kc, vc = qkv
        att = jnp.einsum('bihd,bjhd->bhij', qc, kc) * inner
        o = (jnp.einsum('bhij,bjhe->bihe', att, vc)
             + jnp.einsum('bihd,bhde->bihe', qc, s) * xi.T[None, :, :, None])
        s = s * gc[None, :, None, None] + jnp.einsum('bjhd,bjhe->bhde', kc * zeta.T[None, :, :, None], vc)
        return s, o

    s, o = lax.scan(step, s0, (chunks(q), chunks(k), chunks(v)))
    return o.transpose(1, 0, 2, 3, 4).reshape(B, N, H, dv), s


def retention_state(k, v, lg, reverse):
    N = k.shape[1]
    pos = jnp.arange(N, dtype=jnp.float32)
    dist = pos if reverse else (N - 1.0 - pos)
    w = jnp.exp(dist[:, None] * lg[None, :]).astype(k.dtype)
    return jnp.einsum('bnhd,nh,bnhe->bhde', k, w, v)


def retention(hx, hc, w_in, gn_g, gn_b, lg, pos, need_ctx):
    B = hx.shape[0]
    kscale = RET_KDIM ** -0.5

    def heads(h, a, b, d):
        return (h @ w_in[:, a:b]).reshape(B, h.shape[1], RET_HEADS, d)

    ang = rope_angles(pos, RET_KDIM)
    qx = apply_rope(heads(hx, COL_RQ, COL_RK, RET_KDIM), ang)
    kx = apply_rope(heads(hx, COL_RK, COL_RV, RET_KDIM), ang) * kscale
    vx = heads(hx, COL_RV, COL_RGF, RET_VDIM)
    kc = heads(hc, COL_RK, COL_RV, RET_KDIM) * kscale
    vc = heads(hc, COL_RV, COL_RGF, RET_VDIM)

    def gated(h, of, ob):
        gf = jax.nn.silu(h @ w_in[:, COL_RGF:COL_RGB])
        gb = jax.nn.silu(h @ w_in[:, COL_RGB:COL_GATE])
        shp = of.shape[:2] + (RET_HEADS * RET_VDIM,)
        return gf * layernorm(of, gn_g, gn_b).reshape(shp) + gb * layernorm(ob, gn_g, gn_b).reshape(shp)

    oc = None
    if need_ctx:
        qc = heads(hc, COL_RQ, COL_RK, RET_KDIM)
        s0 = jnp.zeros((B, RET_HEADS, RET_KDIM, RET_VDIM), hc.dtype)
        ocf, sf = retention_scan(qc, kc, vc, s0, lg)
        ocb, sb = retention_scan(flip(qc), flip(kc), flip(vc), s0, lg)
        oc = gated(hc, ocf, flip(ocb))
    else:
        sf = retention_state(kc, vc, lg, False)
        sb = retention_state(kc, vc, lg, True)
    oxf, _ = retention_scan(qx, kx, vx, sf, lg)
    oxb, _ = retention_scan(flip(qx), flip(kx), flip(vx), sb, lg)
    return gated(hx, oxf, flip(oxb)), oc


def merge_branches(h, oa, ob, oc, w_in, b_gate, w_pa, w_pb, w_pc, w_out):
    g = jax.nn.sigmoid(h @ w_in[:, COL_GATE:IN_COLS] + b_gate.reshape(-1))
    g = g.reshape(h.shape[:2] + (N_BRANCH, D_MODEL))
    y = g[:, :, 0] * (oa @ w_pa) + g[:, :, 1] * (ob @ w_pb) + g[:, :, 2] * (oc @ w_pc)
    return y @ w_out


def moe_ffn(h, w_router, b_router, w_gu, b_gu, w_dn, b_dn):
    T = h.shape[0]
    A = T * TOP_K
    logits = (h @ w_router + b_router).astype(jnp.float32)
    top_v, top_i = lax.top_k(logits, TOP_K)
    gate = jax.nn.softmax(top_v, axis=-1).astype(h.dtype)
    e_flat = top_i.reshape(-1)
    order = jnp.argsort(e_flat, stable=True)
    e_sorted = e_flat[order]
    counts = jnp.bincount(e_flat, length=N_EXPERTS)
    starts = jnp.cumsum(counts) - counts
    padded = (counts + MOE_BLOCK - 1) // MOE_BLOCK * MOE_BLOCK
    pend = jnp.cumsum(padded)
    pstart = pend - padded
    dest = pstart[e_sorted] + jnp.arange(A) - starts[e_sorted]
    n_rows = -(-(A + N_EXPERTS * (MOE_BLOCK - 1)) // MOE_BLOCK) * MOE_BLOCK
    nb = n_rows // MOE_BLOCK
    row_tok = jnp.zeros((n_rows,), jnp.int32).at[dest].set((order // TOP_K).astype(jnp.int32))
    row_w = jnp.zeros((n_rows,), h.dtype).at[dest].set(gate.reshape(-1)[order])
    blk_e = jnp.minimum(jnp.searchsorted(pend, jnp.arange(nb) * MOE_BLOCK, side='right'), N_EXPERTS - 1)

    def expert_block(args):
        tok, wr, e = args
        xb = h[tok]
        gu = xb @ w_gu[e] + b_gu[e]
        gl = jnp.minimum(gu[:, 0::2], SWIGLU_LIMIT)
        up = jnp.clip(gu[:, 1::2], -SWIGLU_LIMIT, SWIGLU_LIMIT)
        act = (up + 1.0) * (gl * jax.nn.sigmoid(SWIGLU_ALPHA * gl))
        return (act @ w_dn[e] + b_dn[e]) * wr[:, None]

    out = lax.map(expert_block, (row_tok.reshape(nb, MOE_BLOCK), row_w.reshape(nb, MOE_BLOCK), blk_e))
    return jax.ops.segment_sum(out.reshape(n_rows, -1), row_tok, num_segments=T)


def setup_inputs(seed: int = 0) -> dict:
    key = jax.random.key(seed)
    ks = iter(jax.random.split(key, 40))
    f32 = jnp.float32

    def nrm(shape, scale):
        return jax.random.normal(next(ks), shape, f32) * scale

    def gain(shape):
        return 1.0 + nrm(shape, 0.02)

    D = D_MODEL
    return {
        "x": nrm((BATCH, SEQ, D), 1.0),
        "c": nrm((BATCH, D), 1.0),
        "ctx": nrm((BATCH, CTX_LEN, D), 1.0),
        "c_ctx": nrm((D,), 1.0),
        "w_mod": nrm((DEPTH, D, N_MOD * D), 0.5 * D ** -0.5),
        "b_mod": nrm((DEPTH, N_MOD * D), 0.01),
        "g_norm1": gain((DEPTH, D)),
        "g_norm2": gain((DEPTH, D)),
        "w_in": nrm((DEPTH, D, IN_COLS), D ** -0.5),
        "b_gate": nrm((DEPTH, N_BRANCH, D), 0.01),
        "g_qa": gain((DEPTH, DA_DIM)),
        "g_ka": gain((DEPTH, DA_DIM)),
        "lam_q1": nrm((DEPTH, DA_DIM), 0.1),
        "lam_k1": nrm((DEPTH, DA_DIM), 0.1),
        "lam_q2": nrm((DEPTH, DA_DIM), 0.1),
        "lam_k2": nrm((DEPTH, DA_DIM), 0.1),
        "g_sub": gain((DEPTH, DA_VDIM)),
        "conv_w": nrm((DEPTH, CONV_WIDTH, CONV_CH), CONV_WIDTH ** -0.5),
        "conv_b": nrm((DEPTH, CONV_CH), 0.01),
        "ln_g": gain((DEPTH, CONV_CH)),
        "ln_b": nrm((DEPTH, CONV_CH), 0.01),
        "gn_g": gain((DEPTH, RET_VDIM)),
        "gn_b": nrm((DEPTH, RET_VDIM), 0.01),
        "w_pa": nrm((DEPTH, DA_HEADS * DA_VDIM, D), (DA_HEADS * DA_VDIM) ** -0.5),
        "w_pb": nrm((DEPTH, CONV_CH, D), CONV_CH ** -0.5),
        "w_pc": nrm((DEPTH, RET_HEADS * RET_VDIM, D), (RET_HEADS * RET_VDIM) ** -0.5),
        "w_out": nrm((DEPTH, D, D), D ** -0.5),
        "w_router": nrm((DEPTH, D, N_EXPERTS), D ** -0.5),
        "b_router": nrm((DEPTH, N_EXPERTS), 0.01),
        "w_gate_up": nrm((DEPTH, N_EXPERTS, D, 2 * D_EXPERT), D ** -0.5),
        "b_gate_up": nrm((DEPTH, N_EXPERTS, 2 * D_EXPERT), 0.01),
        "w_down": nrm((DEPTH, N_EXPERTS, D_EXPERT, D), D_EXPERT ** -0.5),
        "b_down": nrm((DEPTH, N_EXPERTS, D), 0.01),
    }


def reference(x, c, ctx, c_ctx, w_mod, b_mod, g_norm1, g_norm2, w_in, b_gate, g_qa, g_ka,
              lam_q1, lam_k1, lam_q2, lam_k2, g_sub, conv_w, conv_b, ln_g, ln_b, gn_g, gn_b,
              w_pa, w_pb, w_pc, w_out, w_router, b_router, w_gate_up, b_gate_up, w_down, b_down):
    B, L, D = x.shape
    Nc = ctx.shape[1]
    rows = L // GRID_W
    row = jnp.repeat(jnp.arange(rows), GRID_W)
    col = jnp.tile(jnp.arange(GRID_W), rows)
    pos = jnp.arange(L)
    lg = jnp.log1p(-jnp.exp2(-5.0 - jnp.arange(RET_HEADS, dtype=jnp.float32)))
    h_ctx = ctx
    for l in range(DEPTH):
        last = l == DEPTH - 1
        need_ctx = not last
        lam_init = 0.8 - 0.6 * math.exp(-0.3 * l)
        lam = (jnp.exp(jnp.sum(lam_q1[l] * lam_k1[l]).astype(jnp.float32))
               - jnp.exp(jnp.sum(lam_q2[l] * lam_k2[l]).astype(jnp.float32)) + lam_init)
        sx1, cx1, gx1, sx2, cx2, gx2 = modulation(c[:, None, :], w_mod[l], b_mod[l])
        sc1, cc1, gc1, sc2, cc2, gc2 = modulation(c_ctx[None, :], w_mod[l], b_mod[l])

        hx = modulate(rmsnorm(x, g_norm1[l]), sx1, cx1)
        hc = modulate(rmsnorm(h_ctx, g_norm1[l]), sc1, cc1)
        oa_x, oa_c = diff_attention(hx, hc, w_in[l], g_qa[l], g_ka[l], lam, lam_init, g_sub[l], row, col, need_ctx)
        ob_x = conformer_conv(hx, w_in[l], conv_w[l], conv_b[l], ln_g[l], ln_b[l])
        oc_x, oc_c = retention(hx, hc, w_in[l], gn_g[l], gn_b[l], lg, pos, need_ctx)
        mx = merge_branches(hx, oa_x, ob_x, oc_x, w_in[l], b_gate[l], w_pa[l], w_pb[l], w_pc[l], w_out[l])
        x = x + gx1 * mx
        if need_ctx:
            ob_c = conformer_conv(hc, w_in[l], conv_w[l], conv_b[l], ln_g[l], ln_b[l])
            mc = merge_branches(hc, oa_c, ob_c, oc_c, w_in[l], b_gate[l], w_pa[l], w_pb[l], w_pc[l], w_out[l])
            h_ctx = h_ctx + gc1 * mc

        hx2 = modulate(rmsnorm(x, g_norm2[l]), sx2, cx2)
        moe_args = (w_router[l], b_router[l], w_gate_up[l], b_gate_up[l], w_down[l], b_down[l])
        if need_ctx:
            hc2 = modulate(rmsnorm(h_ctx, g_norm2[l]), sc2, cc2)
            f = moe_ffn(jnp.concatenate([hc2, hx2], axis=1).reshape(B * (Nc + L), D), *moe_args)
            f = f.reshape(B, Nc + L, D)
            h_ctx = h_ctx + gc2 * f[:, :Nc]
            fx = f[:, Nc:]
        else:
            fx = moe_ffn(hx2.reshape(B * L, D), *moe_args).reshape(B, L, D)
        x = x + gx2 * fx
    return x
```

```python
import functools
import math

import jax
import jax.numpy as jnp
from jax import lax
from jax.experimental import pallas as pl
from jax.experimental.pallas import tpu as pltpu

F32 = jnp.float32
BF16 = jnp.bfloat16

GRID_W = 64
N_MOD = 6
EPS = 1e-6
ROPE_BASE = 10000.0
RET_CHUNK = 128
TOP_K = 4
SWIGLU_ALPHA = 1.702
SWIGLU_LIMIT = 7.0

LANES = 128
SUBLANES = 8
V7X_VMEM_BYTES = 64 * 2**20
VMEM_LIMIT_CAP = 56 * 2**20
MIB = 2**20

NEG_INF = float("-inf")


def _sigmoid(x):
    return 1.0 / (1.0 + jnp.exp(-x))


def _vmem_limit(pipelined_bytes, resident_bytes):
    need = 2 * pipelined_bytes + resident_bytes
    return int(min(max(need, 16 * MIB), VMEM_LIMIT_CAP))


def _params(sem, pipelined_bytes, resident_bytes):
    return pltpu.CompilerParams(dimension_semantics=sem,
                                vmem_limit_bytes=_vmem_limit(pipelined_bytes, resident_bytes))


def _lane_iota(shape):
    return lax.broadcasted_iota(jnp.int32, shape, len(shape) - 1)


def _mod_kernel(c_ref, w_ref, b_ref, o_ref):
    c = c_ref[...]
    a = (c * _sigmoid(c)).astype(BF16)
    o_ref[...] = jnp.dot(a, w_ref[...].astype(BF16), preferred_element_type=F32) + b_ref[...]


def _modulation(cvecs, w_mod, b_mod):
    depth, d, cols = w_mod.shape
    rows = cvecs.shape[0]
    tn = 1536
    assert cols % tn == 0
    return pl.pallas_call(
        _mod_kernel,
        out_shape=jax.ShapeDtypeStruct((depth, rows, cols), F32),
        grid=(depth, cols // tn),
        in_specs=[pl.BlockSpec((rows, d), lambda l, j: (0, 0)),
                  pl.BlockSpec((None, d, tn), lambda l, j: (l, 0, j)),
                  pl.BlockSpec((None, 1, tn), lambda l, j: (l, 0, j))],
        out_specs=pl.BlockSpec((None, rows, tn), lambda l, j: (l, 0, j)),
        compiler_params=_params(("arbitrary", "arbitrary"), d * tn * 4 + rows * tn * 4, 4 * MIB),
        name="modulation",
    )(cvecs, w_mod, b_mod.reshape(depth, 1, cols))


def _inproj_kernel(h_ref, mod_ref, g_ref, w_ref, o_ref, xs_ref):
    @pl.when(pl.program_id(2) == 0)
    def _():
        h = h_ref[0]
        y = h * lax.rsqrt(jnp.mean(h * h, axis=-1, keepdims=True) + EPS) * g_ref[...]
        xs_ref[...] = (y * (1.0 + mod_ref[0, 1:2, :]) + mod_ref[0, 0:1, :]).astype(BF16)

    o_ref[0] = jnp.dot(xs_ref[...], w_ref[...], preferred_element_type=F32)


def _in_proj(h, mod, g, w_in_bf, layer):
    b, n, d = h.shape
    cols = w_in_bf.shape[-1]
    tm = min(n, 1024)
    tn = 1024
    assert n % tm == 0 and cols % tn == 0
    mb = mod.shape[0]
    return pl.pallas_call(
        _inproj_kernel,
        out_shape=jax.ShapeDtypeStruct((b, n, cols), F32),
        grid=(b, n // tm, cols // tn),
        in_specs=[pl.BlockSpec((1, tm, d), lambda bi, i, j: (bi, i, 0)),
                  pl.BlockSpec((1, N_MOD, d), lambda bi, i, j: (bi % mb, 0, 0)),
                  pl.BlockSpec((1, d), lambda bi, i, j: (0, 0)),
                  pl.BlockSpec((None, d, tn), lambda bi, i, j: (layer, 0, j))],
        out_specs=pl.BlockSpec((1, tm, tn), lambda bi, i, j: (bi, i, j)),
        scratch_shapes=[pltpu.VMEM((tm, d), BF16)],
        compiler_params=_params(("arbitrary", "arbitrary", "arbitrary"),
                                tm * d * 4 + d * tn * 2 + tm * tn * 4, tm * d * 2 + 3 * tm * d * 4),
        name="in_proj",
    )(h, mod, g.reshape(1, d), w_in_bf)


def _prep_kernel(aq_ref, ak_ref, rq_ref, rk_ref, gq_ref, gk_ref, ca_ref, sa_ref, cr_ref, sr_ref,
                 oq_ref, ok_ref, orq_ref, ork_ref, *, rope, da_dim, ret_kdim):
    tm = aq_ref.shape[1]
    lane = _lane_iota((tm, LANES))
    lo = lane < da_dim

    def rms(x, g):
        x2 = x * x
        s_lo = jnp.sum(jnp.where(lo, x2, 0.0), axis=-1, keepdims=True)
        s_hi = jnp.sum(jnp.where(lo, 0.0, x2), axis=-1, keepdims=True)
        ms = jnp.where(lo, s_lo, s_hi) * (1.0 / da_dim)
        return x * lax.rsqrt(ms + EPS) * g

    def rot(x, c, s, half):
        first = (lane % (2 * half)) < half
        partner = jnp.where(first, pltpu.roll(x, LANES - half, 1), pltpu.roll(x, half, 1))
        return x * c + partner * s

    q_scale = da_dim ** -0.5
    k_scale = ret_kdim ** -0.5
    for j in range(aq_ref.shape[2] // LANES):
        sl = slice(j * LANES, (j + 1) * LANES)
        q = rms(aq_ref[0, :, sl], gq_ref[...])
        k = rms(ak_ref[0, :, sl], gk_ref[...])
        if rope:
            q = rot(q, ca_ref[...], sa_ref[...], da_dim // 4)
            k = rot(k, ca_ref[...], sa_ref[...], da_dim // 4)
        oq_ref[0, :, sl] = (q * q_scale).astype(BF16)
        ok_ref[0, :, sl] = k.astype(BF16)
    for j in range(rq_ref.shape[2] // LANES):
        sl = slice(j * LANES, (j + 1) * LANES)
        q = rq_ref[0, :, sl]
        k = rk_ref[0, :, sl]
        if rope:
            q = rot(q, cr_ref[...], sr_ref[...], ret_kdim // 2)
            k = rot(k, cr_ref[...], sr_ref[...], ret_kdim // 2)
        orq_ref[0, :, sl] = q
        ork_ref[0, :, sl] = k * k_scale


def _qk_prep(proj, g_q, g_k, tabs, cols, *, rope, da_dim, ret_kdim):
    b, n, _ = proj.shape
    d = cols["d"]
    rw = cols["ret_qk"]
    tm = min(n, 512)
    assert n % tm == 0
    gq = jnp.tile(g_q, LANES // da_dim).reshape(1, LANES)
    gk = jnp.tile(g_k, LANES // da_dim).reshape(1, LANES)
    tab_spec = pl.BlockSpec((tm, LANES), lambda bi, i: (i, 0))
    vec_spec = pl.BlockSpec((1, LANES), lambda bi, i: (0, 0))

    def col_spec(width, off):
        assert off % width == 0
        return pl.BlockSpec((1, tm, width), lambda bi, i: (bi, i, off // width))

    kern = functools.partial(_prep_kernel, rope=rope, da_dim=da_dim, ret_kdim=ret_kdim)
    return pl.pallas_call(
        kern,
        out_shape=(jax.ShapeDtypeStruct((b, n, d), BF16), jax.ShapeDtypeStruct((b, n, d), BF16),
                   jax.ShapeDtypeStruct((b, n, rw), F32), jax.ShapeDtypeStruct((b, n, rw), F32)),
        grid=(b, n // tm),
        in_specs=[col_spec(d, cols["aq"]), col_spec(d, cols["ak"]),
                  col_spec(rw, cols["rq"]), col_spec(rw, cols["rk"]),
                  vec_spec, vec_spec, tab_spec, tab_spec, tab_spec, tab_spec],
        out_specs=(pl.BlockSpec((1, tm, d), lambda bi, i: (bi, i, 0)),
                   pl.BlockSpec((1, tm, d), lambda bi, i: (bi, i, 0)),
                   pl.BlockSpec((1, tm, rw), lambda bi, i: (bi, i, 0)),
                   pl.BlockSpec((1, tm, rw), lambda bi, i: (bi, i, 0))),
        compiler_params=_params(("arbitrary", "arbitrary"),
                                tm * (2 * d + 2 * rw) * 4 + tm * (2 * d * 2 + 2 * rw * 4)
                                + 4 * tm * LANES * 4, 8 * MIB),
        name="qk_prep",
    )(proj, proj, proj, proj, gq, gk, *tabs)


def _rope_tables(n_tok, da_dim, ret_kdim):
    t = jnp.arange(n_tok)

    def angles(pos, dim):
        inv = ROPE_BASE ** (-jnp.arange(0, dim, 2, dtype=F32) / dim)
        return pos.astype(F32)[:, None] * inv[None, :]

    def cs(ang):
        c, s = jnp.cos(ang), jnp.sin(ang)
        return jnp.concatenate([c, c], axis=-1), jnp.concatenate([-s, s], axis=-1)

    half = da_dim // 2
    cr_, sr_ = cs(angles(t // GRID_W, half))
    cc_, sc_ = cs(angles(t % GRID_W, half))
    ca = jnp.tile(jnp.concatenate([cr_, cc_], axis=-1), (1, LANES // da_dim))
    sa = jnp.tile(jnp.concatenate([sr_, sc_], axis=-1), (1, LANES // da_dim))
    c1, s1 = cs(angles(t, ret_kdim))
    cr = jnp.tile(c1, (1, LANES // ret_kdim))
    sr = jnp.tile(s1, (1, LANES // ret_kdim))
    return ca, sa, cr, sr


def _attn_kernel(scal_ref, q_ref, kx_ref, vx_ref, kc_ref, vc_ref, gsub_ref, o_ref,
                 v2x_ref, v2c_ref, acc_ref, m_ref, *, tk, with_x, da_dim):
    tq = q_ref.shape[1]
    dv = vc_ref.shape[2]
    nc = kc_ref.shape[1]

    @pl.when(pl.program_id(2) == 0)
    def _():
        v2c_ref[:, 0:dv] = vc_ref[0].astype(BF16)
        v2c_ref[:, dv:2 * dv] = jnp.ones((nc, dv), BF16)
        if with_x:
            rows = min(512, vx_ref.shape[1])

            def fill(j, c):
                off = pl.multiple_of(j * rows, rows)
                v2x_ref[pl.ds(off, rows), 0:dv] = vx_ref[0, pl.ds(off, rows), :].astype(BF16)
                v2x_ref[pl.ds(off, rows), dv:2 * dv] = jnp.ones((rows, dv), BF16)
                return c

            lax.fori_loop(0, vx_ref.shape[1] // rows, fill, 0)

    q = q_ref[0]
    lo = _lane_iota((tq, LANES)) < da_dim
    zero = jnp.zeros_like(q)
    q2 = jnp.concatenate([jnp.where(lo, q, zero), jnp.where(lo, zero, q)], axis=0)

    m_ref[...] = jnp.full(m_ref.shape, NEG_INF, F32)
    acc_ref[...] = jnp.zeros(acc_ref.shape, F32)

    def step(k, v2):
        s = lax.dot_general(q2, k, (((1,), (1,)), ((), ())), preferred_element_type=F32)
        m_prev = m_ref[...]
        m_new = jnp.maximum(m_prev, jnp.max(s, axis=-1, keepdims=True))
        alpha = jnp.exp(m_prev - m_new)
        p = jnp.exp(s - m_new[:, 0:1]).astype(BF16)
        acc_ref[...] = alpha[:, 0:1] * acc_ref[...] + jnp.dot(p, v2, preferred_element_type=F32)
        m_ref[...] = m_new

    if with_x:
        def body(j, c):
            off = pl.multiple_of(j * tk, tk)
            step(kx_ref[0, pl.ds(off, tk), :], v2x_ref[pl.ds(off, tk), :])
            return c

        lax.fori_loop(0, kx_ref.shape[1] // tk, body, 0)
    step(kc_ref[0], v2c_ref[...])

    acc = acc_ref[...]
    o0 = acc[0:tq, 0:dv] / acc[0:tq, dv:dv + 1]
    o1 = acc[tq:2 * tq, 0:dv] / acc[tq:2 * tq, dv:dv + 1]
    o = o0 - scal_ref[0] * o1
    y = o * lax.rsqrt(jnp.mean(o * o, axis=-1, keepdims=True) + EPS) * gsub_ref[...]
    o_ref[0] = (y * scal_ref[1]).astype(BF16)


def _attention(scal, q, kx, proj_x, kc, proj_c, g_sub, cols, *, with_x, da_dim):
    b, nq, d = q.shape
    dv = g_sub.shape[0]
    heads = d // dv
    nc = kc.shape[1]
    nx = kx.shape[1]
    tq = min(nq, 256)
    tk = min(nx, 512)
    assert nq % tq == 0 and nx % tk == 0 and 2 * da_dim == LANES and dv == LANES
    v_blk = cols["av"] // dv
    kern = functools.partial(_attn_kernel, tk=tk, with_x=with_x, da_dim=da_dim)
    pipelined = tq * dv * 2 * 2 + nx * dv * (2 + 4) + nc * dv * (2 + 4)
    resident = (nx + nc) * 2 * dv * 2 + 2 * tq * 2 * dv * 4 + 2 * tq * LANES * 4 + 6 * 2 * tq * tk * 4
    return pl.pallas_call(
        kern,
        out_shape=jax.ShapeDtypeStruct((b, nq, d), BF16),
        grid=(b, heads, nq // tq),
        in_specs=[pl.BlockSpec(memory_space=pltpu.SMEM),
                  pl.BlockSpec((1, tq, dv), lambda bi, h, i: (bi, i, h)),
                  pl.BlockSpec((1, nx, dv), lambda bi, h, i: (bi, 0, h)),
                  pl.BlockSpec((1, nx, dv), lambda bi, h, i: (bi, 0, v_blk + h)),
                  pl.BlockSpec((1, nc, dv), lambda bi, h, i: (bi, 0, h)),
                  pl.BlockSpec((1, nc, dv), lambda bi, h, i: (bi, 0, v_blk + h)),
                  pl.BlockSpec((1, dv), lambda bi, h, i: (0, 0))],
        out_specs=pl.BlockSpec((1, tq, dv), lambda bi, h, i: (bi, i, h)),
        scratch_shapes=[pltpu.VMEM((nx, 2 * dv), BF16), pltpu.VMEM((nc, 2 * dv), BF16),
                        pltpu.VMEM((2 * tq, 2 * dv), F32), pltpu.VMEM((2 * tq, LANES), F32)],
        compiler_params=_params(("arbitrary", "arbitrary", "arbitrary"), pipelined, resident),
        name="diff_attention" if with_x else "diff_attention_ctx",
    )(scal, q, kx, proj_x, kc, proj_c, g_sub.reshape(1, dv))


CONV_HALO = 16


def _conv_kernel(a_ref, g_ref, ap_ref, gp_ref, an_ref, gn_ref, w_ref, cb_ref, lg_ref, lb_ref,
                 o_ref, ext_ref, cv_ref):
    i = pl.program_id(1)
    last = pl.num_programs(1) - 1
    tm = a_ref.shape[1]
    ch = a_ref.shape[2]
    width = w_ref.shape[0]
    pad = width // 2

    def glu(a, g):
        return a * _sigmoid(g)

    ext_ref[0:CONV_HALO, :] = jnp.where(i > 0, glu(ap_ref[0], gp_ref[0]), 0.0)
    ext_ref[CONV_HALO:CONV_HALO + tm, :] = glu(a_ref[0], g_ref[0])
    ext_ref[CONV_HALO + tm:CONV_HALO + tm + CONV_HALO, :] = jnp.where(i < last, glu(an_ref[0], gn_ref[0]), 0.0)

    for c in range(ch // LANES):
        sl = slice(c * LANES, (c + 1) * LANES)
        acc = jnp.zeros((tm, LANES), F32)
        for j in range(width):
            acc = acc + w_ref[j:j + 1, sl] * ext_ref[CONV_HALO - pad + j:CONV_HALO - pad + j + tm, sl]
        cv_ref[:, sl] = acc + cb_ref[:, sl]

    v = cv_ref[...]
    mu = jnp.mean(v, axis=-1, keepdims=True)
    vc = v - mu
    var = jnp.mean(vc * vc, axis=-1, keepdims=True)
    y = vc * lax.rsqrt(var + EPS) * lg_ref[...] + lb_ref[...]
    o_ref[0] = (y * _sigmoid(y)).astype(BF16)


def _conformer_conv(proj, conv_w, conv_b, ln_g, ln_b, cols):
    b, n, _ = proj.shape
    width, ch = conv_w.shape
    tm = min(n, 256)
    assert n % tm == 0 and tm % CONV_HALO == 0 and width // 2 < CONV_HALO
    a_blk = cols["conv"] // ch
    g_blk = a_blk + 1
    r = tm // CONV_HALO
    n_halo = n // CONV_HALO

    def cur(blk):
        return pl.BlockSpec((1, tm, ch), lambda bi, i: (bi, i, blk))

    def prev(blk):
        return pl.BlockSpec((1, CONV_HALO, ch), lambda bi, i: (bi, jnp.maximum(i * r - 1, 0), blk))

    def nxt(blk):
        return pl.BlockSpec((1, CONV_HALO, ch), lambda bi, i: (bi, jnp.minimum((i + 1) * r, n_halo - 1), blk))

    vec = pl.BlockSpec((1, ch), lambda bi, i: (0, 0))
    return pl.pallas_call(
        _conv_kernel,
        out_shape=jax.ShapeDtypeStruct((b, n, ch), BF16),
        grid=(b, n // tm),
        in_specs=[cur(a_blk), cur(g_blk), prev(a_blk), prev(g_blk), nxt(a_blk), nxt(g_blk),
                  pl.BlockSpec((width, ch), lambda bi, i: (0, 0)), vec, vec, vec],
        out_specs=pl.BlockSpec((1, tm, ch), lambda bi, i: (bi, i, 0)),
        scratch_shapes=[pltpu.VMEM((tm + 2 * CONV_HALO, ch), F32), pltpu.VMEM((tm, ch), F32)],
        compiler_params=_params(("arbitrary", "arbitrary"),
                                (2 * tm + 4 * CONV_HALO) * ch * 4 + tm * ch * 2 + 40 * ch * 4,
                                (2 * tm + 2 * CONV_HALO) * ch * 4 + 4 * tm * ch * 4),
        name="conformer_conv",
    )(proj, proj, proj, proj, proj, proj, conv_w, conv_b.reshape(1, ch), ln_g.reshape(1, ch),
      ln_b.reshape(1, ch))


def _ret_kernel(qf_ref, kf_ref, vf_ref, gf_ref, qb_ref, kb_ref, vb_ref, gb_ref, s0_ref,
                inner_ref, xi_ref, zeta_ref, gc_ref, gng_ref, gnb_ref,
                of_ref, ob_ref, sfin_ref, s_ref, *, kdim):
    i = pl.program_id(1)
    c = qf_ref.shape[1]
    dv = gng_ref.shape[1]
    n_pair = qf_ref.shape[2] // LANES

    @pl.when(i == 0)
    def _():
        s_ref[...] = s0_ref[0]

    lo = _lane_iota((c, LANES)) < kdim
    dirs = ((qf_ref, kf_ref, vf_ref, gf_ref, of_ref), (qb_ref, kb_ref, vb_ref, gb_ref, ob_ref))
    for d, (q_ref, k_ref, v_ref, g_ref, o_ref) in enumerate(dirs):
        for p in range(n_pair):
            sl = slice(p * LANES, (p + 1) * LANES)
            q = q_ref[0, :, sl]
            k = k_ref[0, :, sl]
            kz = k * zeta_ref[d, p]
            kb16 = k.astype(BF16)
            s_pair = s_ref[d, p]
            s16 = s_pair.astype(BF16)
            upd = jnp.zeros((LANES, dv), F32)
            for hh in range(2):
                h = 2 * p + hh
                hsl = slice(h * dv, (h + 1) * dv)
                keep = lo if hh == 0 else jnp.logical_not(lo)
                qh = jnp.where(keep, q, 0.0).astype(BF16)
                v16 = v_ref[0, :, hsl].astype(BF16)
                att = lax.dot_general(qh, kb16, (((1,), (1,)), ((), ())), preferred_element_type=F32)
                att = (att * inner_ref[d, h]).astype(BF16)
                o = (jnp.dot(att, v16, preferred_element_type=F32)
                     + jnp.dot(qh, s16, preferred_element_type=F32) * xi_ref[d, h])
                kzh = jnp.where(keep, kz, 0.0).astype(BF16)
                upd = upd + lax.dot_general(kzh, v16, (((0,), (0,)), ((), ())), preferred_element_type=F32)
                mu = jnp.mean(o, axis=-1, keepdims=True)
                oc = o - mu
                var = jnp.mean(oc * oc, axis=-1, keepdims=True)
                y = oc * lax.rsqrt(var + EPS) * gng_ref[...] + gnb_ref[...]
                g = g_ref[0, :, hsl]
                o_ref[0, :, hsl] = (g * _sigmoid(g)) * y
            s_ref[d, p] = s_pair * gc_ref[p] + upd

    sfin_ref[0] = s_ref[...]


def _retention(rq, rk, proj, s0, tabs, gn_g, gn_b, cols):
    b, n, rw = rq.shape
    dv = gn_g.shape[0]
    d = cols["d"]
    c = RET_CHUNK
    nch = n // c
    kdim = dv // 2
    assert n % c == 0 and c == LANES and 2 * kdim == LANES
    n_pair = rw // LANES
    inner, xi, zeta, gc = tabs
    v_blk, gf_blk, gb_blk = cols["rv"] // d, cols["rgf"] // d, cols["rgb"] // d

    def fwd(width, blk):
        return pl.BlockSpec((1, c, width), lambda bi, i: (bi, i, blk))

    def bwd(width, blk):
        return pl.BlockSpec((1, c, width), lambda bi, i: (bi, nch - 1 - i, blk))

    def whole(a):
        return pl.BlockSpec(a.shape, lambda bi, i: (0,) * a.ndim)

    state = pl.BlockSpec((1, 2, n_pair, LANES, dv), lambda bi, i: (bi, 0, 0, 0, 0))
    vec = pl.BlockSpec((1, dv), lambda bi, i: (0, 0))
    kern = functools.partial(_ret_kernel, kdim=kdim)
    pipelined = 2 * c * (2 * rw + 2 * d) * 4 + 2 * c * d * 4 + 2 * 2 * n_pair * LANES * dv * 4
    resident = (inner.size + xi.size + zeta.size + gc.size) * 4 * 2 + 2 * n_pair * LANES * dv * 4 + 8 * MIB
    return pl.pallas_call(
        kern,
        out_shape=(jax.ShapeDtypeStruct((b, n, d), F32), jax.ShapeDtypeStruct((b, n, d), F32),
                   jax.ShapeDtypeStruct((b, 2, n_pair, LANES, dv), F32)),
        grid=(b, nch),
        in_specs=[fwd(rw, 0), fwd(rw, 0), fwd(d, v_blk), fwd(d, gf_blk),
                  bwd(rw, 0), bwd(rw, 0), bwd(d, v_blk), bwd(d, gb_blk),
                  state, whole(inner), whole(xi), whole(zeta), whole(gc), vec, vec],
        out_specs=(pl.BlockSpec((1, c, d), lambda bi, i: (bi, i, 0)),
                   pl.BlockSpec((1, c, d), lambda bi, i: (bi, nch - 1 - i, 0)),
                   state),
        scratch_shapes=[pltpu.VMEM((2, n_pair, LANES, dv), F32)],
        compiler_params=_params(("arbitrary", "arbitrary"), pipelined, resident),
        name="retention",
    )(rq, rk, proj, proj, rq, rk, proj, proj, s0, inner, xi, zeta, gc,
      gn_g.reshape(1, dv), gn_b.reshape(1, dv))


def _retention_tables(heads, kdim, dv):
    c = RET_CHUNK
    lg = jnp.log1p(-jnp.exp2(-5.0 - jnp.arange(heads, dtype=F32)))
    pos = jnp.arange(c, dtype=F32)
    diff = pos[:, None] - pos[None, :]
    inner_f = jnp.where(diff[None] >= 0, jnp.exp(jnp.maximum(diff, 0.0)[None] * lg[:, None, None]), 0.0)
    inner = jnp.stack([inner_f, jnp.swapaxes(inner_f, 1, 2)])
    xi_f = jnp.exp((pos + 1.0)[None, :] * lg[:, None])
    zeta_f = jnp.exp((c - 1.0 - pos)[None, :] * lg[:, None])
    xi = jnp.stack([xi_f, xi_f[:, ::-1]])
    zeta = jnp.stack([zeta_f, zeta_f[:, ::-1]])
    xi = jnp.broadcast_to(xi[..., None], (2, heads, c, dv))
    zeta = jnp.broadcast_to(zeta[..., None], (2, heads, c, kdim))
    zeta = zeta.reshape(2, heads // 2, 2, c, kdim).transpose(0, 1, 3, 2, 4).reshape(2, heads // 2, c, 2 * kdim)
    gc = jnp.exp(c * lg)
    gc = jnp.broadcast_to(gc[:, None, None], (heads, kdim, dv)).reshape(heads // 2, 2 * kdim, dv)
    return inner.astype(F32), xi.astype(F32), zeta.astype(F32), gc.astype(F32)


def _merge_kernel(h_ref, mod_ref, oa_ref, ob_ref, of_ref, obk_ref, gate_ref, bg_ref,
                  wa_ref, wb_ref, wc_ref, wo_ref, o_ref):
    d = h_ref.shape[2]
    g = _sigmoid(gate_ref[0] + bg_ref[...])
    oc = (of_ref[0] + obk_ref[0]).astype(BF16)
    y = g[:, 0:d] * jnp.dot(oa_ref[0], wa_ref[...], preferred_element_type=F32)
    y = y + g[:, d:2 * d] * jnp.dot(ob_ref[0], wb_ref[...], preferred_element_type=F32)
    y = y + g[:, 2 * d:3 * d] * jnp.dot(oc, wc_ref[...], preferred_element_type=F32)
    m = jnp.dot(y.astype(BF16), wo_ref[...], preferred_element_type=F32)
    o_ref[0] = h_ref[0] + mod_ref[0, 2:3, :] * m


def _merge(h, mod, oa, ob, of, obk, proj, b_gate, w_pa, w_pb, w_pc, w_out, layer, cols):
    b, n, d = h.shape
    tm = min(n, 512)
    assert n % tm == 0
    mb = mod.shape[0]
    gate_blk = cols["gate"] // (3 * d)
    row = lambda width: pl.BlockSpec((1, tm, width), lambda bi, i: (bi, i, 0))
    wspec = pl.BlockSpec((None, d, d), lambda bi, i: (layer, 0, 0))
    return pl.pallas_call(
        _merge_kernel,
        out_shape=jax.ShapeDtypeStruct((b, n, d), F32),
        grid=(b, n // tm),
        in_specs=[row(d), pl.BlockSpec((1, N_MOD, d), lambda bi, i: (bi % mb, 0, 0)),
                  row(d), row(d), row(d), row(d),
                  pl.BlockSpec((1, tm, 3 * d), lambda bi, i: (bi, i, gate_blk)),
                  pl.BlockSpec((1, 3 * d), lambda bi, i: (0, 0)),
                  wspec, wspec, wspec, wspec],
        out_specs=row(d),
        compiler_params=_params(("arbitrary", "arbitrary"),
                                tm * d * (4 + 2 + 2 + 4 + 4 + 12 + 4) + 4 * d * d * 2, 10 * tm * d * 4),
        name="merge",
    )(h, mod, oa, ob, of, obk, proj, b_gate.reshape(1, 3 * d), w_pa, w_pb, w_pc, w_out)


ROUTE_E, ROUTE_G, ROUTE_R = 0, 4, 8


def _route_kernel(h_ref, mod_ref, g_ref, wr_ref, br_ref, tri_ref, base_ref,
                  hx_ref, route_ref, cnt_ref, run_ref, *, n_exp):
    first = jnp.logical_and(pl.program_id(0) == 0, pl.program_id(1) == 0)

    @pl.when(first)
    def _():
        run_ref[...] = base_ref[...]

    h = h_ref[0]
    tm = h.shape[0]
    y = h * lax.rsqrt(jnp.mean(h * h, axis=-1, keepdims=True) + EPS) * g_ref[...]
    hx = y * (1.0 + mod_ref[0, 4:5, :]) + mod_ref[0, 3:4, :]
    hx_ref[0] = hx

    lane = _lane_iota((tm, LANES))
    lane_f = lane.astype(F32)
    logits = jnp.dot(hx, wr_ref[...], preferred_element_type=F32,
                     precision=lax.Precision.HIGHEST) + br_ref[...]
    logits = jnp.where(lane < n_exp, logits, NEG_INF)

    sels, vals = [], []
    for _ in range(TOP_K):
        mx = jnp.max(logits, axis=-1, keepdims=True)
        idx = jnp.min(jnp.where(logits == mx, lane_f, float(LANES)), axis=-1, keepdims=True)
        sel = lane_f == idx
        sels.append((sel, idx))
        vals.append(mx)
        logits = jnp.where(sel, NEG_INF, logits)

    es = [jnp.exp(v - vals[0]) for v in vals]
    denom = es[0]
    for e in es[1:]:
        denom = denom + e

    onehot = jnp.zeros((tm, LANES), F32)
    for sel, _ in sels:
        onehot = onehot + jnp.where(sel, 1.0, 0.0)
    before = run_ref[...] + jnp.dot(tri_ref[...], onehot.astype(BF16), preferred_element_type=F32)

    route = jnp.zeros((tm, LANES), F32)
    for k, (sel, idx) in enumerate(sels):
        rank = jnp.sum(jnp.where(sel, before, 0.0), axis=-1, keepdims=True)
        route = jnp.where(lane == ROUTE_E + k, idx, route)
        route = jnp.where(lane == ROUTE_G + k, es[k] / denom, route)
        route = jnp.where(lane == ROUTE_R + k, rank, route)
    route_ref[0] = route

    run_ref[...] = run_ref[...] + jnp.sum(onehot, axis=0, keepdims=True)
    cnt_ref[...] = run_ref[...]


def _route(h, mod, g, w_router, b_router, base):
    b, n, d = h.shape
    n_exp = w_router.shape[-1]
    tm = min(n, 256)
    assert n % tm == 0 and n_exp <= LANES
    mb = mod.shape[0]
    wr = jnp.zeros((d, LANES), F32).at[:, :n_exp].set(w_router)
    br = jnp.zeros((1, LANES), F32).at[0, :n_exp].set(b_router)
    tri = (jnp.arange(tm)[:, None] > jnp.arange(tm)[None, :]).astype(BF16)
    row = lambda width: pl.BlockSpec((1, tm, width), lambda bi, i: (bi, i, 0))
    const = lambda shape: pl.BlockSpec(shape, lambda bi, i: (0,) * len(shape))
    kern = functools.partial(_route_kernel, n_exp=n_exp)
    return pl.pallas_call(
        kern,
        out_shape=(jax.ShapeDtypeStruct((b, n, d), F32), jax.ShapeDtypeStruct((b, n, LANES), F32),
                   jax.ShapeDtypeStruct((1, LANES), F32)),
        grid=(b, n // tm),
        in_specs=[row(d), pl.BlockSpec((1, N_MOD, d), lambda bi, i: (bi % mb, 0, 0)), const((1, d)),
                  const((d, LANES)), const((1, LANES)), const((tm, tm)), const((1, LANES))],
        out_specs=(row(d), row(LANES), const((1, LANES))),
        scratch_shapes=[pltpu.VMEM((1, LANES), F32)],
        compiler_params=_params(("arbitrary", "arbitrary"),
                                2 * tm * d * 4 + tm * LANES * 4 + d * LANES * 4 + tm * tm * 2, 8 * tm * d * 4),
        name="moe_route",
    )(h, mod, g.reshape(1, d), wr, br, tri, base)


def _dispatch_kernel(dest_ref, hx_ref, xs_in_ref, xs_ref, sem):
    del xs_in_ref
    n_copy = dest_ref.shape[2]

    def copy(r):
        return pltpu.make_async_copy(hx_ref.at[pl.ds(r // TOP_K, 1)],
                                     xs_ref.at[pl.ds(dest_ref[0, 0, r], 1)], sem)

    def start(r, c):
        copy(r).start()
        return c

    def wait(r, c):
        copy(r).wait()
        return c

    lax.fori_loop(0, n_copy, start, 0)
    lax.fori_loop(0, n_copy, wait, 0)


def _dispatch(hx_flat, dest, xs):
    t, d = hx_flat.shape
    tm = min(t, 256)
    assert t % tm == 0
    dest3 = dest.reshape(t // tm, 1, tm * TOP_K)
    return pl.pallas_call(
        _dispatch_kernel,
        out_shape=jax.ShapeDtypeStruct(xs.shape, xs.dtype),
        grid=(t // tm,),
        in_specs=[pl.BlockSpec((1, 1, tm * TOP_K), lambda i: (i, 0, 0), memory_space=pltpu.SMEM),
                  pl.BlockSpec((tm, d), lambda i: (i, 0)),
                  pl.BlockSpec(memory_space=pl.ANY)],
        out_specs=pl.BlockSpec(memory_space=pl.ANY),
        scratch_shapes=[pltpu.SemaphoreType.DMA],
        input_output_aliases={2: 0},
        compiler_params=pltpu.CompilerParams(dimension_semantics=("arbitrary",),
                                             vmem_limit_bytes=_vmem_limit(tm * d * 4, 4 * MIB),
                                             has_side_effects=True),
        name="moe_dispatch",
    )(dest3, hx_flat, xs)


def _expert_kernel(blk_e_ref, n_used_ref, x_ref, wg_ref, wu_ref, wd_ref, bg_ref, bu_ref, bd_ref, y_ref):
    del blk_e_ref
    used = pl.program_id(0) < n_used_ref[0]

    @pl.when(used)
    def _():
        x = x_ref[...].astype(BF16)
        gl = jnp.dot(x, wg_ref[...], preferred_element_type=F32) + bg_ref[...]
        up = jnp.dot(x, wu_ref[...], preferred_element_type=F32) + bu_ref[...]
        gl = jnp.minimum(gl, SWIGLU_LIMIT)
        up = jnp.clip(up, -SWIGLU_LIMIT, SWIGLU_LIMIT)
        act = (up + 1.0) * (gl * _sigmoid(SWIGLU_ALPHA * gl))
        y_ref[...] = jnp.dot(act.astype(BF16), wd_ref[...], preferred_element_type=F32) + bd_ref[...]

    @pl.when(jnp.logical_not(used))
    def _():
        y_ref[...] = jnp.zeros(y_ref.shape, F32)


def _experts(blk_e, n_used, xs, wg, wu, wd, bg, bu, bd, layer, bm):
    n_rows, d = xs.shape
    de = wg.shape[-1]
    nb = n_rows // bm
    wspec = lambda a, c: pl.BlockSpec((None, None, a, c), lambda i, be, nu: (layer, be[i], 0, 0))
    return pl.pallas_call(
        _expert_kernel,
        out_shape=jax.ShapeDtypeStruct((n_rows, d), F32),
        grid_spec=pltpu.PrefetchScalarGridSpec(
            num_scalar_prefetch=2,
            grid=(nb,),
            in_specs=[pl.BlockSpec((bm, d), lambda i, be, nu: (i, 0)),
                      wspec(d, de), wspec(d, de), wspec(de, d),
                      wspec(1, de), wspec(1, de), wspec(1, d)],
            out_specs=pl.BlockSpec((bm, d), lambda i, be, nu: (i, 0))),
        compiler_params=_params(("arbitrary",), 2 * bm * d * 4 + 3 * d * de * 2, 8 * bm * de * 4),
        name="moe_experts",
    )(blk_e, n_used, xs, wg, wu, wd, bg, bu, bd)


def _combine_kernel(dest_ref, h_ref, mod_ref, route_ref, y_ref, o_ref, ybuf, sem):
    tm = h_ref.shape[1]
    n_copy = tm * TOP_K

    def copy(r):
        src = dest_ref[0, 0, (r % tm) * TOP_K + r // tm]
        return pltpu.make_async_copy(y_ref.at[pl.ds(src, 1)], ybuf.at[pl.ds(r, 1)], sem)

    def start(r, c):
        copy(r).start()
        return c

    def wait(r, c):
        copy(r).wait()
        return c

    lax.fori_loop(0, n_copy, start, 0)
    lax.fori_loop(0, n_copy, wait, 0)

    route = route_ref[0]
    f = jnp.zeros((tm, h_ref.shape[2]), F32)
    for k in range(TOP_K):
        f = f + route[:, ROUTE_G + k:ROUTE_G + k + 1] * ybuf[k * tm:(k + 1) * tm, :]
    o_ref[0] = h_ref[0] + mod_ref[0, 5:6, :] * f


def _combine(h, mod, route, dest, y):
    b, n, d = h.shape
    tm = min(n, 256)
    assert n % tm == 0
    mb = mod.shape[0]
    nt = n // tm
    dest3 = dest.reshape(b * nt, 1, tm * TOP_K)
    row = lambda width: pl.BlockSpec((1, tm, width), lambda bi, i: (bi, i, 0))
    return pl.pallas_call(
        _combine_kernel,
        out_shape=jax.ShapeDtypeStruct((b, n, d), F32),
        grid=(b, nt),
        in_specs=[pl.BlockSpec((1, 1, tm * TOP_K), lambda bi, i: (bi * nt + i, 0, 0), memory_space=pltpu.SMEM),
                  row(d), pl.BlockSpec((1, N_MOD, d), lambda bi, i: (bi % mb, 0, 0)), row(LANES),
                  pl.BlockSpec(memory_space=pl.ANY)],
        out_specs=row(d),
        scratch_shapes=[pltpu.VMEM((TOP_K * tm, d), F32), pltpu.SemaphoreType.DMA],
        compiler_params=_params(("arbitrary", "arbitrary"), 2 * tm * d * 4 + tm * LANES * 4,
                                TOP_K * tm * d * 4 + 4 * tm * d * 4),
        name="moe_combine",
    )(dest3, h, mod, route, y)


def _moe(parts, g, w_router, b_router, wg, wu, wd, bg, bu, bd, layer, bm):
    n_exp = w_router.shape[-1]
    d = parts[0][0].shape[-1]
    base = jnp.zeros((1, LANES), F32)
    hxs, routes = [], []
    for h, mod in parts:
        hx, route, base = _route(h, mod, g, w_router, b_router, base)
        hxs.append(hx)
        routes.append(route)
    counts = base[0, :n_exp].astype(jnp.int32)

    padded = (counts + bm - 1) // bm * bm
    pend = jnp.cumsum(padded)
    pstart = pend - padded
    n_tok = sum(h.shape[0] * h.shape[1] for h, _ in parts)
    n_rows = -(-(n_tok * TOP_K + n_exp * (bm - 1)) // bm) * bm
    nb = n_rows // bm
    blk_e = jnp.minimum(jnp.searchsorted(pend, jnp.arange(nb, dtype=jnp.int32) * bm, side="right"),
                        n_exp - 1).astype(jnp.int32)
    n_used = (pend[-1] // bm).astype(jnp.int32).reshape(1)

    dests = []
    xs = jnp.zeros((n_rows, d), F32)
    for hx, route in zip(hxs, routes):
        e = route[..., ROUTE_E:ROUTE_E + TOP_K].astype(jnp.int32)
        rank = route[..., ROUTE_R:ROUTE_R + TOP_K].astype(jnp.int32)
        dest = (pstart[e] + rank).reshape(-1, TOP_K)
        dests.append(dest)
        xs = _dispatch(hx.reshape(-1, d), dest, xs)

    y = _experts(blk_e, n_used, xs, wg, wu, wd, bg, bu, bd, layer, bm)
    return [_combine(h, mod, route, dest, y)
            for (h, mod), route, dest in zip(parts, routes, dests)]


def kernel(x, c, ctx, c_ctx, w_mod, b_mod, g_norm1, g_norm2, w_in, b_gate, g_qa, g_ka, lam_q1, lam_k1, lam_q2, lam_k2, g_sub, conv_w, conv_b, ln_g, ln_b, gn_g, gn_b, w_pa, w_pb, w_pc, w_out, w_router, b_router, w_gate_up, b_gate_up, w_down, b_down):
    bsz, n_x, d = x.shape
    n_c = ctx.shape[1]
    depth = w_mod.shape[0]
    da_dim = g_qa.shape[-1]
    dv = g_sub.shape[-1]
    heads = d // dv
    ret_kdim = gn_g.shape[-1] // 2
    n_exp = w_router.shape[-1]
    de = w_down.shape[-2]
    moe_bm = 512

    cols = {"d": d, "ret_qk": heads * ret_kdim}
    off = 0
    for name, width in (("aq", d), ("ak", d), ("av", d), ("conv", 2 * d), ("rq", heads * ret_kdim),
                        ("rk", heads * ret_kdim), ("rv", d), ("rgf", d), ("rgb", d), ("gate", 3 * d)):
        cols[name] = off
        off += width
    assert off == w_in.shape[-1]

    w_in_bf = w_in.astype(BF16)
    w_pa_bf, w_pb_bf, w_pc_bf, w_out_bf = (w.astype(BF16) for w in (w_pa, w_pb, w_pc, w_out))
    wg = w_gate_up[..., 0::2].astype(BF16)
    wu = w_gate_up[..., 1::2].astype(BF16)
    wd = w_down.astype(BF16)
    bg = b_gate_up[..., 0::2].reshape(depth, n_exp, 1, de)
    bu = b_gate_up[..., 1::2].reshape(depth, n_exp, 1, de)
    bd = b_down.reshape(depth, n_exp, 1, d)

    rows = -(-(bsz + 1) // SUBLANES) * SUBLANES
    cvecs = jnp.zeros((rows, d), F32).at[:bsz].set(c).at[bsz].set(c_ctx)
    mods = _modulation(cvecs, w_mod, b_mod).reshape(depth, rows, N_MOD, d)

    tabs_x = _rope_tables(n_x, da_dim, ret_kdim)
    tabs_c = tuple(jnp.zeros((n_c, LANES), F32) for _ in range(4))
    ret_tabs = _retention_tables(heads, ret_kdim, dv)
    s_zero = jnp.zeros((bsz, 2, heads // 2, LANES, dv), F32)

    h_ctx = ctx
    for l in range(depth):
        need_ctx = l < depth - 1
        lam_init = 0.8 - 0.6 * math.exp(-0.3 * l)
        lam = (jnp.exp(jnp.sum(lam_q1[l] * lam_k1[l])) - jnp.exp(jnp.sum(lam_q2[l] * lam_k2[l])) + lam_init)
        scal = jnp.stack([lam, jnp.asarray(1.0 - lam_init, F32)]).astype(F32)
        mod_x = mods[l, :bsz]
        mod_c = mods[l, bsz:bsz + 1]

        proj_x = _in_proj(x, mod_x, g_norm1[l], w_in_bf, l)
        proj_c = _in_proj(h_ctx, mod_c, g_norm1[l], w_in_bf, l)
        aq_x, ak_x, rq_x, rk_x = _qk_prep(proj_x, g_qa[l], g_ka[l], tabs_x, cols, rope=True,
                                          da_dim=da_dim, ret_kdim=ret_kdim)
        aq_c, ak_c, rq_c, rk_c = _qk_prep(proj_c, g_qa[l], g_ka[l], tabs_c, cols, rope=False,
                                          da_dim=da_dim, ret_kdim=ret_kdim)
        oa_x = _attention(scal, aq_x, ak_x, proj_x, ak_c, proj_c, g_sub[l], cols, with_x=True, da_dim=da_dim)
        ob_x = _conformer_conv(proj_x, conv_w[l], conv_b[l], ln_g[l], ln_b[l], cols)
        of_c, ob_c, s_ctx = _retention(rq_c, rk_c, proj_c, s_zero, ret_tabs, gn_g[l], gn_b[l], cols)
        of_x, obk_x, _ = _retention(rq_x, rk_x, proj_x, s_ctx, ret_tabs, gn_g[l], gn_b[l], cols)
        x = _merge(x, mod_x, oa_x, ob_x, of_x, obk_x, proj_x, b_gate[l],
                   w_pa_bf, w_pb_bf, w_pc_bf, w_out_bf, l, cols)
        if need_ctx:
            oa_c = _attention(scal, aq_c, ak_c, proj_c, ak_c, proj_c, g_sub[l], cols, with_x=False, da_dim=da_dim)
            cb_c = _conformer_conv(proj_c, conv_w[l], conv_b[l], ln_g[l], ln_b[l], cols)
            h_ctx = _merge(h_ctx, mod_c, oa_c, cb_c, of_c, ob_c, proj_c, b_gate[l],
                           w_pa_bf, w_pb_bf, w_pc_bf, w_out_bf, l, cols)

        parts = [(h_ctx, mod_c), (x, mod_x)] if need_ctx else [(x, mod_x)]
        outs = _moe(parts, g_norm2[l], w_router[l], b_router[l], wg, wu, wd, bg, bu, bd, l, moe_bm)
        if need_ctx:
            h_ctx, x = outs
        else:
            (x,) = outs
    return x
```

```python
import functools
import math

import jax
import jax.numpy as jnp
from jax import lax
from jax.experimental import pallas as pl
from jax.experimental.pallas import tpu as pltpu

F32 = jnp.float32
BF16 = jnp.bfloat16

GRID_W = 64
N_MOD = 6
EPS = 1e-6
ROPE_BASE = 10000.0
RET_CHUNK = 128
TOP_K = 4
SWIGLU_ALPHA = 1.702
SWIGLU_LIMIT = 7.0

LANES = 128
SUBLANES = 8
V7X_VMEM_BYTES = 64 * 2**20
VMEM_LIMIT_CAP = 56 * 2**20
MIB = 2**20

NEG_INF = float("-inf")


def _sigmoid(x):
    return 1.0 / (1.0 + jnp.exp(-x))


def _vmem_limit(pipelined_bytes, resident_bytes):
    need = 2 * pipelined_bytes + resident_bytes
    return int(min(max(need, 16 * MIB), VMEM_LIMIT_CAP))


def _params(sem, pipelined_bytes, resident_bytes):
    return pltpu.CompilerParams(dimension_semantics=sem,
                                vmem_limit_bytes=_vmem_limit(pipelined_bytes, resident_bytes))


def _lane_iota(shape):
    return lax.broadcasted_iota(jnp.int32, shape, len(shape) - 1)


def _mod_kernel(c_ref, w_ref, b_ref, o_ref):
    c = c_ref[...]
    a = (c * _sigmoid(c)).astype(BF16)
    o_ref[...] = jnp.dot(a, w_ref[...].astype(BF16), preferred_element_type=F32) + b_ref[...]


def _modulation(cvecs, w_mod, b_mod):
    depth, d, cols = w_mod.shape
    rows = cvecs.shape[0]
    tn = 1536
    assert cols % tn == 0
    return pl.pallas_call(
        _mod_kernel,
        out_shape=jax.ShapeDtypeStruct((depth, rows, cols), F32),
        grid=(depth, cols // tn),
        in_specs=[pl.BlockSpec((rows, d), lambda l, j: (0, 0)),
                  pl.BlockSpec((None, d, tn), lambda l, j: (l, 0, j)),
                  pl.BlockSpec((None, 1, tn), lambda l, j: (l, 0, j))],
        out_specs=pl.BlockSpec((None, rows, tn), lambda l, j: (l, 0, j)),
        compiler_params=_params(("arbitrary", "arbitrary"), d * tn * 4 + rows * tn * 4, 4 * MIB),
        name="modulation",
    )(cvecs, w_mod, b_mod.reshape(depth, 1, cols))


def _inproj_kernel(h_ref, mod_ref, g_ref, w_ref, o_ref, xs_ref):
    @pl.when(pl.program_id(2) == 0)
    def _():
        h = h_ref[0]
        y = h * lax.rsqrt(jnp.mean(h * h, axis=-1, keepdims=True) + EPS) * g_ref[...]
        xs_ref[...] = (y * (1.0 + mod_ref[0, 1:2, :]) + mod_ref[0, 0:1, :]).astype(BF16)

    o_ref[0] = jnp.dot(xs_ref[...], w_ref[...], preferred_element_type=F32)


def _in_proj(h, mod, g, w_in_bf, layer):
    b, n, d = h.shape
    cols = w_in_bf.shape[-1]
    tm = min(n, 1024)
    tn = 1024
    assert n % tm == 0 and cols % tn == 0
    mb = mod.shape[0]
    return pl.pallas_call(
        _inproj_kernel,
        out_shape=jax.ShapeDtypeStruct((b, n, cols), F32),
        grid=(b, n // tm, cols // tn),
        in_specs=[pl.BlockSpec((1, tm, d), lambda bi, i, j: (bi, i, 0)),
                  pl.BlockSpec((1, N_MOD, d), lambda bi, i, j: (bi % mb, 0, 0)),
                  pl.BlockSpec((1, d), lambda bi, i, j: (0, 0)),
                  pl.BlockSpec((None, d, tn), lambda bi, i, j: (layer, 0, j))],
        out_specs=pl.BlockSpec((1, tm, tn), lambda bi, i, j: (bi, i, j)),
        scratch_shapes=[pltpu.VMEM((tm, d), BF16)],
        compiler_params=_params(("arbitrary", "arbitrary", "arbitrary"),
                                tm * d * 4 + d * tn * 2 + tm * tn * 4, tm * d * 2 + 3 * tm * d * 4),
        name="in_proj",
    )(h, mod, g.reshape(1, d), w_in_bf)


def _prep_kernel(aq_ref, ak_ref, rq_ref, rk_ref, gq_ref, gk_ref, ca_ref, sa_ref, cr_ref, sr_ref,
                 oq_ref, ok_ref, orq_ref, ork_ref, *, rope, da_dim, ret_kdim):
    tm = aq_ref.shape[1]
    lane = _lane_iota((tm, LANES))
    lo = lane < da_dim

    def rms(x, g):
        x2 = x * x
        s_lo = jnp.sum(jnp.where(lo, x2, 0.0), axis=-1, keepdims=True)
        s_hi = jnp.sum(jnp.where(lo, 0.0, x2), axis=-1, keepdims=True)
        ms = jnp.where(lo, s_lo, s_hi) * (1.0 / da_dim)
        return x * lax.rsqrt(ms + EPS) * g

    def rot(x, c, s, half):
        first = (lane % (2 * half)) < half
        partner = jnp.where(first, pltpu.roll(x, LANES - half, 1), pltpu.roll(x, half, 1))
        return x * c + partner * s

    q_scale = da_dim ** -0.5
    k_scale = ret_kdim ** -0.5
    for j in range(aq_ref.shape[2] // LANES):
        sl = slice(j * LANES, (j + 1) * LANES)
        q = rms(aq_ref[0, :, sl], gq_ref[...])
        k = rms(ak_ref[0, :, sl], gk_ref[...])
        if rope:
            q = rot(q, ca_ref[...], sa_ref[...], da_dim // 4)
            k = rot(k, ca_ref[...], sa_ref[...], da_dim // 4)
        oq_ref[0, :, sl] = (q * q_scale).astype(BF16)
        ok_ref[0, :, sl] = k.astype(BF16)
    for j in range(rq_ref.shape[2] // LANES):
        sl = slice(j * LANES, (j + 1) * LANES)
        q = rq_ref[0, :, sl]
        k = rk_ref[0, :, sl]
        if rope:
            q = rot(q, cr_ref[...], sr_ref[...], ret_kdim // 2)
            k = rot(k, cr_ref[...], sr_ref[...], ret_kdim // 2)
        orq_ref[0, :, sl] = q
        ork_ref[0, :, sl] = k * k_scale


def _qk_prep(proj, g_q, g_k, tabs, cols, *, rope, da_dim, ret_kdim):
    b, n, _ = proj.shape
    d = cols["d"]
    rw = cols["ret_qk"]
    tm = min(n, 512)
    assert n % tm == 0
    gq = jnp.tile(g_q, LANES // da_dim).reshape(1, LANES)
    gk = jnp.tile(g_k, LANES // da_dim).reshape(1, LANES)
    tab_spec = pl.BlockSpec((tm, LANES), lambda bi, i: (i, 0))
    vec_spec = pl.BlockSpec((1, LANES), lambda bi, i: (0, 0))

    def col_spec(width, off):
        assert off % width == 0
        return pl.BlockSpec((1, tm, width), lambda bi, i: (bi, i, off // width))

    kern = functools.partial(_prep_kernel, rope=rope, da_dim=da_dim, ret_kdim=ret_kdim)
    return pl.pallas_call(
        kern,
        out_shape=(jax.ShapeDtypeStruct((b, n, d), BF16), jax.ShapeDtypeStruct((b, n, d), BF16),
                   jax.ShapeDtypeStruct((b, n, rw), F32), jax.ShapeDtypeStruct((b, n, rw), F32)),
        grid=(b, n // tm),
        in_specs=[col_spec(d, cols["aq"]), col_spec(d, cols["ak"]),
                  col_spec(rw, cols["rq"]), col_spec(rw, cols["rk"]),
                  vec_spec, vec_spec, tab_spec, tab_spec, tab_spec, tab_spec],
        out_specs=(pl.BlockSpec((1, tm, d), lambda bi, i: (bi, i, 0)),
                   pl.BlockSpec((1, tm, d), lambda bi, i: (bi, i, 0)),
                   pl.BlockSpec((1, tm, rw), lambda bi, i: (bi, i, 0)),
                   pl.BlockSpec((1, tm, rw), lambda bi, i: (bi, i, 0))),
        compiler_params=_params(("arbitrary", "arbitrary"),
                                tm * (2 * d + 2 * rw) * 4 + tm * (2 * d * 2 + 2 * rw * 4)
                                + 4 * tm * LANES * 4, 8 * MIB),
        name="qk_prep",
    )(proj, proj, proj, proj, gq, gk, *tabs)


def _rope_tables(n_tok, da_dim, ret_kdim):
    t = jnp.arange(n_tok)

    def angles(pos, dim):
        inv = ROPE_BASE ** (-jnp.arange(0, dim, 2, dtype=F32) / dim)
        return pos.astype(F32)[:, None] * inv[None, :]

    def cs(ang):
        c, s = jnp.cos(ang), jnp.sin(ang)
        return jnp.concatenate([c, c], axis=-1), jnp.concatenate([-s, s], axis=-1)

    half = da_dim // 2
    cr_, sr_ = cs(angles(t // GRID_W, half))
    cc_, sc_ = cs(angles(t % GRID_W, half))
    ca = jnp.tile(jnp.concatenate([cr_, cc_], axis=-1), (1, LANES // da_dim))
    sa = jnp.tile(jnp.concatenate([sr_, sc_], axis=-1), (1, LANES // da_dim))
    c1, s1 = cs(angles(t, ret_kdim))
    cr = jnp.tile(c1, (1, LANES // ret_kdim))
    sr = jnp.tile(s1, (1, LANES // ret_kdim))
    return ca, sa, cr, sr


ATTN_MAX_KEY_TILE = 768


def _attn_kernel(scal_ref, q_ref, kx_ref, vx_ref, kc_ref, vc_ref, gsub_ref, o_ref,
                 k_all, v_all, q2_ref, s0_ref, s1_ref, p0_ref, p1_ref, a0_ref, a1_ref, m_ref, acc_ref,
                 *, tk, with_x, da_dim):
    tq = q_ref.shape[1]
    dv = vc_ref.shape[2]
    nc = kc_ref.shape[1]
    nx = kx_ref.shape[1] if with_x else 0
    n_steps = (nx + nc) // tk

    @pl.when(pl.program_id(2) == 0)
    def _():
        k_all[nx:nx + nc, :] = kc_ref[0]
        v_all[nx:nx + nc, 0:dv] = vc_ref[0].astype(BF16)
        v_all[nx:nx + nc, dv:2 * dv] = jnp.ones((nc, dv), BF16)
        if with_x:
            rows = min(512, nx)

            def fill(j, c):
                off = pl.multiple_of(j * rows, rows)
                k_all[pl.ds(off, rows), :] = kx_ref[0, pl.ds(off, rows), :]
                v_all[pl.ds(off, rows), 0:dv] = vx_ref[0, pl.ds(off, rows), :].astype(BF16)
                v_all[pl.ds(off, rows), dv:2 * dv] = jnp.ones((rows, dv), BF16)
                return c

            lax.fori_loop(0, nx // rows, fill, 0)

    q = q_ref[0]
    lo = _lane_iota((tq, LANES)) < da_dim
    zero = jnp.zeros_like(q)
    q2_ref[0:tq, :] = jnp.where(lo, q, zero)
    q2_ref[tq:2 * tq, :] = jnp.where(lo, zero, q)
    m_ref[...] = jnp.full(m_ref.shape, NEG_INF, F32)
    acc_ref[...] = jnp.zeros(acc_ref.shape, F32)

    bufs = ((s0_ref, p0_ref, a0_ref), (s1_ref, p1_ref, a1_ref))

    def tile_start(j):
        return j * tk if isinstance(j, int) else pl.multiple_of(j * tk, tk)

    def qk(j, par):
        off = tile_start(j)
        bufs[par][0][...] = lax.dot_general(q2_ref[...], k_all[pl.ds(off, tk), :],
                                            (((1,), (1,)), ((), ())), preferred_element_type=F32)

    def softmax(par):
        s_ref, p_ref, a_ref = bufs[par]
        s = s_ref[...]
        m_prev = m_ref[...]
        m_new = jnp.maximum(m_prev, jnp.max(s, axis=-1, keepdims=True))
        a_ref[...] = jnp.exp(m_prev - m_new)
        p_ref[...] = jnp.exp(s - jnp.concatenate([m_new] * (tk // LANES), axis=1)).astype(BF16)
        m_ref[...] = m_new

    def pv(j, par):
        off = tile_start(j)
        _, p_ref, a_ref = bufs[par]
        alpha = jnp.concatenate([a_ref[...]] * (2 * dv // LANES), axis=1)
        acc_ref[...] = alpha * acc_ref[...] + jnp.dot(p_ref[...], v_all[pl.ds(off, tk), :],
                                                      preferred_element_type=F32)

    qk(0, 0)
    if n_steps >= 2:
        qk(1, 1)
        softmax(0)
        inner = n_steps - 2

        def pair(i, c):
            j = 1 + 2 * i
            qk(j + 1, 0)
            softmax(1)
            pv(j - 1, 0)
            qk(j + 2, 1)
            softmax(0)
            pv(j, 1)
            return c

        lax.fori_loop(0, inner // 2, pair, 0)
        j = 1 + 2 * (inner // 2)
        if inner % 2:
            qk(j + 1, (j + 1) % 2)
            softmax(j % 2)
            pv(j - 1, (j - 1) % 2)
        softmax((n_steps - 1) % 2)
        pv(n_steps - 2, (n_steps - 2) % 2)
    else:
        softmax(0)
    pv(n_steps - 1, (n_steps - 1) % 2)

    acc = acc_ref[...]
    o0 = acc[0:tq, 0:dv] / acc[0:tq, dv:dv + 1]
    o1 = acc[tq:2 * tq, 0:dv] / acc[tq:2 * tq, dv:dv + 1]
    o = o0 - scal_ref[0] * o1
    y = o * lax.rsqrt(jnp.mean(o * o, axis=-1, keepdims=True) + EPS) * gsub_ref[...]
    o_ref[0] = (y * scal_ref[1]).astype(BF16)


def _attention(scal, q, kx, proj_x, kc, proj_c, g_sub, cols, *, with_x, da_dim):
    b, nq, d = q.shape
    dv = g_sub.shape[0]
    heads = d // dv
    nc = kc.shape[1]
    nx = kx.shape[1]
    tq = min(nq, 256)
    n_keys = (nx if with_x else 0) + nc
    tk = max(t for t in range(LANES, ATTN_MAX_KEY_TILE + 1, LANES) if n_keys % t == 0)
    assert nq % tq == 0 and 2 * da_dim == LANES and dv == LANES
    v_blk = cols["av"] // dv
    kern = functools.partial(_attn_kernel, tk=tk, with_x=with_x, da_dim=da_dim)
    rows = 2 * tq
    pipelined = tq * dv * 2 * 2 + nx * dv * (2 + 4) + nc * dv * (2 + 4)
    resident = (n_keys * 3 * dv * 2 + rows * LANES * 2 + 2 * rows * tk * (4 + 2) + 3 * rows * LANES * 4
                + rows * 2 * dv * 4 + 3 * rows * tk * 4)
    return pl.pallas_call(
        kern,
        out_shape=jax.ShapeDtypeStruct((b, nq, d), BF16),
        grid=(b, heads, nq // tq),
        in_specs=[pl.BlockSpec(memory_space=pltpu.SMEM),
                  pl.BlockSpec((1, tq, dv), lambda bi, h, i: (bi, i, h)),
                  pl.BlockSpec((1, nx, dv), lambda bi, h, i: (bi, 0, h)),
                  pl.BlockSpec((1, nx, dv), lambda bi, h, i: (bi, 0, v_blk + h)),
                  pl.BlockSpec((1, nc, dv), lambda bi, h, i: (bi, 0, h)),
                  pl.BlockSpec((1, nc, dv), lambda bi, h, i: (bi, 0, v_blk + h)),
                  pl.BlockSpec((1, dv), lambda bi, h, i: (0, 0))],
        out_specs=pl.BlockSpec((1, tq, dv), lambda bi, h, i: (bi, i, h)),
        scratch_shapes=[pltpu.VMEM((n_keys, dv), BF16), pltpu.VMEM((n_keys, 2 * dv), BF16),
                        pltpu.VMEM((rows, LANES), BF16),
                        pltpu.VMEM((rows, tk), F32), pltpu.VMEM((rows, tk), F32),
                        pltpu.VMEM((rows, tk), BF16), pltpu.VMEM((rows, tk), BF16),
                        pltpu.VMEM((rows, LANES), F32), pltpu.VMEM((rows, LANES), F32),
                        pltpu.VMEM((rows, LANES), F32), pltpu.VMEM((rows, 2 * dv), F32)],
        compiler_params=_params(("arbitrary", "arbitrary", "arbitrary"), pipelined, resident),
        name="diff_attention" if with_x else "diff_attention_ctx",
    )(scal, q, kx, proj_x, kc, proj_c, g_sub.reshape(1, dv))


CONV_HALO = 16


def _conv_kernel(a_ref, g_ref, ap_ref, gp_ref, an_ref, gn_ref, w_ref, cb_ref, lg_ref, lb_ref,
                 o_ref, ext_ref, cv_ref):
    i = pl.program_id(1)
    last = pl.num_programs(1) - 1
    tm = a_ref.shape[1]
    ch = a_ref.shape[2]
    width = w_ref.shape[0]
    pad = width // 2

    def glu(a, g):
        return a * _sigmoid(g)

    ext_ref[0:CONV_HALO, :] = jnp.where(i > 0, glu(ap_ref[0], gp_ref[0]), 0.0)
    ext_ref[CONV_HALO:CONV_HALO + tm, :] = glu(a_ref[0], g_ref[0])
    ext_ref[CONV_HALO + tm:CONV_HALO + tm + CONV_HALO, :] = jnp.where(i < last, glu(an_ref[0], gn_ref[0]), 0.0)

    for c in range(ch // LANES):
        sl = slice(c * LANES, (c + 1) * LANES)
        acc = jnp.zeros((tm, LANES), F32)
        for j in range(width):
            acc = acc + w_ref[j:j + 1, sl] * ext_ref[CONV_HALO - pad + j:CONV_HALO - pad + j + tm, sl]
        cv_ref[:, sl] = acc + cb_ref[:, sl]

    v = cv_ref[...]
    mu = jnp.mean(v, axis=-1, keepdims=True)
    vc = v - mu
    var = jnp.mean(vc * vc, axis=-1, keepdims=True)
    y = vc * lax.rsqrt(var + EPS) * lg_ref[...] + lb_ref[...]
    o_ref[0] = (y * _sigmoid(y)).astype(BF16)


def _conformer_conv(proj, conv_w, conv_b, ln_g, ln_b, cols):
    b, n, _ = proj.shape
    width, ch = conv_w.shape
    tm = min(n, 256)
    assert n % tm == 0 and tm % CONV_HALO == 0 and width // 2 < CONV_HALO
    a_blk = cols["conv"] // ch
    g_blk = a_blk + 1
    r = tm // CONV_HALO
    n_halo = n // CONV_HALO

    def cur(blk):
        return pl.BlockSpec((1, tm, ch), lambda bi, i: (bi, i, blk))

    def prev(blk):
        return pl.BlockSpec((1, CONV_HALO, ch), lambda bi, i: (bi, jnp.maximum(i * r - 1, 0), blk))

    def nxt(blk):
        return pl.BlockSpec((1, CONV_HALO, ch), lambda bi, i: (bi, jnp.minimum((i + 1) * r, n_halo - 1), blk))

    vec = pl.BlockSpec((1, ch), lambda bi, i: (0, 0))
    return pl.pallas_call(
        _conv_kernel,
        out_shape=jax.ShapeDtypeStruct((b, n, ch), BF16),
        grid=(b, n // tm),
        in_specs=[cur(a_blk), cur(g_blk), prev(a_blk), prev(g_blk), nxt(a_blk), nxt(g_blk),
                  pl.BlockSpec((width, ch), lambda bi, i: (0, 0)), vec, vec, vec],
        out_specs=pl.BlockSpec((1, tm, ch), lambda bi, i: (bi, i, 0)),
        scratch_shapes=[pltpu.VMEM((tm + 2 * CONV_HALO, ch), F32), pltpu.VMEM((tm, ch), F32)],
        compiler_params=_params(("arbitrary", "arbitrary"),
                                (2 * tm + 4 * CONV_HALO) * ch * 4 + tm * ch * 2 + 40 * ch * 4,
                                (2 * tm + 2 * CONV_HALO) * ch * 4 + 4 * tm * ch * 4),
        name="conformer_conv",
    )(proj, proj, proj, proj, proj, proj, conv_w, conv_b.reshape(1, ch), ln_g.reshape(1, ch),
      ln_b.reshape(1, ch))


def _ret_kernel(qf_ref, kf_ref, vf_ref, gf_ref, qb_ref, kb_ref, vb_ref, gb_ref, s0_ref,
                inner_ref, xi_ref, zeta_ref, gc_ref, gng_ref, gnb_ref,
                of_ref, ob_ref, sfin_ref, s_ref, *, kdim):
    i = pl.program_id(1)
    c = qf_ref.shape[1]
    dv = gng_ref.shape[1]
    n_pair = qf_ref.shape[2] // LANES

    @pl.when(i == 0)
    def _():
        s_ref[...] = s0_ref[0]

    lo = _lane_iota((c, LANES)) < kdim
    dirs = ((qf_ref, kf_ref, vf_ref, gf_ref, of_ref), (qb_ref, kb_ref, vb_ref, gb_ref, ob_ref))
    for d, (q_ref, k_ref, v_ref, g_ref, o_ref) in enumerate(dirs):
        for p in range(n_pair):
            sl = slice(p * LANES, (p + 1) * LANES)
            q = q_ref[0, :, sl]
            k = k_ref[0, :, sl]
            kz = k * zeta_ref[d, p]
            kb16 = k.astype(BF16)
            s_pair = s_ref[d, p]
            s16 = s_pair.astype(BF16)
            upd = jnp.zeros((LANES, dv), F32)
            for hh in range(2):
                h = 2 * p + hh
                hsl = slice(h * dv, (h + 1) * dv)
                keep = lo if hh == 0 else jnp.logical_not(lo)
                qh = jnp.where(keep, q, 0.0).astype(BF16)
                v16 = v_ref[0, :, hsl].astype(BF16)
                att = lax.dot_general(qh, kb16, (((1,), (1,)), ((), ())), preferred_element_type=F32)
                att = (att * inner_ref[d, h]).astype(BF16)
                o = (jnp.dot(att, v16, preferred_element_type=F32)
                     + jnp.dot(qh, s16, preferred_element_type=F32) * xi_ref[d, h])
                kzh = jnp.where(keep, kz, 0.0).astype(BF16)
                upd = upd + lax.dot_general(kzh, v16, (((0,), (0,)), ((), ())), preferred_element_type=F32)
                mu = jnp.mean(o, axis=-1, keepdims=True)
                oc = o - mu
                var = jnp.mean(oc * oc, axis=-1, keepdims=True)
                y = oc * lax.rsqrt(var + EPS) * gng_ref[...] + gnb_ref[...]
                g = g_ref[0, :, hsl]
                o_ref[0, :, hsl] = (g * _sigmoid(g)) * y
            s_ref[d, p] = s_pair * gc_ref[p] + upd

    sfin_ref[0] = s_ref[...]


def _retention(rq, rk, proj, s0, tabs, gn_g, gn_b, cols):
    b, n, rw = rq.shape
    dv = gn_g.shape[0]
    d = cols["d"]
    c = RET_CHUNK
    nch = n // c
    kdim = dv // 2
    assert n % c == 0 and c == LANES and 2 * kdim == LANES
    n_pair = rw // LANES
    inner, xi, zeta, gc = tabs
    v_blk, gf_blk, gb_blk = cols["rv"] // d, cols["rgf"] // d, cols["rgb"] // d

    def fwd(width, blk):
        return pl.BlockSpec((1, c, width), lambda bi, i: (bi, i, blk))

    def bwd(width, blk):
        return pl.BlockSpec((1, c, width), lambda bi, i: (bi, nch - 1 - i, blk))

    def whole(a):
        return pl.BlockSpec(a.shape, lambda bi, i: (0,) * a.ndim)

    state = pl.BlockSpec((1, 2, n_pair, LANES, dv), lambda bi, i: (bi, 0, 0, 0, 0))
    vec = pl.BlockSpec((1, dv), lambda bi, i: (0, 0))
    kern = functools.partial(_ret_kernel, kdim=kdim)
    pipelined = 2 * c * (2 * rw + 2 * d) * 4 + 2 * c * d * 4 + 2 * 2 * n_pair * LANES * dv * 4
    resident = (inner.size + xi.size + zeta.size + gc.size) * 4 * 2 + 2 * n_pair * LANES * dv * 4 + 8 * MIB
    return pl.pallas_call(
        kern,
        out_shape=(jax.ShapeDtypeStruct((b, n, d), F32), jax.ShapeDtypeStruct((b, n, d), F32),
                   jax.ShapeDtypeStruct((b, 2, n_pair, LANES, dv), F32)),
        grid=(b, nch),
        in_specs=[fwd(rw, 0), fwd(rw, 0), fwd(d, v_blk), fwd(d, gf_blk),
                  bwd(rw, 0), bwd(rw, 0), bwd(d, v_blk), bwd(d, gb_blk),
                  state, whole(inner), whole(xi), whole(zeta), whole(gc), vec, vec],
        out_specs=(pl.BlockSpec((1, c, d), lambda bi, i: (bi, i, 0)),
                   pl.BlockSpec((1, c, d), lambda bi, i: (bi, nch - 1 - i, 0)),
                   state),
        scratch_shapes=[pltpu.VMEM((2, n_pair, LANES, dv), F32)],
        compiler_params=_params(("arbitrary", "arbitrary"), pipelined, resident),
        name="retention",
    )(rq, rk, proj, proj, rq, rk, proj, proj, s0, inner, xi, zeta, gc,
      gn_g.reshape(1, dv), gn_b.reshape(1, dv))


def _retention_tables(heads, kdim, dv):
    c = RET_CHUNK
    lg = jnp.log1p(-jnp.exp2(-5.0 - jnp.arange(heads, dtype=F32)))
    pos = jnp.arange(c, dtype=F32)
    diff = pos[:, None] - pos[None, :]
    inner_f = jnp.where(diff[None] >= 0, jnp.exp(jnp.maximum(diff, 0.0)[None] * lg[:, None, None]), 0.0)
    inner = jnp.stack([inner_f, jnp.swapaxes(inner_f, 1, 2)])
    xi_f = jnp.exp((pos + 1.0)[None, :] * lg[:, None])
    zeta_f = jnp.exp((c - 1.0 - pos)[None, :] * lg[:, None])
    xi = jnp.stack([xi_f, xi_f[:, ::-1]])
    zeta = jnp.stack([zeta_f, zeta_f[:, ::-1]])
    xi = jnp.broadcast_to(xi[..., None], (2, heads, c, dv))
    zeta = jnp.broadcast_to(zeta[..., None], (2, heads, c, kdim))
    zeta = zeta.reshape(2, heads // 2, 2, c, kdim).transpose(0, 1, 3, 2, 4).reshape(2, heads // 2, c, 2 * kdim)
    gc = jnp.exp(c * lg)
    gc = jnp.broadcast_to(gc[:, None, None], (heads, kdim, dv)).reshape(heads // 2, 2 * kdim, dv)
    return inner.astype(F32), xi.astype(F32), zeta.astype(F32), gc.astype(F32)


def _merge_kernel(h_ref, mod_ref, oa_ref, ob_ref, of_ref, obk_ref, gate_ref, bg_ref,
                  wa_ref, wb_ref, wc_ref, wo_ref, o_ref):
    d = h_ref.shape[2]
    g = _sigmoid(gate_ref[0] + bg_ref[...])
    oc = (of_ref[0] + obk_ref[0]).astype(BF16)
    y = g[:, 0:d] * jnp.dot(oa_ref[0], wa_ref[...], preferred_element_type=F32)
    y = y + g[:, d:2 * d] * jnp.dot(ob_ref[0], wb_ref[...], preferred_element_type=F32)
    y = y + g[:, 2 * d:3 * d] * jnp.dot(oc, wc_ref[...], preferred_element_type=F32)
    m = jnp.dot(y.astype(BF16), wo_ref[...], preferred_element_type=F32)
    o_ref[0] = h_ref[0] + mod_ref[0, 2:3, :] * m


def _merge(h, mod, oa, ob, of, obk, proj, b_gate, w_pa, w_pb, w_pc, w_out, layer, cols):
    b, n, d = h.shape
    tm = min(n, 512)
    assert n % tm == 0
    mb = mod.shape[0]
    gate_blk = cols["gate"] // (3 * d)
    row = lambda width: pl.BlockSpec((1, tm, width), lambda bi, i: (bi, i, 0))
    wspec = pl.BlockSpec((None, d, d), lambda bi, i: (layer, 0, 0))
    return pl.pallas_call(
        _merge_kernel,
        out_shape=jax.ShapeDtypeStruct((b, n, d), F32),
        grid=(b, n // tm),
        in_specs=[row(d), pl.BlockSpec((1, N_MOD, d), lambda bi, i: (bi % mb, 0, 0)),
                  row(d), row(d), row(d), row(d),
                  pl.BlockSpec((1, tm, 3 * d), lambda bi, i: (bi, i, gate_blk)),
                  pl.BlockSpec((1, 3 * d), lambda bi, i: (0, 0)),
                  wspec, wspec, wspec, wspec],
        out_specs=row(d),
        compiler_params=_params(("arbitrary", "arbitrary"),
                                tm * d * (4 + 2 + 2 + 4 + 4 + 12 + 4) + 4 * d * d * 2, 10 * tm * d * 4),
        name="merge",
    )(h, mod, oa, ob, of, obk, proj, b_gate.reshape(1, 3 * d), w_pa, w_pb, w_pc, w_out)


ROUTE_E, ROUTE_G, ROUTE_R = 0, 4, 8


def _route_kernel(h_ref, mod_ref, g_ref, wr_ref, br_ref, tri_ref, base_ref,
                  hx_ref, route_ref, cnt_ref, run_ref, *, n_exp):
    first = jnp.logical_and(pl.program_id(0) == 0, pl.program_id(1) == 0)

    @pl.when(first)
    def _():
        run_ref[...] = base_ref[...]

    h = h_ref[0]
    tm = h.shape[0]
    y = h * lax.rsqrt(jnp.mean(h * h, axis=-1, keepdims=True) + EPS) * g_ref[...]
    hx = y * (1.0 + mod_ref[0, 4:5, :]) + mod_ref[0, 3:4, :]
    hx_ref[0] = hx

    lane = _lane_iota((tm, LANES))
    lane_f = lane.astype(F32)
    logits = jnp.dot(hx, wr_ref[...], preferred_element_type=F32,
                     precision=lax.Precision.HIGHEST) + br_ref[...]
    logits = jnp.where(lane < n_exp, logits, NEG_INF)

    sels, vals = [], []
    for _ in range(TOP_K):
        mx = jnp.max(logits, axis=-1, keepdims=True)
        idx = jnp.min(jnp.where(logits == mx, lane_f, float(LANES)), axis=-1, keepdims=True)
        sel = lane_f == idx
        sels.append((sel, idx))
        vals.append(mx)
        logits = jnp.where(sel, NEG_INF, logits)

    es = [jnp.exp(v - vals[0]) for v in vals]
    denom = es[0]
    for e in es[1:]:
        denom = denom + e

    onehot = jnp.zeros((tm, LANES), F32)
    for sel, _ in sels:
        onehot = onehot + jnp.where(sel, 1.0, 0.0)
    before = run_ref[...] + jnp.dot(tri_ref[...], onehot.astype(BF16), preferred_element_type=F32)

    route = jnp.zeros((tm, LANES), F32)
    for k, (sel, idx) in enumerate(sels):
        rank = jnp.sum(jnp.where(sel, before, 0.0), axis=-1, keepdims=True)
        route = jnp.where(lane == ROUTE_E + k, idx, route)
        route = jnp.where(lane == ROUTE_G + k, es[k] / denom, route)
        route = jnp.where(lane == ROUTE_R + k, rank, route)
    route_ref[0] = route

    run_ref[...] = run_ref[...] + jnp.sum(onehot, axis=0, keepdims=True)
    cnt_ref[...] = run_ref[...]


def _route(h, mod, g, w_router, b_router, base):
    b, n, d = h.shape
    n_exp = w_router.shape[-1]
    tm = min(n, 256)
    assert n % tm == 0 and n_exp <= LANES
    mb = mod.shape[0]
    wr = jnp.zeros((d, LANES), F32).at[:, :n_exp].set(w_router)
    br = jnp.zeros((1, LANES), F32).at[0, :n_exp].set(b_router)
    tri = (jnp.arange(tm)[:, None] > jnp.arange(tm)[None, :]).astype(BF16)
    row = lambda width: pl.BlockSpec((1, tm, width), lambda bi, i: (bi, i, 0))
    const = lambda shape: pl.BlockSpec(shape, lambda bi, i: (0,) * len(shape))
    kern = functools.partial(_route_kernel, n_exp=n_exp)
    return pl.pallas_call(
        kern,
        out_shape=(jax.ShapeDtypeStruct((b, n, d), F32), jax.ShapeDtypeStruct((b, n, LANES), F32),
                   jax.ShapeDtypeStruct((1, LANES), F32)),
        grid=(b, n // tm),
        in_specs=[row(d), pl.BlockSpec((1, N_MOD, d), lambda bi, i: (bi % mb, 0, 0)), const((1, d)),
                  const((d, LANES)), const((1, LANES)), const((tm, tm)), const((1, LANES))],
        out_specs=(row(d), row(LANES), const((1, LANES))),
        scratch_shapes=[pltpu.VMEM((1, LANES), F32)],
        compiler_params=_params(("arbitrary", "arbitrary"),
                                2 * tm * d * 4 + tm * LANES * 4 + d * LANES * 4 + tm * tm * 2, 8 * tm * d * 4),
        name="moe_route",
    )(h, mod, g.reshape(1, d), wr, br, tri, base)


DMA_ISSUE_UNROLL = 8


def _tile_dest(dest, tm):
    t = dest.shape[0]
    return dest.reshape(t // tm, tm, TOP_K).transpose(0, 2, 1).reshape(t // tm, 1, TOP_K * tm)


def _dispatch_kernel(dest_ref, hx_ref, xs_in_ref, xs_ref, sem):
    del xs_in_ref
    tm = hx_ref.shape[0]

    def copy(k, t):
        return pltpu.make_async_copy(hx_ref.at[pl.ds(t, 1)],
                                     xs_ref.at[pl.ds(dest_ref[0, 0, k * tm + t], 1)], sem)

    for k in range(TOP_K):
        def start(t, c, k=k):
            copy(k, t).start()
            return c

        lax.fori_loop(0, tm, start, 0, unroll=DMA_ISSUE_UNROLL)
    for k in range(TOP_K):
        def wait(t, c, k=k):
            copy(k, t).wait()
            return c

        lax.fori_loop(0, tm, wait, 0, unroll=DMA_ISSUE_UNROLL)


def _dispatch(hx_flat, dest, xs):
    t, d = hx_flat.shape
    tm = min(t, 256)
    assert t % tm == 0
    dest3 = _tile_dest(dest, tm)
    return pl.pallas_call(
        _dispatch_kernel,
        out_shape=jax.ShapeDtypeStruct(xs.shape, xs.dtype),
        grid=(t // tm,),
        in_specs=[pl.BlockSpec((1, 1, tm * TOP_K), lambda i: (i, 0, 0), memory_space=pltpu.SMEM),
                  pl.BlockSpec((tm, d), lambda i: (i, 0)),
                  pl.BlockSpec(memory_space=pl.ANY)],
        out_specs=pl.BlockSpec(memory_space=pl.ANY),
        scratch_shapes=[pltpu.SemaphoreType.DMA],
        input_output_aliases={2: 0},
        compiler_params=pltpu.CompilerParams(dimension_semantics=("arbitrary",),
                                             vmem_limit_bytes=_vmem_limit(tm * d * 4, 4 * MIB),
                                             has_side_effects=True),
        name="moe_dispatch",
    )(dest3, hx_flat, xs)


def _split_kernel(w_ref, sel_ref, g_ref, u_ref):
    de = g_ref.shape[-1]
    r = jnp.dot(w_ref[...].astype(BF16), sel_ref[...], preferred_element_type=F32)
    g_ref[...] = r[:, :de].astype(BF16)
    u_ref[...] = r[:, de:].astype(BF16)


def _split_gate_up(w_gate_up):
    depth, n_exp, d, de2 = w_gate_up.shape
    de = de2 // 2
    tr = min(d, 512)
    assert d % tr == 0
    order = jnp.concatenate([jnp.arange(0, de2, 2), jnp.arange(1, de2, 2)])
    sel = (jnp.arange(de2)[:, None] == order[None, :]).astype(BF16)
    out = jax.ShapeDtypeStruct((depth, n_exp, d, de), BF16)
    ospec = pl.BlockSpec((None, None, tr, de), lambda l, e, r: (l, e, r, 0))
    return pl.pallas_call(
        _split_kernel,
        out_shape=(out, out),
        grid=(depth, n_exp, d // tr),
        in_specs=[pl.BlockSpec((None, None, tr, de2), lambda l, e, r: (l, e, r, 0)),
                  pl.BlockSpec((de2, de2), lambda l, e, r: (0, 0))],
        out_specs=(ospec, ospec),
        compiler_params=_params(("arbitrary", "arbitrary", "arbitrary"),
                                tr * de2 * 4 + de2 * de2 * 2 + 2 * tr * de * 2, tr * de2 * (2 + 4 + 4)),
        name="split_gate_up",
    )(w_gate_up, sel)


def _expert_kernel(blk_e_ref, n_used_ref, x_ref, wg_ref, wu_ref, wd_ref, bg_ref, bu_ref, bd_ref, y_ref):
    del blk_e_ref
    used = pl.program_id(0) < n_used_ref[0]

    @pl.when(used)
    def _():
        x = x_ref[...].astype(BF16)
        gl = jnp.dot(x, wg_ref[...], preferred_element_type=F32) + bg_ref[...]
        up = jnp.dot(x, wu_ref[...], preferred_element_type=F32) + bu_ref[...]
        gl = jnp.minimum(gl, SWIGLU_LIMIT)
        up = jnp.clip(up, -SWIGLU_LIMIT, SWIGLU_LIMIT)
        act = (up + 1.0) * (gl * _sigmoid(SWIGLU_ALPHA * gl))
        y_ref[...] = jnp.dot(act.astype(BF16), wd_ref[...], preferred_element_type=F32) + bd_ref[...]

    @pl.when(jnp.logical_not(used))
    def _():
        y_ref[...] = jnp.zeros(y_ref.shape, F32)


def _experts(blk_e, n_used, xs, wg, wu, wd, bg, bu, bd, layer, bm):
    n_rows, d = xs.shape
    de = wg.shape[-1]
    nb = n_rows // bm
    wspec = lambda a, c: pl.BlockSpec((None, None, a, c), lambda i, be, nu: (layer, be[i], 0, 0))
    return pl.pallas_call(
        _expert_kernel,
        out_shape=jax.ShapeDtypeStruct((n_rows, d), F32),
        grid_spec=pltpu.PrefetchScalarGridSpec(
            num_scalar_prefetch=2,
            grid=(nb,),
            in_specs=[pl.BlockSpec((bm, d), lambda i, be, nu: (i, 0)),
                      wspec(d, de), wspec(d, de), wspec(de, d),
                      wspec(1, de), wspec(1, de), wspec(1, d)],
            out_specs=pl.BlockSpec((bm, d), lambda i, be, nu: (i, 0))),
        compiler_params=_params(("arbitrary",), 2 * bm * d * 4 + 3 * d * de * 2, 8 * bm * de * 4),
        name="moe_experts",
    )(blk_e, n_used, xs, wg, wu, wd, bg, bu, bd)


def _combine_kernel(dest_ref, h_ref, mod_ref, route_ref, y_ref, o_ref, ybuf, sem):
    tm = h_ref.shape[1]

    def copy(r):
        return pltpu.make_async_copy(y_ref.at[pl.ds(dest_ref[0, 0, r], 1)], ybuf.at[pl.ds(r, 1)], sem)

    def start(r, c):
        copy(r).start()
        return c

    def wait(r, c):
        copy(r).wait()
        return c

    lax.fori_loop(0, TOP_K * tm, start, 0, unroll=DMA_ISSUE_UNROLL)
    lax.fori_loop(0, TOP_K * tm, wait, 0, unroll=DMA_ISSUE_UNROLL)

    route = route_ref[0]
    f = jnp.zeros((tm, h_ref.shape[2]), F32)
    for k in range(TOP_K):
        f = f + route[:, ROUTE_G + k:ROUTE_G + k + 1] * ybuf[k * tm:(k + 1) * tm, :]
    o_ref[0] = h_ref[0] + mod_ref[0, 5:6, :] * f


def _combine(h, mod, route, dest, y):
    b, n, d = h.shape
    tm = min(n, 256)
    assert n % tm == 0
    mb = mod.shape[0]
    nt = n // tm
    dest3 = _tile_dest(dest, tm)
    row = lambda width: pl.BlockSpec((1, tm, width), lambda bi, i: (bi, i, 0))
    return pl.pallas_call(
        _combine_kernel,
        out_shape=jax.ShapeDtypeStruct((b, n, d), F32),
        grid=(b, nt),
        in_specs=[pl.BlockSpec((1, 1, tm * TOP_K), lambda bi, i: (bi * nt + i, 0, 0), memory_space=pltpu.SMEM),
                  row(d), pl.BlockSpec((1, N_MOD, d), lambda bi, i: (bi % mb, 0, 0)), row(LANES),
                  pl.BlockSpec(memory_space=pl.ANY)],
        out_specs=row(d),
        scratch_shapes=[pltpu.VMEM((TOP_K * tm, d), F32), pltpu.SemaphoreType.DMA],
        compiler_params=_params(("arbitrary", "arbitrary"), 2 * tm * d * 4 + tm * LANES * 4,
                                TOP_K * tm * d * 4 + 4 * tm * d * 4),
        name="moe_combine",
    )(dest3, h, mod, route, y)


def _moe(parts, g, w_router, b_router, wg, wu, wd, bg, bu, bd, layer, bm):
    n_exp = w_router.shape[-1]
    d = parts[0][0].shape[-1]
    base = jnp.zeros((1, LANES), F32)
    hxs, routes = [], []
    for h, mod in parts:
        hx, route, base = _route(h, mod, g, w_router, b_router, base)
        hxs.append(hx)
        routes.append(route)
    counts = base[0, :n_exp].astype(jnp.int32)

    padded = (counts + bm - 1) // bm * bm
    pend = jnp.cumsum(padded)
    pstart = pend - padded
    n_tok = sum(h.shape[0] * h.shape[1] for h, _ in parts)
    n_rows = -(-(n_tok * TOP_K + n_exp * (bm - 1)) // bm) * bm
    nb = n_rows // bm
    blk_start = jnp.arange(nb, dtype=jnp.int32) * bm
    blk_e = jnp.minimum(jnp.sum((pend[None, :] <= blk_start[:, None]).astype(jnp.int32), axis=1), n_exp - 1)
    n_used = (pend[-1] // bm).astype(jnp.int32).reshape(1)
    eids = jnp.arange(n_exp, dtype=jnp.int32)

    dests = []
    xs = jnp.zeros((n_rows, d), F32)
    for hx, route in zip(hxs, routes):
        e = route[..., ROUTE_E:ROUTE_E + TOP_K].astype(jnp.int32)
        rank = route[..., ROUTE_R:ROUTE_R + TOP_K].astype(jnp.int32)
        start = jnp.sum(jnp.where(e[..., None] == eids, pstart, 0), axis=-1)
        dest = (start + rank).reshape(-1, TOP_K)
        dests.append(dest)
        xs = _dispatch(hx.reshape(-1, d), dest, xs)

    y = _experts(blk_e, n_used, xs, wg, wu, wd, bg, bu, bd, layer, bm)
    return [_combine(h, mod, route, dest, y)
            for (h, mod), route, dest in zip(parts, routes, dests)]


def kernel(x, c, ctx, c_ctx, w_mod, b_mod, g_norm1, g_norm2, w_in, b_gate, g_qa, g_ka, lam_q1, lam_k1, lam_q2, lam_k2, g_sub, conv_w, conv_b, ln_g, ln_b, gn_g, gn_b, w_pa, w_pb, w_pc, w_out, w_router, b_router, w_gate_up, b_gate_up, w_down, b_down):
    bsz, n_x, d = x.shape
    n_c = ctx.shape[1]
    depth = w_mod.shape[0]
    da_dim = g_qa.shape[-1]
    dv = g_sub.shape[-1]
    heads = d // dv
    ret_kdim = gn_g.shape[-1] // 2
    n_exp = w_router.shape[-1]
    de = w_down.shape[-2]
    moe_bm = 512

    cols = {"d": d, "ret_qk": heads * ret_kdim}
    off = 0
    for name, width in (("aq", d), ("ak", d), ("av", d), ("conv", 2 * d), ("rq", heads * ret_kdim),
                        ("rk", heads * ret_kdim), ("rv", d), ("rgf", d), ("rgb", d), ("gate", 3 * d)):
        cols[name] = off
        off += width
    assert off == w_in.shape[-1]

    w_in_bf = w_in.astype(BF16)
    w_pa_bf, w_pb_bf, w_pc_bf, w_out_bf = (w.astype(BF16) for w in (w_pa, w_pb, w_pc, w_out))
    wg, wu = _split_gate_up(w_gate_up)
    wd = w_down.astype(BF16)
    bg = b_gate_up[..., 0::2].reshape(depth, n_exp, 1, de)
    bu = b_gate_up[..., 1::2].reshape(depth, n_exp, 1, de)
    bd = b_down.reshape(depth, n_exp, 1, d)

    rows = -(-(bsz + 1) // SUBLANES) * SUBLANES
    cvecs = jnp.zeros((rows, d), F32).at[:bsz].set(c).at[bsz].set(c_ctx)
    mods = _modulation(cvecs, w_mod, b_mod).reshape(depth, rows, N_MOD, d)

    tabs_x = _rope_tables(n_x, da_dim, ret_kdim)
    tabs_c = tuple(jnp.zeros((n_c, LANES), F32) for _ in range(4))
    ret_tabs = _retention_tables(heads, ret_kdim, dv)
    s_zero = jnp.zeros((bsz, 2, heads // 2, LANES, dv), F32)

    h_ctx = ctx
    for l in range(depth):
        need_ctx = l < depth - 1
        lam_init = 0.8 - 0.6 * math.exp(-0.3 * l)
        lam = (jnp.exp(jnp.sum(lam_q1[l] * lam_k1[l])) - jnp.exp(jnp.sum(lam_q2[l] * lam_k2[l])) + lam_init)
        scal = jnp.stack([lam, jnp.asarray(1.0 - lam_init, F32)]).astype(F32)
        mod_x = mods[l, :bsz]
        mod_c = mods[l, bsz:bsz + 1]

        proj_x = _in_proj(x, mod_x, g_norm1[l], w_in_bf, l)
        proj_c = _in_proj(h_ctx, mod_c, g_norm1[l], w_in_bf, l)
        aq_x, ak_x, rq_x, rk_x = _qk_prep(proj_x, g_qa[l], g_ka[l], tabs_x, cols, rope=True,
                                          da_dim=da_dim, ret_kdim=ret_kdim)
        aq_c, ak_c, rq_c, rk_c = _qk_prep(proj_c, g_qa[l], g_ka[l], tabs_c, cols, rope=False,
                                          da_dim=da_dim, ret_kdim=ret_kdim)
        oa_x = _attention(scal, aq_x, ak_x, proj_x, ak_c, proj_c, g_sub[l], cols, with_x=True, da_dim=da_dim)
        ob_x = _conformer_conv(proj_x, conv_w[l], conv_b[l], ln_g[l], ln_b[l], cols)
        of_c, ob_c, s_ctx = _retention(rq_c, rk_c, proj_c, s_zero, ret_tabs, gn_g[l], gn_b[l], cols)
        of_x, obk_x, _ = _retention(rq_x, rk_x, proj_x, s_ctx, ret_tabs, gn_g[l], gn_b[l], cols)
        x = _merge(x, mod_x, oa_x, ob_x, of_x, obk_x, proj_x, b_gate[l],
                   w_pa_bf, w_pb_bf, w_pc_bf, w_out_bf, l, cols)
        if need_ctx:
            oa_c = _attention(scal, aq_c, ak_c, proj_c, ak_c, proj_c, g_sub[l], cols, with_x=False, da_dim=da_dim)
            cb_c = _conformer_conv(proj_c, conv_w[l], conv_b[l], ln_g[l], ln_b[l], cols)
            h_ctx = _merge(h_ctx, mod_c, oa_c, cb_c, of_c, ob_c, proj_c, b_gate[l],
                           w_pa_bf, w_pb_bf, w_pc_bf, w_out_bf, l, cols)

        parts = [(h_ctx, mod_c), (x, mod_x)] if need_ctx else [(x, mod_x)]
        outs = _moe(parts, g_norm2[l], w_router[l], b_router[l], wg, wu, wd, bg, bu, bd, l, moe_bm)
        if need_ctx:
            h_ctx, x = outs
        else:
            (x,) = outs
    return x
```

```python
import functools
import math

import jax
import jax.numpy as jnp
from jax import lax
from jax.experimental import pallas as pl
from jax.experimental.pallas import tpu as pltpu

F32 = jnp.float32
BF16 = jnp.bfloat16

GRID_W = 64
N_MOD = 6
EPS = 1e-6
ROPE_BASE = 10000.0
RET_CHUNK = 128
TOP_K = 4
SWIGLU_ALPHA = 1.702
SWIGLU_LIMIT = 7.0

LANES = 128
SUBLANES = 8
V7X_VMEM_BYTES = 64 * 2**20
VMEM_LIMIT_CAP = 56 * 2**20
MIB = 2**20

NEG_INF = float("-inf")


def _sigmoid(x):
    return 1.0 / (1.0 + jnp.exp(-x))


def _vmem_limit(pipelined_bytes, resident_bytes):
    need = 2 * pipelined_bytes + resident_bytes
    return int(min(max(need, 16 * MIB), VMEM_LIMIT_CAP))


def _params(sem, pipelined_bytes, resident_bytes):
    return pltpu.CompilerParams(dimension_semantics=sem,
                                vmem_limit_bytes=_vmem_limit(pipelined_bytes, resident_bytes))


def _lane_iota(shape):
    return lax.broadcasted_iota(jnp.int32, shape, len(shape) - 1)


def _mod_kernel(c_ref, w_ref, b_ref, o_ref):
    c = c_ref[...]
    a = (c * _sigmoid(c)).astype(BF16)
    o_ref[...] = jnp.dot(a, w_ref[...].astype(BF16), preferred_element_type=F32) + b_ref[...]


def _modulation(cvecs, w_mod, b_mod):
    depth, d, cols = w_mod.shape
    rows = cvecs.shape[0]
    tn = 1536
    assert cols % tn == 0
    return pl.pallas_call(
        _mod_kernel,
        out_shape=jax.ShapeDtypeStruct((depth, rows, cols), F32),
        grid=(depth, cols // tn),
        in_specs=[pl.BlockSpec((rows, d), lambda l, j: (0, 0)),
                  pl.BlockSpec((None, d, tn), lambda l, j: (l, 0, j)),
                  pl.BlockSpec((None, 1, tn), lambda l, j: (l, 0, j))],
        out_specs=pl.BlockSpec((None, rows, tn), lambda l, j: (l, 0, j)),
        compiler_params=_params(("arbitrary", "arbitrary"), d * tn * 4 + rows * tn * 4, 4 * MIB),
        name="modulation",
    )(cvecs, w_mod, b_mod.reshape(depth, 1, cols))


def _inproj_kernel(h_ref, mod_ref, g_ref, w_ref, o_ref, xs_ref):
    @pl.when(pl.program_id(2) == 0)
    def _():
        h = h_ref[0]
        y = h * lax.rsqrt(jnp.mean(h * h, axis=-1, keepdims=True) + EPS) * g_ref[...]
        xs_ref[...] = (y * (1.0 + mod_ref[0, 1:2, :]) + mod_ref[0, 0:1, :]).astype(BF16)

    o_ref[0] = jnp.dot(xs_ref[...], w_ref[...], preferred_element_type=F32)


def _in_proj(h, mod, g, w_in_bf, layer):
    b, n, d = h.shape
    cols = w_in_bf.shape[-1]
    tm = min(n, 1024)
    tn = 1024
    assert n % tm == 0 and cols % tn == 0
    mb = mod.shape[0]
    return pl.pallas_call(
        _inproj_kernel,
        out_shape=jax.ShapeDtypeStruct((b, n, cols), F32),
        grid=(b, n // tm, cols // tn),
        in_specs=[pl.BlockSpec((1, tm, d), lambda bi, i, j: (bi, i, 0)),
                  pl.BlockSpec((1, N_MOD, d), lambda bi, i, j: (bi % mb, 0, 0)),
                  pl.BlockSpec((1, d), lambda bi, i, j: (0, 0)),
                  pl.BlockSpec((None, d, tn), lambda bi, i, j: (layer, 0, j))],
        out_specs=pl.BlockSpec((1, tm, tn), lambda bi, i, j: (bi, i, j)),
        scratch_shapes=[pltpu.VMEM((tm, d), BF16)],
        compiler_params=_params(("arbitrary", "arbitrary", "arbitrary"),
                                tm * d * 4 + d * tn * 2 + tm * tn * 4, tm * d * 2 + 3 * tm * d * 4),
        name="in_proj",
    )(h, mod, g.reshape(1, d), w_in_bf)


def _prep_kernel(aq_ref, ak_ref, rq_ref, rk_ref, gq_ref, gk_ref, ca_ref, sa_ref, cr_ref, sr_ref,
                 oq_ref, ok_ref, orq_ref, ork_ref, *, rope, da_dim, ret_kdim):
    tm = aq_ref.shape[1]
    lane = _lane_iota((tm, LANES))
    lo = lane < da_dim

    def rms(x, g):
        x2 = x * x
        s_lo = jnp.sum(jnp.where(lo, x2, 0.0), axis=-1, keepdims=True)
        s_hi = jnp.sum(jnp.where(lo, 0.0, x2), axis=-1, keepdims=True)
        ms = jnp.where(lo, s_lo, s_hi) * (1.0 / da_dim)
        return x * lax.rsqrt(ms + EPS) * g

    def rot(x, c, s, half):
        first = (lane % (2 * half)) < half
        partner = jnp.where(first, pltpu.roll(x, LANES - half, 1), pltpu.roll(x, half, 1))
        return x * c + partner * s

    q_scale = da_dim ** -0.5
    k_scale = ret_kdim ** -0.5
    for j in range(aq_ref.shape[2] // LANES):
        sl = slice(j * LANES, (j + 1) * LANES)
        q = rms(aq_ref[0, :, sl], gq_ref[...])
        k = rms(ak_ref[0, :, sl], gk_ref[...])
        if rope:
            q = rot(q, ca_ref[...], sa_ref[...], da_dim // 4)
            k = rot(k, ca_ref[...], sa_ref[...], da_dim // 4)
        oq_ref[0, :, sl] = (q * q_scale).astype(BF16)
        ok_ref[0, :, sl] = k.astype(BF16)
    for j in range(rq_ref.shape[2] // LANES):
        sl = slice(j * LANES, (j + 1) * LANES)
        q = rq_ref[0, :, sl]
        k = rk_ref[0, :, sl]
        if rope:
            q = rot(q, cr_ref[...], sr_ref[...], ret_kdim // 2)
            k = rot(k, cr_ref[...], sr_ref[...], ret_kdim // 2)
        orq_ref[0, :, sl] = q
        ork_ref[0, :, sl] = k * k_scale


def _qk_prep(proj, g_q, g_k, tabs, cols, *, rope, da_dim, ret_kdim):
    b, n, _ = proj.shape
    d = cols["d"]
    rw = cols["ret_qk"]
    tm = min(n, 512)
    assert n % tm == 0
    gq = jnp.tile(g_q, LANES // da_dim).reshape(1, LANES)
    gk = jnp.tile(g_k, LANES // da_dim).reshape(1, LANES)
    tab_spec = pl.BlockSpec((tm, LANES), lambda bi, i: (i, 0))
    vec_spec = pl.BlockSpec((1, LANES), lambda bi, i: (0, 0))

    def col_spec(width, off):
        assert off % width == 0
        return pl.BlockSpec((1, tm, width), lambda bi, i: (bi, i, off // width))

    kern = functools.partial(_prep_kernel, rope=rope, da_dim=da_dim, ret_kdim=ret_kdim)
    return pl.pallas_call(
        kern,
        out_shape=(jax.ShapeDtypeStruct((b, n, d), BF16), jax.ShapeDtypeStruct((b, n, d), BF16),
                   jax.ShapeDtypeStruct((b, n, rw), F32), jax.ShapeDtypeStruct((b, n, rw), F32)),
        grid=(b, n // tm),
        in_specs=[col_spec(d, cols["aq"]), col_spec(d, cols["ak"]),
                  col_spec(rw, cols["rq"]), col_spec(rw, cols["rk"]),
                  vec_spec, vec_spec, tab_spec, tab_spec, tab_spec, tab_spec],
        out_specs=(pl.BlockSpec((1, tm, d), lambda bi, i: (bi, i, 0)),
                   pl.BlockSpec((1, tm, d), lambda bi, i: (bi, i, 0)),
                   pl.BlockSpec((1, tm, rw), lambda bi, i: (bi, i, 0)),
                   pl.BlockSpec((1, tm, rw), lambda bi, i: (bi, i, 0))),
        compiler_params=_params(("arbitrary", "arbitrary"),
                                tm * (2 * d + 2 * rw) * 4 + tm * (2 * d * 2 + 2 * rw * 4)
                                + 4 * tm * LANES * 4, 8 * MIB),
        name="qk_prep",
    )(proj, proj, proj, proj, gq, gk, *tabs)


def _rope_tables(n_tok, da_dim, ret_kdim):
    t = jnp.arange(n_tok)

    def angles(pos, dim):
        inv = ROPE_BASE ** (-jnp.arange(0, dim, 2, dtype=F32) / dim)
        return pos.astype(F32)[:, None] * inv[None, :]

    def cs(ang):
        c, s = jnp.cos(ang), jnp.sin(ang)
        return jnp.concatenate([c, c], axis=-1), jnp.concatenate([-s, s], axis=-1)

    half = da_dim // 2
    cr_, sr_ = cs(angles(t // GRID_W, half))
    cc_, sc_ = cs(angles(t % GRID_W, half))
    ca = jnp.tile(jnp.concatenate([cr_, cc_], axis=-1), (1, LANES // da_dim))
    sa = jnp.tile(jnp.concatenate([sr_, sc_], axis=-1), (1, LANES // da_dim))
    c1, s1 = cs(angles(t, ret_kdim))
    cr = jnp.tile(c1, (1, LANES // ret_kdim))
    sr = jnp.tile(s1, (1, LANES // ret_kdim))
    return ca, sa, cr, sr


ATTN_MAX_KEY_TILE = 768


def _attn_kernel(scal_ref, q_ref, kx_ref, vx_ref, kc_ref, vc_ref, gsub_ref, o_ref,
                 k_all, v_all, s0_ref, s1_ref, s2_ref, p0_ref, p1_ref, p2_ref, a0_ref, a1_ref, a2_ref,
                 m_ref, acc_ref, *, tq, tk, with_x, da_dim):
    dv = vc_ref.shape[2]
    nc = kc_ref.shape[1]
    nx = kx_ref.shape[1] if with_x else 0
    n_steps = (nx + nc) // tk
    n_tiles = q_ref.shape[1] // tq
    rows = 2 * tq

    k_all[nx:nx + nc, :] = kc_ref[0]
    v_all[nx:nx + nc, 0:dv] = vc_ref[0].astype(BF16)
    v_all[nx:nx + nc, dv:2 * dv] = jnp.ones((nc, dv), BF16)
    if with_x:
        chunk = min(512, nx)

        def fill(j, c):
            off = pl.multiple_of(j * chunk, chunk)
            k_all[pl.ds(off, chunk), :] = kx_ref[0, pl.ds(off, chunk), :]
            v_all[pl.ds(off, chunk), 0:dv] = vx_ref[0, pl.ds(off, chunk), :].astype(BF16)
            v_all[pl.ds(off, chunk), dv:2 * dv] = jnp.ones((chunk, dv), BF16)
            return c

        lax.fori_loop(0, nx // chunk, fill, 0)

    s_bufs, p_bufs, a_bufs = (s0_ref, s1_ref, s2_ref), (p0_ref, p1_ref, p2_ref), (a0_ref, a1_ref, a2_ref)
    lo = _lane_iota((tq, LANES)) < da_dim

    def buf(j):
        return 2 if (n_steps % 2 == 1 and n_steps > 1 and j == n_steps - 1) else j % 2

    def q_start(t):
        return t * tq if isinstance(t, int) else pl.multiple_of(t * tq, tq)

    def qk(t, j):
        q = q_ref[0, pl.ds(q_start(t), tq), :]
        zero = jnp.zeros_like(q)
        q2 = jnp.concatenate([jnp.where(lo, q, zero), jnp.where(lo, zero, q)], axis=0)
        s_bufs[buf(j)][...] = lax.dot_general(q2, k_all[j * tk:(j + 1) * tk, :],
                                              (((1,), (1,)), ((), ())), preferred_element_type=F32)

    def softmax(j):
        b = buf(j)
        s = s_bufs[b][...]
        row_max = jnp.max(s, axis=-1, keepdims=True)
        if j == 0:
            m_new = jnp.broadcast_to(row_max, (rows, LANES))
        else:
            m_prev = m_ref[...]
            m_new = jnp.maximum(m_prev, row_max)
            a_bufs[b][...] = jnp.exp(m_prev - m_new)
        p_bufs[b][...] = jnp.exp(s - jnp.concatenate([m_new] * (tk // LANES), axis=1)).astype(BF16)
        m_ref[...] = m_new

    def pv(j):
        b = buf(j)
        r = jnp.dot(p_bufs[b][...], v_all[j * tk:(j + 1) * tk, :], preferred_element_type=F32)
        if j == 0:
            acc_ref[...] = r
        else:
            acc_ref[...] = jnp.concatenate([a_bufs[b][...]] * (2 * dv // LANES), axis=1) * acc_ref[...] + r

    def finalize(t):
        acc = acc_ref[...]
        o0 = acc[0:tq, 0:dv] / acc[0:tq, dv:dv + 1]
        o1 = acc[tq:2 * tq, 0:dv] / acc[tq:2 * tq, dv:dv + 1]
        o = o0 - scal_ref[0] * o1
        y = o * lax.rsqrt(jnp.mean(o * o, axis=-1, keepdims=True) + EPS) * gsub_ref[...]
        o_ref[0, pl.ds(q_start(t), tq), :] = (y * scal_ref[1]).astype(BF16)

    if n_steps == 1:
        def lone(t, c):
            qk(t, 0)
            softmax(0)
            pv(0)
            finalize(t)
            return c

        lax.fori_loop(0, n_tiles, lone, 0)
    else:
        qk(0, 0)

        def tile(t, c):
            qk(t, 1)
            softmax(0)

            @pl.when(t > 0)
            def _():
                pv(n_steps - 1)
                finalize(t - 1)

            for j in range(1, n_steps - 1):
                qk(t, j + 1)
                softmax(j)
                pv(j - 1)
            qk(jnp.minimum(t + 1, n_tiles - 1), 0)
            softmax(n_steps - 1)
            pv(n_steps - 2)
            return c

        lax.fori_loop(0, n_tiles, tile, 0)
        pv(n_steps - 1)
        finalize(n_tiles - 1)


def _attention(scal, q, kx, proj_x, kc, proj_c, g_sub, cols, *, with_x, da_dim):
    b, nq, d = q.shape
    dv = g_sub.shape[0]
    heads = d // dv
    nc = kc.shape[1]
    nx = kx.shape[1]
    tq = min(nq, 256)
    n_keys = (nx if with_x else 0) + nc
    tk = max(t for t in range(LANES, ATTN_MAX_KEY_TILE + 1, LANES) if n_keys % t == 0)
    assert nq % tq == 0 and 2 * da_dim == LANES and dv == LANES
    v_blk = cols["av"] // dv
    kern = functools.partial(_attn_kernel, tq=tq, tk=tk, with_x=with_x, da_dim=da_dim)
    rows = 2 * tq
    pipelined = 2 * nq * dv * 2 + nx * dv * (2 + 4) + nc * dv * (2 + 4)
    resident = (n_keys * 3 * dv * 2 + 3 * rows * tk * (4 + 2) + 4 * rows * LANES * 4
                + rows * 2 * dv * 4 + 3 * rows * tk * 4)
    return pl.pallas_call(
        kern,
        out_shape=jax.ShapeDtypeStruct((b, nq, d), BF16),
        grid=(b, heads),
        in_specs=[pl.BlockSpec(memory_space=pltpu.SMEM),
                  pl.BlockSpec((1, nq, dv), lambda bi, h: (bi, 0, h)),
                  pl.BlockSpec((1, nx, dv), lambda bi, h: (bi, 0, h)),
                  pl.BlockSpec((1, nx, dv), lambda bi, h: (bi, 0, v_blk + h)),
                  pl.BlockSpec((1, nc, dv), lambda bi, h: (bi, 0, h)),
                  pl.BlockSpec((1, nc, dv), lambda bi, h: (bi, 0, v_blk + h)),
                  pl.BlockSpec((1, dv), lambda bi, h: (0, 0))],
        out_specs=pl.BlockSpec((1, nq, dv), lambda bi, h: (bi, 0, h)),
        scratch_shapes=[pltpu.VMEM((n_keys, dv), BF16), pltpu.VMEM((n_keys, 2 * dv), BF16)]
        + [pltpu.VMEM((rows, tk), F32)] * 3 + [pltpu.VMEM((rows, tk), BF16)] * 3
        + [pltpu.VMEM((rows, LANES), F32)] * 4 + [pltpu.VMEM((rows, 2 * dv), F32)],
        compiler_params=_params(("arbitrary", "arbitrary"), pipelined, resident),
        name="diff_attention" if with_x else "diff_attention_ctx",
    )(scal, q, kx, proj_x, kc, proj_c, g_sub.reshape(1, dv))


CONV_HALO = 16
CONV_ROW_CHUNK = 64


def _conv_kernel(a_ref, g_ref, ap_ref, gp_ref, an_ref, gn_ref, w_ref, cb_ref, lg_ref, lb_ref,
                 o_ref, ext_ref, cv_ref, sh_ref):
    i = pl.program_id(1)
    last = pl.num_programs(1) - 1
    tm = a_ref.shape[1]
    ch = a_ref.shape[2]
    width = w_ref.shape[0]
    pad = width // 2

    def glu(a, g):
        return a * _sigmoid(g)

    ext_ref[0:CONV_HALO, :] = jnp.where(i > 0, glu(ap_ref[0], gp_ref[0]), 0.0)
    ext_ref[CONV_HALO:CONV_HALO + tm, :] = glu(a_ref[0], g_ref[0])
    ext_ref[CONV_HALO + tm:CONV_HALO + tm + CONV_HALO, :] = jnp.where(i < last, glu(an_ref[0], gn_ref[0]), 0.0)

    span = sh_ref.shape[1]
    for c in range(ch // LANES):
        sl = slice(c * LANES, (c + 1) * LANES)
        for s in range(SUBLANES):
            sh_ref[s] = ext_ref[s:s + span, sl]
        def rows(i, carry, sl=sl):
            r0 = pl.multiple_of(i * CONV_ROW_CHUNK, CONV_ROW_CHUNK)
            acc = jnp.zeros((CONV_ROW_CHUNK, LANES), F32)
            for j in range(width):
                start = CONV_HALO - pad + j
                base = start - start % SUBLANES
                acc = acc + w_ref[j:j + 1, sl] * sh_ref[start % SUBLANES, pl.ds(base + r0, CONV_ROW_CHUNK), :]
            cv_ref[pl.ds(r0, CONV_ROW_CHUNK), sl] = acc + cb_ref[:, sl]
            return carry

        lax.fori_loop(0, tm // CONV_ROW_CHUNK, rows, 0)

    v = cv_ref[...]
    mu = jnp.mean(v, axis=-1, keepdims=True)
    vc = v - mu
    var = jnp.mean(vc * vc, axis=-1, keepdims=True)
    y = vc * lax.rsqrt(var + EPS) * lg_ref[...] + lb_ref[...]
    o_ref[0] = (y * _sigmoid(y)).astype(BF16)


def _conformer_conv(proj, conv_w, conv_b, ln_g, ln_b, cols):
    b, n, _ = proj.shape
    width, ch = conv_w.shape
    tm = min(n, 256)
    assert n % tm == 0 and tm % CONV_HALO == 0 and width // 2 < CONV_HALO
    a_blk = cols["conv"] // ch
    g_blk = a_blk + 1
    r = tm // CONV_HALO
    n_halo = n // CONV_HALO

    def cur(blk):
        return pl.BlockSpec((1, tm, ch), lambda bi, i: (bi, i, blk))

    def prev(blk):
        return pl.BlockSpec((1, CONV_HALO, ch), lambda bi, i: (bi, jnp.maximum(i * r - 1, 0), blk))

    def nxt(blk):
        return pl.BlockSpec((1, CONV_HALO, ch), lambda bi, i: (bi, jnp.minimum((i + 1) * r, n_halo - 1), blk))

    vec = pl.BlockSpec((1, ch), lambda bi, i: (0, 0))
    return pl.pallas_call(
        _conv_kernel,
        out_shape=jax.ShapeDtypeStruct((b, n, ch), BF16),
        grid=(b, n // tm),
        in_specs=[cur(a_blk), cur(g_blk), prev(a_blk), prev(g_blk), nxt(a_blk), nxt(g_blk),
                  pl.BlockSpec((width, ch), lambda bi, i: (0, 0)), vec, vec, vec],
        out_specs=pl.BlockSpec((1, tm, ch), lambda bi, i: (bi, i, 0)),
        scratch_shapes=[pltpu.VMEM((tm + 2 * CONV_HALO, ch), F32), pltpu.VMEM((tm, ch), F32),
                        pltpu.VMEM((SUBLANES, tm + 2 * CONV_HALO - SUBLANES, LANES), F32)],
        compiler_params=_params(("arbitrary", "arbitrary"),
                                (2 * tm + 4 * CONV_HALO) * ch * 4 + tm * ch * 2 + 40 * ch * 4,
                                (2 * tm + 2 * CONV_HALO) * ch * 4 + 4 * tm * ch * 4
                                + SUBLANES * (tm + 2 * CONV_HALO) * LANES * 4),
        name="conformer_conv",
    )(proj, proj, proj, proj, proj, proj, conv_w, conv_b.reshape(1, ch), ln_g.reshape(1, ch),
      ln_b.reshape(1, ch))


def _ret_kernel(qf_ref, kf_ref, vf_ref, gf_ref, qb_ref, kb_ref, vb_ref, gb_ref, s0_ref,
                inner_ref, xi_ref, zeta_ref, gc_ref, gng_ref, gnb_ref,
                of_ref, ob_ref, sfin_ref, s_ref, *, kdim):
    i = pl.program_id(1)
    c = qf_ref.shape[1]
    dv = gng_ref.shape[1]
    n_pair = qf_ref.shape[2] // LANES

    @pl.when(i == 0)
    def _():
        s_ref[...] = s0_ref[0]

    lo = _lane_iota((c, LANES)) < kdim
    dirs = ((qf_ref, kf_ref, vf_ref, gf_ref, of_ref), (qb_ref, kb_ref, vb_ref, gb_ref, ob_ref))
    for d, (q_ref, k_ref, v_ref, g_ref, o_ref) in enumerate(dirs):
        for p in range(n_pair):
            sl = slice(p * LANES, (p + 1) * LANES)
            q = q_ref[0, :, sl]
            k = k_ref[0, :, sl]
            kz = k * zeta_ref[d, p]
            kb16 = k.astype(BF16)
            s_pair = s_ref[d, p]
            s16 = s_pair.astype(BF16)
            upd = jnp.zeros((LANES, dv), F32)
            for hh in range(2):
                h = 2 * p + hh
                hsl = slice(h * dv, (h + 1) * dv)
                keep = lo if hh == 0 else jnp.logical_not(lo)
                qh = jnp.where(keep, q, 0.0).astype(BF16)
                v16 = v_ref[0, :, hsl].astype(BF16)
                att = lax.dot_general(qh, kb16, (((1,), (1,)), ((), ())), preferred_element_type=F32)
                att = (att * inner_ref[d, h]).astype(BF16)
                o = (jnp.dot(att, v16, preferred_element_type=F32)
                     + jnp.dot(qh, s16, preferred_element_type=F32) * xi_ref[d, h])
                kzh = jnp.where(keep, kz, 0.0).astype(BF16)
                upd = upd + lax.dot_general(kzh, v16, (((0,), (0,)), ((), ())), preferred_element_type=F32)
                mu = jnp.mean(o, axis=-1, keepdims=True)
                oc = o - mu
                var = jnp.mean(oc * oc, axis=-1, keepdims=True)
                y = oc * lax.rsqrt(var + EPS) * gng_ref[...] + gnb_ref[...]
                g = g_ref[0, :, hsl]
                o_ref[0, :, hsl] = (g * _sigmoid(g)) * y
            s_ref[d, p] = s_pair * gc_ref[p] + upd

    sfin_ref[0] = s_ref[...]


def _retention(rq, rk, proj, s0, tabs, gn_g, gn_b, cols):
    b, n, rw = rq.shape
    dv = gn_g.shape[0]
    d = cols["d"]
    c = RET_CHUNK
    nch = n // c
    kdim = dv // 2
    assert n % c == 0 and c == LANES and 2 * kdim == LANES
    n_pair = rw // LANES
    inner, xi, zeta, gc = tabs
    v_blk, gf_blk, gb_blk = cols["rv"] // d, cols["rgf"] // d, cols["rgb"] // d

    def fwd(width, blk):
        return pl.BlockSpec((1, c, width), lambda bi, i: (bi, i, blk))

    def bwd(width, blk):
        return pl.BlockSpec((1, c, width), lambda bi, i: (bi, nch - 1 - i, blk))

    def whole(a):
        return pl.BlockSpec(a.shape, lambda bi, i: (0,) * a.ndim)

    state = pl.BlockSpec((1, 2, n_pair, LANES, dv), lambda bi, i: (bi, 0, 0, 0, 0))
    vec = pl.BlockSpec((1, dv), lambda bi, i: (0, 0))
    kern = functools.partial(_ret_kernel, kdim=kdim)
    pipelined = 2 * c * (2 * rw + 2 * d) * 4 + 2 * c * d * 4 + 2 * 2 * n_pair * LANES * dv * 4
    resident = (inner.size + xi.size + zeta.size + gc.size) * 4 * 2 + 2 * n_pair * LANES * dv * 4 + 8 * MIB
    return pl.pallas_call(
        kern,
        out_shape=(jax.ShapeDtypeStruct((b, n, d), F32), jax.ShapeDtypeStruct((b, n, d), F32),
                   jax.ShapeDtypeStruct((b, 2, n_pair, LANES, dv), F32)),
        grid=(b, nch),
        in_specs=[fwd(rw, 0), fwd(rw, 0), fwd(d, v_blk), fwd(d, gf_blk),
                  bwd(rw, 0), bwd(rw, 0), bwd(d, v_blk), bwd(d, gb_blk),
                  state, whole(inner), whole(xi), whole(zeta), whole(gc), vec, vec],
        out_specs=(pl.BlockSpec((1, c, d), lambda bi, i: (bi, i, 0)),
                   pl.BlockSpec((1, c, d), lambda bi, i: (bi, nch - 1 - i, 0)),
                   state),
        scratch_shapes=[pltpu.VMEM((2, n_pair, LANES, dv), F32)],
        compiler_params=_params(("arbitrary", "arbitrary"), pipelined, resident),
        name="retention",
    )(rq, rk, proj, proj, rq, rk, proj, proj, s0, inner, xi, zeta, gc,
      gn_g.reshape(1, dv), gn_b.reshape(1, dv))


def _retention_tables(heads, kdim, dv):
    c = RET_CHUNK
    lg = jnp.log1p(-jnp.exp2(-5.0 - jnp.arange(heads, dtype=F32)))
    pos = jnp.arange(c, dtype=F32)
    diff = pos[:, None] - pos[None, :]
    inner_f = jnp.where(diff[None] >= 0, jnp.exp(jnp.maximum(diff, 0.0)[None] * lg[:, None, None]), 0.0)
    inner = jnp.stack([inner_f, jnp.swapaxes(inner_f, 1, 2)])
    xi_f = jnp.exp((pos + 1.0)[None, :] * lg[:, None])
    zeta_f = jnp.exp((c - 1.0 - pos)[None, :] * lg[:, None])
    xi = jnp.stack([xi_f, xi_f[:, ::-1]])
    zeta = jnp.stack([zeta_f, zeta_f[:, ::-1]])
    xi = jnp.broadcast_to(xi[..., None], (2, heads, c, dv))
    zeta = jnp.broadcast_to(zeta[..., None], (2, heads, c, kdim))
    zeta = zeta.reshape(2, heads // 2, 2, c, kdim).transpose(0, 1, 3, 2, 4).reshape(2, heads // 2, c, 2 * kdim)
    gc = jnp.exp(c * lg)
    gc = jnp.broadcast_to(gc[:, None, None], (heads, kdim, dv)).reshape(heads // 2, 2 * kdim, dv)
    return inner.astype(F32), xi.astype(F32), zeta.astype(F32), gc.astype(F32)


def _merge_kernel(h_ref, mod_ref, oa_ref, ob_ref, of_ref, obk_ref, gate_ref, bg_ref,
                  wa_ref, wb_ref, wc_ref, wo_ref, o_ref):
    d = h_ref.shape[2]
    g = _sigmoid(gate_ref[0] + bg_ref[...])
    oc = (of_ref[0] + obk_ref[0]).astype(BF16)
    y = g[:, 0:d] * jnp.dot(oa_ref[0], wa_ref[...], preferred_element_type=F32)
    y = y + g[:, d:2 * d] * jnp.dot(ob_ref[0], wb_ref[...], preferred_element_type=F32)
    y = y + g[:, 2 * d:3 * d] * jnp.dot(oc, wc_ref[...], preferred_element_type=F32)
    m = jnp.dot(y.astype(BF16), wo_ref[...], preferred_element_type=F32)
    o_ref[0] = h_ref[0] + mod_ref[0, 2:3, :] * m


def _merge(h, mod, oa, ob, of, obk, proj, b_gate, w_pa, w_pb, w_pc, w_out, layer, cols):
    b, n, d = h.shape
    tm = min(n, 512)
    assert n % tm == 0
    mb = mod.shape[0]
    gate_blk = cols["gate"] // (3 * d)
    row = lambda width: pl.BlockSpec((1, tm, width), lambda bi, i: (bi, i, 0))
    wspec = pl.BlockSpec((None, d, d), lambda bi, i: (layer, 0, 0))
    return pl.pallas_call(
        _merge_kernel,
        out_shape=jax.ShapeDtypeStruct((b, n, d), F32),
        grid=(b, n // tm),
        in_specs=[row(d), pl.BlockSpec((1, N_MOD, d), lambda bi, i: (bi % mb, 0, 0)),
                  row(d), row(d), row(d), row(d),
                  pl.BlockSpec((1, tm, 3 * d), lambda bi, i: (bi, i, gate_blk)),
                  pl.BlockSpec((1, 3 * d), lambda bi, i: (0, 0)),
                  wspec, wspec, wspec, wspec],
        out_specs=row(d),
        compiler_params=_params(("arbitrary", "arbitrary"),
                                tm * d * (4 + 2 + 2 + 4 + 4 + 12 + 4) + 4 * d * d * 2, 10 * tm * d * 4),
        name="merge",
    )(h, mod, oa, ob, of, obk, proj, b_gate.reshape(1, 3 * d), w_pa, w_pb, w_pc, w_out)


ROUTE_E, ROUTE_G, ROUTE_R = 0, 4, 8


def _route_kernel(h_ref, mod_ref, g_ref, wr_ref, br_ref, tri_ref, base_ref,
                  hx_ref, route_ref, cnt_ref, run_ref, *, n_exp):
    first = jnp.logical_and(pl.program_id(0) == 0, pl.program_id(1) == 0)

    @pl.when(first)
    def _():
        run_ref[...] = base_ref[...]

    h = h_ref[0]
    tm = h.shape[0]
    y = h * lax.rsqrt(jnp.mean(h * h, axis=-1, keepdims=True) + EPS) * g_ref[...]
    hx = y * (1.0 + mod_ref[0, 4:5, :]) + mod_ref[0, 3:4, :]
    hx_ref[0] = hx

    lane = _lane_iota((tm, LANES))
    lane_f = lane.astype(F32)
    logits = jnp.dot(hx, wr_ref[...], preferred_element_type=F32,
                     precision=lax.Precision.HIGHEST) + br_ref[...]
    logits = jnp.where(lane < n_exp, logits, NEG_INF)

    sels, vals = [], []
    for _ in range(TOP_K):
        mx = jnp.max(logits, axis=-1, keepdims=True)
        idx = jnp.min(jnp.where(logits == mx, lane_f, float(LANES)), axis=-1, keepdims=True)
        sel = lane_f == idx
        sels.append((sel, idx))
        vals.append(mx)
        logits = jnp.where(sel, NEG_INF, logits)

    es = [jnp.exp(v - vals[0]) for v in vals]
    denom = es[0]
    for e in es[1:]:
        denom = denom + e

    onehot = jnp.zeros((tm, LANES), F32)
    for sel, _ in sels:
        onehot = onehot + jnp.where(sel, 1.0, 0.0)
    before = run_ref[...] + jnp.dot(tri_ref[...], onehot.astype(BF16), preferred_element_type=F32)

    route = jnp.zeros((tm, LANES), F32)
    for k, (sel, idx) in enumerate(sels):
        rank = jnp.sum(jnp.where(sel, before, 0.0), axis=-1, keepdims=True)
        route = jnp.where(lane == ROUTE_E + k, idx, route)
        route = jnp.where(lane == ROUTE_G + k, es[k] / denom, route)
        route = jnp.where(lane == ROUTE_R + k, rank, route)
    route_ref[0] = route

    run_ref[...] = run_ref[...] + jnp.sum(onehot, axis=0, keepdims=True)
    cnt_ref[...] = run_ref[...]


def _route(h, mod, g, w_router, b_router, base):
    b, n, d = h.shape
    n_exp = w_router.shape[-1]
    tm = min(n, 256)
    assert n % tm == 0 and n_exp <= LANES
    mb = mod.shape[0]
    wr = jnp.zeros((d, LANES), F32).at[:, :n_exp].set(w_router)
    br = jnp.zeros((1, LANES), F32).at[0, :n_exp].set(b_router)
    tri = (jnp.arange(tm)[:, None] > jnp.arange(tm)[None, :]).astype(BF16)
    row = lambda width: pl.BlockSpec((1, tm, width), lambda bi, i: (bi, i, 0))
    const = lambda shape: pl.BlockSpec(shape, lambda bi, i: (0,) * len(shape))
    kern = functools.partial(_route_kernel, n_exp=n_exp)
    return pl.pallas_call(
        kern,
        out_shape=(jax.ShapeDtypeStruct((b, n, d), F32), jax.ShapeDtypeStruct((b, n, LANES), F32),
                   jax.ShapeDtypeStruct((1, LANES), F32)),
        grid=(b, n // tm),
        in_specs=[row(d), pl.BlockSpec((1, N_MOD, d), lambda bi, i: (bi % mb, 0, 0)), const((1, d)),
                  const((d, LANES)), const((1, LANES)), const((tm, tm)), const((1, LANES))],
        out_specs=(row(d), row(LANES), const((1, LANES))),
        scratch_shapes=[pltpu.VMEM((1, LANES), F32)],
        compiler_params=_params(("arbitrary", "arbitrary"),
                                2 * tm * d * 4 + tm * LANES * 4 + d * LANES * 4 + tm * tm * 2, 8 * tm * d * 4),
        name="moe_route",
    )(h, mod, g.reshape(1, d), wr, br, tri, base)


DMA_ISSUE_UNROLL = 8


def _tile_dest(dest, tm):
    t = dest.shape[0]
    return dest.reshape(t // tm, tm, TOP_K).transpose(0, 2, 1).reshape(t // tm, 1, TOP_K * tm)


def _dispatch_kernel(dest_ref, hx_ref, xs_in_ref, xs_ref, sem):
    del xs_in_ref
    tm = hx_ref.shape[0]

    def copy(k, t):
        return pltpu.make_async_copy(hx_ref.at[pl.ds(t, 1)],
                                     xs_ref.at[pl.ds(dest_ref[0, 0, k * tm + t], 1)], sem)

    for k in range(TOP_K):
        def start(t2, c, k=k):
            copy(k, 2 * t2).start(priority=0)
            copy(k, 2 * t2 + 1).start(priority=1)
            return c

        lax.fori_loop(0, tm // 2, start, 0, unroll=DMA_ISSUE_UNROLL // 2)
    for k in range(TOP_K):
        def wait(t, c, k=k):
            copy(k, t).wait()
            return c

        lax.fori_loop(0, tm, wait, 0, unroll=DMA_ISSUE_UNROLL)


def _dispatch(hx_flat, dest, xs):
    t, d = hx_flat.shape
    tm = min(t, 256)
    assert t % tm == 0
    dest3 = _tile_dest(dest, tm)
    return pl.pallas_call(
        _dispatch_kernel,
        out_shape=jax.ShapeDtypeStruct(xs.shape, xs.dtype),
        grid=(t // tm,),
        in_specs=[pl.BlockSpec((1, 1, tm * TOP_K), lambda i: (i, 0, 0), memory_space=pltpu.SMEM),
                  pl.BlockSpec((tm, d), lambda i: (i, 0)),
                  pl.BlockSpec(memory_space=pl.ANY)],
        out_specs=pl.BlockSpec(memory_space=pl.ANY),
        scratch_shapes=[pltpu.SemaphoreType.DMA],
        input_output_aliases={2: 0},
        compiler_params=pltpu.CompilerParams(dimension_semantics=("arbitrary",),
                                             vmem_limit_bytes=_vmem_limit(tm * d * 4, 4 * MIB),
                                             has_side_effects=True),
        name="moe_dispatch",
    )(dest3, hx_flat, xs)


def _split_kernel(w_ref, sel_ref, g_ref, u_ref):
    de = g_ref.shape[-1]
    r = jnp.dot(w_ref[...].astype(BF16), sel_ref[...], preferred_element_type=F32)
    g_ref[...] = r[:, :de].astype(BF16)
    u_ref[...] = r[:, de:].astype(BF16)


def _split_gate_up(w_gate_up):
    depth, n_exp, d, de2 = w_gate_up.shape
    de = de2 // 2
    tr = min(d, 512)
    assert d % tr == 0
    order = jnp.concatenate([jnp.arange(0, de2, 2), jnp.arange(1, de2, 2)])
    sel = (jnp.arange(de2)[:, None] == order[None, :]).astype(BF16)
    out = jax.ShapeDtypeStruct((depth, n_exp, d, de), BF16)
    ospec = pl.BlockSpec((None, None, tr, de), lambda l, e, r: (l, e, r, 0))
    return pl.pallas_call(
        _split_kernel,
        out_shape=(out, out),
        grid=(depth, n_exp, d // tr),
        in_specs=[pl.BlockSpec((None, None, tr, de2), lambda l, e, r: (l, e, r, 0)),
                  pl.BlockSpec((de2, de2), lambda l, e, r: (0, 0))],
        out_specs=(ospec, ospec),
        compiler_params=_params(("arbitrary", "arbitrary", "arbitrary"),
                                tr * de2 * 4 + de2 * de2 * 2 + 2 * tr * de * 2, tr * de2 * (2 + 4 + 4)),
        name="split_gate_up",
    )(w_gate_up, sel)


def _expert_kernel(blk_e_ref, n_used_ref, x_ref, wg_ref, wu_ref, wd_ref, bg_ref, bu_ref, bd_ref, y_ref):
    del blk_e_ref
    used = pl.program_id(0) < n_used_ref[0]

    @pl.when(used)
    def _():
        x = x_ref[...].astype(BF16)
        gl = jnp.dot(x, wg_ref[...], preferred_element_type=F32) + bg_ref[...]
        up = jnp.dot(x, wu_ref[...], preferred_element_type=F32) + bu_ref[...]
        gl = jnp.minimum(gl, SWIGLU_LIMIT)
        up = jnp.clip(up, -SWIGLU_LIMIT, SWIGLU_LIMIT)
        act = (up + 1.0) * (gl * _sigmoid(SWIGLU_ALPHA * gl))
        y_ref[...] = jnp.dot(act.astype(BF16), wd_ref[...], preferred_element_type=F32) + bd_ref[...]

    @pl.when(jnp.logical_not(used))
    def _():
        y_ref[...] = jnp.zeros(y_ref.shape, F32)


def _experts(blk_e, n_used, xs, wg, wu, wd, bg, bu, bd, layer, bm):
    n_rows, d = xs.shape
    de = wg.shape[-1]
    nb = n_rows // bm
    wspec = lambda a, c: pl.BlockSpec((None, None, a, c), lambda i, be, nu: (layer, be[i], 0, 0))
    return pl.pallas_call(
        _expert_kernel,
        out_shape=jax.ShapeDtypeStruct((n_rows, d), F32),
        grid_spec=pltpu.PrefetchScalarGridSpec(
            num_scalar_prefetch=2,
            grid=(nb,),
            in_specs=[pl.BlockSpec((bm, d), lambda i, be, nu: (i, 0)),
                      wspec(d, de), wspec(d, de), wspec(de, d),
                      wspec(1, de), wspec(1, de), wspec(1, d)],
            out_specs=pl.BlockSpec((bm, d), lambda i, be, nu: (i, 0))),
        compiler_params=_params(("arbitrary",), 2 * bm * d * 4 + 3 * d * de * 2, 8 * bm * de * 4),
        name="moe_experts",
    )(blk_e, n_used, xs, wg, wu, wd, bg, bu, bd)


def _combine_kernel(dest_ref, h_ref, mod_ref, route_ref, y_ref, o_ref, ybuf, sem):
    tm = h_ref.shape[1]

    def copy(r):
        return pltpu.make_async_copy(y_ref.at[pl.ds(dest_ref[0, 0, r], 1)], ybuf.at[pl.ds(r, 1)], sem)

    def start(r2, c):
        copy(2 * r2).start(priority=0)
        copy(2 * r2 + 1).start(priority=1)
        return c

    def wait(r, c):
        copy(r).wait()
        return c

    lax.fori_loop(0, TOP_K * tm // 2, start, 0, unroll=DMA_ISSUE_UNROLL // 2)
    lax.fori_loop(0, TOP_K * tm, wait, 0, unroll=DMA_ISSUE_UNROLL)

    route = route_ref[0]
    f = jnp.zeros((tm, h_ref.shape[2]), F32)
    for k in range(TOP_K):
        f = f + route[:, ROUTE_G + k:ROUTE_G + k + 1] * ybuf[k * tm:(k + 1) * tm, :]
    o_ref[0] = h_ref[0] + mod_ref[0, 5:6, :] * f


def _combine(h, mod, route, dest, y):
    b, n, d = h.shape
    tm = min(n, 256)
    assert n % tm == 0
    mb = mod.shape[0]
    nt = n // tm
    dest3 = _tile_dest(dest, tm)
    row = lambda width: pl.BlockSpec((1, tm, width), lambda bi, i: (bi, i, 0))
    return pl.pallas_call(
        _combine_kernel,
        out_shape=jax.ShapeDtypeStruct((b, n, d), F32),
        grid=(b, nt),
        in_specs=[pl.BlockSpec((1, 1, tm * TOP_K), lambda bi, i: (bi * nt + i, 0, 0), memory_space=pltpu.SMEM),
                  row(d), pl.BlockSpec((1, N_MOD, d), lambda bi, i: (bi % mb, 0, 0)), row(LANES),
                  pl.BlockSpec(memory_space=pl.ANY)],
        out_specs=row(d),
        scratch_shapes=[pltpu.VMEM((TOP_K * tm, d), F32), pltpu.SemaphoreType.DMA],
        compiler_params=_params(("arbitrary", "arbitrary"), 2 * tm * d * 4 + tm * LANES * 4,
                                TOP_K * tm * d * 4 + 4 * tm * d * 4),
        name="moe_combine",
    )(dest3, h, mod, route, y)


def _moe(parts, g, w_router, b_router, wg, wu, wd, bg, bu, bd, layer, bm):
    n_exp = w_router.shape[-1]
    d = parts[0][0].shape[-1]
    base = jnp.zeros((1, LANES), F32)
    hxs, routes = [], []
    for h, mod in parts:
        hx, route, base = _route(h, mod, g, w_router, b_router, base)
        hxs.append(hx)
        routes.append(route)
    counts = base[0, :n_exp].astype(jnp.int32)

    padded = (counts + bm - 1) // bm * bm
    pend = jnp.cumsum(padded)
    pstart = pend - padded
    n_tok = sum(h.shape[0] * h.shape[1] for h, _ in parts)
    n_rows = -(-(n_tok * TOP_K + n_exp * (bm - 1)) // bm) * bm
    nb = n_rows // bm
    blk_start = jnp.arange(nb, dtype=jnp.int32) * bm
    blk_e = jnp.minimum(jnp.sum((pend[None, :] <= blk_start[:, None]).astype(jnp.int32), axis=1), n_exp - 1)
    n_used = (pend[-1] // bm).astype(jnp.int32).reshape(1)
    eids = jnp.arange(n_exp, dtype=jnp.int32)

    dests = []
    xs = jnp.zeros((n_rows, d), F32)
    for hx, route in zip(hxs, routes):
        e = route[..., ROUTE_E:ROUTE_E + TOP_K].astype(jnp.int32)
        rank = route[..., ROUTE_R:ROUTE_R + TOP_K].astype(jnp.int32)
        start = jnp.sum(jnp.where(e[..., None] == eids, pstart, 0), axis=-1)
        dest = (start + rank).reshape(-1, TOP_K)
        dests.append(dest)
        xs = _dispatch(hx.reshape(-1, d), dest, xs)

    y = _experts(blk_e, n_used, xs, wg, wu, wd, bg, bu, bd, layer, bm)
    return [_combine(h, mod, route, dest, y)
            for (h, mod), route, dest in zip(parts, routes, dests)]


def kernel(x, c, ctx, c_ctx, w_mod, b_mod, g_norm1, g_norm2, w_in, b_gate, g_qa, g_ka, lam_q1, lam_k1, lam_q2, lam_k2, g_sub, conv_w, conv_b, ln_g, ln_b, gn_g, gn_b, w_pa, w_pb, w_pc, w_out, w_router, b_router, w_gate_up, b_gate_up, w_down, b_down):
    bsz, n_x, d = x.shape
    n_c = ctx.shape[1]
    depth = w_mod.shape[0]
    da_dim = g_qa.shape[-1]
    dv = g_sub.shape[-1]
    heads = d // dv
    ret_kdim = gn_g.shape[-1] // 2
    n_exp = w_router.shape[-1]
    de = w_down.shape[-2]
    moe_bm = 512

    cols = {"d": d, "ret_qk": heads * ret_kdim}
    off = 0
    for name, width in (("aq", d), ("ak", d), ("av", d), ("conv", 2 * d), ("rq", heads * ret_kdim),
                        ("rk", heads * ret_kdim), ("rv", d), ("rgf", d), ("rgb", d), ("gate", 3 * d)):
        cols[name] = off
        off += width
    assert off == w_in.shape[-1]

    w_in_bf = w_in.astype(BF16)
    w_pa_bf, w_pb_bf, w_pc_bf, w_out_bf = (w.astype(BF16) for w in (w_pa, w_pb, w_pc, w_out))
    wg, wu = _split_gate_up(w_gate_up)
    wd = w_down.astype(BF16)
    bg = b_gate_up[..., 0::2].reshape(depth, n_exp, 1, de)
    bu = b_gate_up[..., 1::2].reshape(depth, n_exp, 1, de)
    bd = b_down.reshape(depth, n_exp, 1, d)

    rows = -(-(bsz + 1) // SUBLANES) * SUBLANES
    cvecs = jnp.zeros((rows, d), F32).at[:bsz].set(c).at[bsz].set(c_ctx)
    mods = _modulation(cvecs, w_mod, b_mod).reshape(depth, rows, N_MOD, d)

    tabs_x = _rope_tables(n_x, da_dim, ret_kdim)
    tabs_c = tuple(jnp.zeros((n_c, LANES), F32) for _ in range(4))
    ret_tabs = _retention_tables(heads, ret_kdim, dv)
    s_zero = jnp.zeros((bsz, 2, heads // 2, LANES, dv), F32)

    h_ctx = ctx
    for l in range(depth):
        need_ctx = l < depth - 1
        lam_init = 0.8 - 0.6 * math.exp(-0.3 * l)
        lam = (jnp.exp(jnp.sum(lam_q1[l] * lam_k1[l])) - jnp.exp(jnp.sum(lam_q2[l] * lam_k2[l])) + lam_init)
        scal = jnp.stack([lam, jnp.asarray(1.0 - lam_init, F32)]).astype(F32)
        mod_x = mods[l, :bsz]
        mod_c = mods[l, bsz:bsz + 1]

        proj_x = _in_proj(x, mod_x, g_norm1[l], w_in_bf, l)
        proj_c = _in_proj(h_ctx, mod_c, g_norm1[l], w_in_bf, l)
        aq_x, ak_x, rq_x, rk_x = _qk_prep(proj_x, g_qa[l], g_ka[l], tabs_x, cols, rope=True,
                                          da_dim=da_dim, ret_kdim=ret_kdim)
        aq_c, ak_c, rq_c, rk_c = _qk_prep(proj_c, g_qa[l], g_ka[l], tabs_c, cols, rope=False,
                                          da_dim=da_dim, ret_kdim=ret_kdim)
        oa_x = _attention(scal, aq_x, ak_x, proj_x, ak_c, proj_c, g_sub[l], cols, with_x=True, da_dim=da_dim)
        ob_x = _conformer_conv(proj_x, conv_w[l], conv_b[l], ln_g[l], ln_b[l], cols)
        of_c, ob_c, s_ctx = _retention(rq_c, rk_c, proj_c, s_zero, ret_tabs, gn_g[l], gn_b[l], cols)
        of_x, obk_x, _ = _retention(rq_x, rk_x, proj_x, s_ctx, ret_tabs, gn_g[l], gn_b[l], cols)
        x = _merge(x, mod_x, oa_x, ob_x, of_x, obk_x, proj_x, b_gate[l],
                   w_pa_bf, w_pb_bf, w_pc_bf, w_out_bf, l, cols)
        if need_ctx:
            oa_c = _attention(scal, aq_c, ak_c, proj_c, ak_c, proj_c, g_sub[l], cols, with_x=False, da_dim=da_dim)
            cb_c = _conformer_conv(proj_c, conv_w[l], conv_b[l], ln_g[l], ln_b[l], cols)
            h_ctx = _merge(h_ctx, mod_c, oa_c, cb_c, of_c, ob_c, proj_c, b_gate[l],
                           w_pa_bf, w_pb_bf, w_pc_bf, w_out_bf, l, cols)

        parts = [(h_ctx, mod_c), (x, mod_x)] if need_ctx else [(x, mod_x)]
        outs = _moe(parts, g_norm2[l], w_router[l], b_router[l], wg, wu, wd, bg, bu, bd, l, moe_bm)
        if need_ctx:
            h_ctx, x = outs
        else:
            (x,) = outs
    return x
```

```python
import functools
import math

import jax
import jax.numpy as jnp
from jax import lax
from jax.experimental import pallas as pl
from jax.experimental.pallas import tpu as pltpu

F32 = jnp.float32
BF16 = jnp.bfloat16

GRID_W = 64
N_MOD = 6
EPS = 1e-6
ROPE_BASE = 10000.0
RET_CHUNK = 128
TOP_K = 4
SWIGLU_ALPHA = 1.702
SWIGLU_LIMIT = 7.0

LANES = 128
SUBLANES = 8
V7X_VMEM_BYTES = 64 * 2**20
VMEM_LIMIT_CAP = 56 * 2**20
MIB = 2**20

NEG_INF = float("-inf")


def _sigmoid(x):
    return 1.0 / (1.0 + jnp.exp(-x))


def _vmem_limit(pipelined_bytes, resident_bytes):
    need = 2 * pipelined_bytes + resident_bytes
    return int(min(max(need, 16 * MIB), VMEM_LIMIT_CAP))


def _params(sem, pipelined_bytes, resident_bytes):
    return pltpu.CompilerParams(dimension_semantics=sem,
                                vmem_limit_bytes=_vmem_limit(pipelined_bytes, resident_bytes))


def _lane_iota(shape):
    return lax.broadcasted_iota(jnp.int32, shape, len(shape) - 1)


def _mod_kernel(c_ref, w_ref, b_ref, o_ref):
    c = c_ref[...]
    a = (c * _sigmoid(c)).astype(BF16)
    o_ref[...] = jnp.dot(a, w_ref[...].astype(BF16), preferred_element_type=F32) + b_ref[...]


def _modulation(cvecs, w_mod, b_mod):
    depth, d, cols = w_mod.shape
    rows = cvecs.shape[0]
    tn = 1536
    assert cols % tn == 0
    return pl.pallas_call(
        _mod_kernel,
        out_shape=jax.ShapeDtypeStruct((depth, rows, cols), F32),
        grid=(depth, cols // tn),
        in_specs=[pl.BlockSpec((rows, d), lambda l, j: (0, 0)),
                  pl.BlockSpec((None, d, tn), lambda l, j: (l, 0, j)),
                  pl.BlockSpec((None, 1, tn), lambda l, j: (l, 0, j))],
        out_specs=pl.BlockSpec((None, rows, tn), lambda l, j: (l, 0, j)),
        compiler_params=_params(("arbitrary", "arbitrary"), d * tn * 4 + rows * tn * 4, 4 * MIB),
        name="modulation",
    )(cvecs, w_mod, b_mod.reshape(depth, 1, cols))


def _inproj_kernel(h_ref, mod_ref, g_ref, w_ref, o_ref, xs_ref):
    @pl.when(pl.program_id(2) == 0)
    def _():
        h = h_ref[0]
        y = h * lax.rsqrt(jnp.mean(h * h, axis=-1, keepdims=True) + EPS) * g_ref[...]
        xs_ref[...] = (y * (1.0 + mod_ref[0, 1:2, :]) + mod_ref[0, 0:1, :]).astype(BF16)

    o_ref[0] = jnp.dot(xs_ref[...], w_ref[...], preferred_element_type=F32)


def _in_proj(h, mod, g, w_in_bf, layer):
    b, n, d = h.shape
    cols = w_in_bf.shape[-1]
    tm = min(n, 1024)
    tn = 2048
    assert n % tm == 0 and cols % tn == 0
    mb = mod.shape[0]
    return pl.pallas_call(
        _inproj_kernel,
        out_shape=jax.ShapeDtypeStruct((b, n, cols), F32),
        grid=(b, n // tm, cols // tn),
        in_specs=[pl.BlockSpec((1, tm, d), lambda bi, i, j: (bi, i, 0)),
                  pl.BlockSpec((1, N_MOD, d), lambda bi, i, j: (bi % mb, 0, 0)),
                  pl.BlockSpec((1, d), lambda bi, i, j: (0, 0)),
                  pl.BlockSpec((None, d, tn), lambda bi, i, j: (layer, 0, j))],
        out_specs=pl.BlockSpec((1, tm, tn), lambda bi, i, j: (bi, i, j)),
        scratch_shapes=[pltpu.VMEM((tm, d), BF16)],
        compiler_params=_params(("arbitrary", "arbitrary", "arbitrary"),
                                tm * d * 4 + d * tn * 2 + tm * tn * 4, tm * d * 2 + 3 * tm * d * 4),
        name="in_proj",
    )(h, mod, g.reshape(1, d), w_in_bf)


def _prep_kernel(aq_ref, ak_ref, rq_ref, rk_ref, gq_ref, gk_ref, ca_ref, sa_ref, cr_ref, sr_ref,
                 oq_ref, ok_ref, orq_ref, ork_ref, *, rope, da_dim, ret_kdim):
    tm = aq_ref.shape[1]
    lane = _lane_iota((tm, LANES))
    lo = lane < da_dim

    def rms(x, g):
        x2 = x * x
        s_lo = jnp.sum(jnp.where(lo, x2, 0.0), axis=-1, keepdims=True)
        s_hi = jnp.sum(jnp.where(lo, 0.0, x2), axis=-1, keepdims=True)
        ms = jnp.where(lo, s_lo, s_hi) * (1.0 / da_dim)
        return x * lax.rsqrt(ms + EPS) * g

    def rot(x, c, s, half):
        first = (lane % (2 * half)) < half
        partner = jnp.where(first, pltpu.roll(x, LANES - half, 1), pltpu.roll(x, half, 1))
        return x * c + partner * s

    q_scale = da_dim ** -0.5
    k_scale = ret_kdim ** -0.5
    for j in range(aq_ref.shape[2] // LANES):
        sl = slice(j * LANES, (j + 1) * LANES)
        q = rms(aq_ref[0, :, sl], gq_ref[...])
        k = rms(ak_ref[0, :, sl], gk_ref[...])
        if rope:
            q = rot(q, ca_ref[...], sa_ref[...], da_dim // 4)
            k = rot(k, ca_ref[...], sa_ref[...], da_dim // 4)
        oq_ref[0, :, sl] = (q * q_scale).astype(BF16)
        ok_ref[0, :, sl] = k.astype(BF16)
    for j in range(rq_ref.shape[2] // LANES):
        sl = slice(j * LANES, (j + 1) * LANES)
        q = rq_ref[0, :, sl]
        k = rk_ref[0, :, sl]
        if rope:
            q = rot(q, cr_ref[...], sr_ref[...], ret_kdim // 2)
            k = rot(k, cr_ref[...], sr_ref[...], ret_kdim // 2)
        orq_ref[0, :, sl] = q
        ork_ref[0, :, sl] = k * k_scale


def _qk_prep(proj, g_q, g_k, tabs, cols, *, rope, da_dim, ret_kdim):
    b, n, _ = proj.shape
    d = cols["d"]
    rw = cols["ret_qk"]
    tm = min(n, 512)
    assert n % tm == 0
    gq = jnp.tile(g_q, LANES // da_dim).reshape(1, LANES)
    gk = jnp.tile(g_k, LANES // da_dim).reshape(1, LANES)
    tab_spec = pl.BlockSpec((tm, LANES), lambda bi, i: (i, 0))
    vec_spec = pl.BlockSpec((1, LANES), lambda bi, i: (0, 0))

    def col_spec(width, off):
        assert off % width == 0
        return pl.BlockSpec((1, tm, width), lambda bi, i: (bi, i, off // width))

    kern = functools.partial(_prep_kernel, rope=rope, da_dim=da_dim, ret_kdim=ret_kdim)
    return pl.pallas_call(
        kern,
        out_shape=(jax.ShapeDtypeStruct((b, n, d), BF16), jax.ShapeDtypeStruct((b, n, d), BF16),
                   jax.ShapeDtypeStruct((b, n, rw), F32), jax.ShapeDtypeStruct((b, n, rw), F32)),
        grid=(b, n // tm),
        in_specs=[col_spec(d, cols["aq"]), col_spec(d, cols["ak"]),
                  col_spec(rw, cols["rq"]), col_spec(rw, cols["rk"]),
                  vec_spec, vec_spec, tab_spec, tab_spec, tab_spec, tab_spec],
        out_specs=(pl.BlockSpec((1, tm, d), lambda bi, i: (bi, i, 0)),
                   pl.BlockSpec((1, tm, d), lambda bi, i: (bi, i, 0)),
                   pl.BlockSpec((1, tm, rw), lambda bi, i: (bi, i, 0)),
                   pl.BlockSpec((1, tm, rw), lambda bi, i: (bi, i, 0))),
        compiler_params=_params(("arbitrary", "arbitrary"),
                                tm * (2 * d + 2 * rw) * 4 + tm * (2 * d * 2 + 2 * rw * 4)
                                + 4 * tm * LANES * 4, 8 * MIB),
        name="qk_prep",
    )(proj, proj, proj, proj, gq, gk, *tabs)


def _rope_tables(n_tok, da_dim, ret_kdim):
    t = jnp.arange(n_tok)

    def angles(pos, dim):
        inv = ROPE_BASE ** (-jnp.arange(0, dim, 2, dtype=F32) / dim)
        return pos.astype(F32)[:, None] * inv[None, :]

    def cs(ang):
        c, s = jnp.cos(ang), jnp.sin(ang)
        return jnp.concatenate([c, c], axis=-1), jnp.concatenate([-s, s], axis=-1)

    half = da_dim // 2
    cr_, sr_ = cs(angles(t // GRID_W, half))
    cc_, sc_ = cs(angles(t % GRID_W, half))
    ca = jnp.tile(jnp.concatenate([cr_, cc_], axis=-1), (1, LANES // da_dim))
    sa = jnp.tile(jnp.concatenate([sr_, sc_], axis=-1), (1, LANES // da_dim))
    c1, s1 = cs(angles(t, ret_kdim))
    cr = jnp.tile(c1, (1, LANES // ret_kdim))
    sr = jnp.tile(s1, (1, LANES // ret_kdim))
    return ca, sa, cr, sr


ATTN_MAX_KEY_TILE = 768


def _attn_kernel(scal_ref, q_ref, kx_ref, vx_ref, kc_ref, vc_ref, gsub_ref, o_ref,
                 k_all, v_all, s0_ref, s1_ref, s2_ref, p0_ref, p1_ref, p2_ref, a0_ref, a1_ref, a2_ref,
                 m_ref, acc_ref, *, tq, tk, with_x, da_dim):
    dv = vc_ref.shape[2]
    nc = kc_ref.shape[1]
    nx = kx_ref.shape[1] if with_x else 0
    n_steps = (nx + nc) // tk
    n_tiles = q_ref.shape[1] // tq
    rows = 2 * tq

    k_all[nx:nx + nc, :] = kc_ref[0]
    v_all[nx:nx + nc, 0:dv] = vc_ref[0].astype(BF16)
    v_all[nx:nx + nc, dv:2 * dv] = jnp.ones((nc, dv), BF16)
    if with_x:
        chunk = min(512, nx)

        def fill(j, c):
            off = pl.multiple_of(j * chunk, chunk)
            k_all[pl.ds(off, chunk), :] = kx_ref[0, pl.ds(off, chunk), :]
            v_all[pl.ds(off, chunk), 0:dv] = vx_ref[0, pl.ds(off, chunk), :].astype(BF16)
            v_all[pl.ds(off, chunk), dv:2 * dv] = jnp.ones((chunk, dv), BF16)
            return c

        lax.fori_loop(0, nx // chunk, fill, 0)

    s_bufs, p_bufs, a_bufs = (s0_ref, s1_ref, s2_ref), (p0_ref, p1_ref, p2_ref), (a0_ref, a1_ref, a2_ref)
    lo = _lane_iota((tq, LANES)) < da_dim

    def buf(j):
        return 2 if (n_steps % 2 == 1 and n_steps > 1 and j == n_steps - 1) else j % 2

    def q_start(t):
        return t * tq if isinstance(t, int) else pl.multiple_of(t * tq, tq)

    def qk(t, j):
        q = q_ref[0, pl.ds(q_start(t), tq), :]
        zero = jnp.zeros_like(q)
        q2 = jnp.concatenate([jnp.where(lo, q, zero), jnp.where(lo, zero, q)], axis=0)
        s_bufs[buf(j)][...] = lax.dot_general(q2, k_all[j * tk:(j + 1) * tk, :],
                                              (((1,), (1,)), ((), ())), preferred_element_type=F32)

    def softmax(j):
        b = buf(j)
        s = s_bufs[b][...]
        row_max = jnp.max(s, axis=-1, keepdims=True)
        if j == 0:
            m_new = jnp.broadcast_to(row_max, (rows, LANES))
        else:
            m_prev = m_ref[...]
            m_new = jnp.maximum(m_prev, row_max)
            a_bufs[b][...] = jnp.exp(m_prev - m_new)
        p_bufs[b][...] = jnp.exp(s - jnp.concatenate([m_new] * (tk // LANES), axis=1)).astype(BF16)
        m_ref[...] = m_new

    def pv(j):
        b = buf(j)
        r = jnp.dot(p_bufs[b][...], v_all[j * tk:(j + 1) * tk, :], preferred_element_type=F32)
        if j == 0:
            acc_ref[...] = r
        else:
            acc_ref[...] = jnp.concatenate([a_bufs[b][...]] * (2 * dv // LANES), axis=1) * acc_ref[...] + r

    def finalize(t):
        acc = acc_ref[...]
        o0 = acc[0:tq, 0:dv] / acc[0:tq, dv:dv + 1]
        o1 = acc[tq:2 * tq, 0:dv] / acc[tq:2 * tq, dv:dv + 1]
        o = o0 - scal_ref[0] * o1
        y = o * lax.rsqrt(jnp.mean(o * o, axis=-1, keepdims=True) + EPS) * gsub_ref[...]
        o_ref[0, pl.ds(q_start(t), tq), :] = (y * scal_ref[1]).astype(BF16)

    if n_steps == 1:
        def lone(t, c):
            qk(t, 0)
            softmax(0)
            pv(0)
            finalize(t)
            return c

        lax.fori_loop(0, n_tiles, lone, 0)
    else:
        qk(0, 0)

        def tile(t, c):
            qk(t, 1)
            softmax(0)

            @pl.when(t > 0)
            def _():
                pv(n_steps - 1)
                finalize(t - 1)

            for j in range(1, n_steps - 1):
                qk(t, j + 1)
                softmax(j)
                pv(j - 1)
            qk(jnp.minimum(t + 1, n_tiles - 1), 0)
            softmax(n_steps - 1)
            pv(n_steps - 2)
            return c

        lax.fori_loop(0, n_tiles, tile, 0)
        pv(n_steps - 1)
        finalize(n_tiles - 1)


def _attention(scal, q, kx, proj_x, kc, proj_c, g_sub, cols, *, with_x, da_dim):
    b, nq, d = q.shape
    dv = g_sub.shape[0]
    heads = d // dv
    nc = kc.shape[1]
    nx = kx.shape[1]
    tq = min(nq, 256)
    n_keys = (nx if with_x else 0) + nc
    tk = max(t for t in range(LANES, ATTN_MAX_KEY_TILE + 1, LANES) if n_keys % t == 0)
    assert nq % tq == 0 and 2 * da_dim == LANES and dv == LANES
    v_blk = cols["av"] // dv
    kern = functools.partial(_attn_kernel, tq=tq, tk=tk, with_x=with_x, da_dim=da_dim)
    rows = 2 * tq
    pipelined = 2 * nq * dv * 2 + nx * dv * (2 + 4) + nc * dv * (2 + 4)
    resident = (n_keys * 3 * dv * 2 + 3 * rows * tk * (4 + 2) + 4 * rows * LANES * 4
                + rows * 2 * dv * 4 + 3 * rows * tk * 4)
    return pl.pallas_call(
        kern,
        out_shape=jax.ShapeDtypeStruct((b, nq, d), BF16),
        grid=(b, heads),
        in_specs=[pl.BlockSpec(memory_space=pltpu.SMEM),
                  pl.BlockSpec((1, nq, dv), lambda bi, h: (bi, 0, h)),
                  pl.BlockSpec((1, nx, dv), lambda bi, h: (bi, 0, h)),
                  pl.BlockSpec((1, nx, dv), lambda bi, h: (bi, 0, v_blk + h)),
                  pl.BlockSpec((1, nc, dv), lambda bi, h: (bi, 0, h)),
                  pl.BlockSpec((1, nc, dv), lambda bi, h: (bi, 0, v_blk + h)),
                  pl.BlockSpec((1, dv), lambda bi, h: (0, 0))],
        out_specs=pl.BlockSpec((1, nq, dv), lambda bi, h: (bi, 0, h)),
        scratch_shapes=[pltpu.VMEM((n_keys, dv), BF16), pltpu.VMEM((n_keys, 2 * dv), BF16)]
        + [pltpu.VMEM((rows, tk), F32)] * 3 + [pltpu.VMEM((rows, tk), BF16)] * 3
        + [pltpu.VMEM((rows, LANES), F32)] * 4 + [pltpu.VMEM((rows, 2 * dv), F32)],
        compiler_params=_params(("arbitrary", "arbitrary"), pipelined, resident),
        name="diff_attention" if with_x else "diff_attention_ctx",
    )(scal, q, kx, proj_x, kc, proj_c, g_sub.reshape(1, dv))


CONV_HALO = 16
CONV_ROW_CHUNK = 64


def _conv_kernel(a_ref, g_ref, ap_ref, gp_ref, an_ref, gn_ref, w_ref, cb_ref, lg_ref, lb_ref,
                 o_ref, ext_ref, cv_ref, sh_ref):
    i = pl.program_id(1)
    last = pl.num_programs(1) - 1
    tm = a_ref.shape[1]
    ch = a_ref.shape[2]
    width = w_ref.shape[0]
    pad = width // 2

    def glu(a, g):
        return a * _sigmoid(g)

    ext_ref[0:CONV_HALO, :] = jnp.where(i > 0, glu(ap_ref[0], gp_ref[0]), 0.0)
    ext_ref[CONV_HALO:CONV_HALO + tm, :] = glu(a_ref[0], g_ref[0])
    ext_ref[CONV_HALO + tm:CONV_HALO + tm + CONV_HALO, :] = jnp.where(i < last, glu(an_ref[0], gn_ref[0]), 0.0)

    span = sh_ref.shape[1]
    for c in range(ch // LANES):
        sl = slice(c * LANES, (c + 1) * LANES)
        for s in range(SUBLANES):
            sh_ref[s] = ext_ref[s:s + span, sl]
        def rows(i, carry, sl=sl):
            r0 = pl.multiple_of(i * CONV_ROW_CHUNK, CONV_ROW_CHUNK)
            acc = jnp.zeros((CONV_ROW_CHUNK, LANES), F32)
            for j in range(width):
                start = CONV_HALO - pad + j
                base = start - start % SUBLANES
                acc = acc + w_ref[j:j + 1, sl] * sh_ref[start % SUBLANES, pl.ds(base + r0, CONV_ROW_CHUNK), :]
            cv_ref[pl.ds(r0, CONV_ROW_CHUNK), sl] = acc + cb_ref[:, sl]
            return carry

        lax.fori_loop(0, tm // CONV_ROW_CHUNK, rows, 0)

    v = cv_ref[...]
    mu = jnp.mean(v, axis=-1, keepdims=True)
    vc = v - mu
    var = jnp.mean(vc * vc, axis=-1, keepdims=True)
    y = vc * lax.rsqrt(var + EPS) * lg_ref[...] + lb_ref[...]
    o_ref[0] = (y * _sigmoid(y)).astype(BF16)


def _conformer_conv(proj, conv_w, conv_b, ln_g, ln_b, cols):
    b, n, _ = proj.shape
    width, ch = conv_w.shape
    tm = min(n, 256)
    assert n % tm == 0 and tm % CONV_HALO == 0 and width // 2 < CONV_HALO
    a_blk = cols["conv"] // ch
    g_blk = a_blk + 1
    r = tm // CONV_HALO
    n_halo = n // CONV_HALO

    def cur(blk):
        return pl.BlockSpec((1, tm, ch), lambda bi, i: (bi, i, blk))

    def prev(blk):
        return pl.BlockSpec((1, CONV_HALO, ch), lambda bi, i: (bi, jnp.maximum(i * r - 1, 0), blk))

    def nxt(blk):
        return pl.BlockSpec((1, CONV_HALO, ch), lambda bi, i: (bi, jnp.minimum((i + 1) * r, n_halo - 1), blk))

    vec = pl.BlockSpec((1, ch), lambda bi, i: (0, 0))
    return pl.pallas_call(
        _conv_kernel,
        out_shape=jax.ShapeDtypeStruct((b, n, ch), BF16),
        grid=(b, n // tm),
        in_specs=[cur(a_blk), cur(g_blk), prev(a_blk), prev(g_blk), nxt(a_blk), nxt(g_blk),
                  pl.BlockSpec((width, ch), lambda bi, i: (0, 0)), vec, vec, vec],
        out_specs=pl.BlockSpec((1, tm, ch), lambda bi, i: (bi, i, 0)),
        scratch_shapes=[pltpu.VMEM((tm + 2 * CONV_HALO, ch), F32), pltpu.VMEM((tm, ch), F32),
                        pltpu.VMEM((SUBLANES, tm + 2 * CONV_HALO - SUBLANES, LANES), F32)],
        compiler_params=_params(("arbitrary", "arbitrary"),
                                (2 * tm + 4 * CONV_HALO) * ch * 4 + tm * ch * 2 + 40 * ch * 4,
                                (2 * tm + 2 * CONV_HALO) * ch * 4 + 4 * tm * ch * 4
                                + SUBLANES * (tm + 2 * CONV_HALO) * LANES * 4),
        name="conformer_conv",
    )(proj, proj, proj, proj, proj, proj, conv_w, conv_b.reshape(1, ch), ln_g.reshape(1, ch),
      ln_b.reshape(1, ch))


def _ret_kernel(qf_ref, kf_ref, vf_ref, gf_ref, qb_ref, kb_ref, vb_ref, gb_ref, s0_ref,
                inner_ref, xi_ref, zeta_ref, gc_ref, gng_ref, gnb_ref,
                of_ref, ob_ref, sfin_ref, s_ref, *, kdim):
    i = pl.program_id(1)
    c = qf_ref.shape[1]
    dv = gng_ref.shape[1]
    n_pair = qf_ref.shape[2] // LANES

    @pl.when(i == 0)
    def _():
        s_ref[...] = s0_ref[0]

    lo = _lane_iota((c, LANES)) < kdim
    dirs = ((qf_ref, kf_ref, vf_ref, gf_ref, of_ref), (qb_ref, kb_ref, vb_ref, gb_ref, ob_ref))
    for d, (q_ref, k_ref, v_ref, g_ref, o_ref) in enumerate(dirs):
        for p in range(n_pair):
            sl = slice(p * LANES, (p + 1) * LANES)
            q = q_ref[0, :, sl]
            k = k_ref[0, :, sl]
            kz = k * zeta_ref[d, p]
            kb16 = k.astype(BF16)
            s_pair = s_ref[d, p]
            s16 = s_pair.astype(BF16)
            upd = jnp.zeros((LANES, dv), F32)
            for hh in range(2):
                h = 2 * p + hh
                hsl = slice(h * dv, (h + 1) * dv)
                keep = lo if hh == 0 else jnp.logical_not(lo)
                qh = jnp.where(keep, q, 0.0).astype(BF16)
                v16 = v_ref[0, :, hsl].astype(BF16)
                att = lax.dot_general(qh, kb16, (((1,), (1,)), ((), ())), preferred_element_type=F32)
                att = (att * inner_ref[d, h]).astype(BF16)
                o = (jnp.dot(att, v16, preferred_element_type=F32)
                     + jnp.dot(qh, s16, preferred_element_type=F32) * xi_ref[d, h])
                kzh = jnp.where(keep, kz, 0.0).astype(BF16)
                upd = upd + lax.dot_general(kzh, v16, (((0,), (0,)), ((), ())), preferred_element_type=F32)
                mu = jnp.mean(o, axis=-1, keepdims=True)
                oc = o - mu
                var = jnp.mean(oc * oc, axis=-1, keepdims=True)
                y = oc * lax.rsqrt(var + EPS) * gng_ref[...] + gnb_ref[...]
                g = g_ref[0, :, hsl]
                o_ref[0, :, hsl] = (g * _sigmoid(g)) * y
            s_ref[d, p] = s_pair * gc_ref[p] + upd

    sfin_ref[0] = s_ref[...]


def _retention(rq, rk, proj, s0, tabs, gn_g, gn_b, cols):
    b, n, rw = rq.shape
    dv = gn_g.shape[0]
    d = cols["d"]
    c = RET_CHUNK
    nch = n // c
    kdim = dv // 2
    assert n % c == 0 and c == LANES and 2 * kdim == LANES
    n_pair = rw // LANES
    inner, xi, zeta, gc = tabs
    v_blk, gf_blk, gb_blk = cols["rv"] // d, cols["rgf"] // d, cols["rgb"] // d

    def fwd(width, blk):
        return pl.BlockSpec((1, c, width), lambda bi, i: (bi, i, blk))

    def bwd(width, blk):
        return pl.BlockSpec((1, c, width), lambda bi, i: (bi, nch - 1 - i, blk))

    def whole(a):
        return pl.BlockSpec(a.shape, lambda bi, i: (0,) * a.ndim)

    state = pl.BlockSpec((1, 2, n_pair, LANES, dv), lambda bi, i: (bi, 0, 0, 0, 0))
    vec = pl.BlockSpec((1, dv), lambda bi, i: (0, 0))
    kern = functools.partial(_ret_kernel, kdim=kdim)
    pipelined = 2 * c * (2 * rw + 2 * d) * 4 + 2 * c * d * 4 + 2 * 2 * n_pair * LANES * dv * 4
    resident = (inner.size + xi.size + zeta.size + gc.size) * 4 * 2 + 2 * n_pair * LANES * dv * 4 + 8 * MIB
    return pl.pallas_call(
        kern,
        out_shape=(jax.ShapeDtypeStruct((b, n, d), F32), jax.ShapeDtypeStruct((b, n, d), F32),
                   jax.ShapeDtypeStruct((b, 2, n_pair, LANES, dv), F32)),
        grid=(b, nch),
        in_specs=[fwd(rw, 0), fwd(rw, 0), fwd(d, v_blk), fwd(d, gf_blk),
                  bwd(rw, 0), bwd(rw, 0), bwd(d, v_blk), bwd(d, gb_blk),
                  state, whole(inner), whole(xi), whole(zeta), whole(gc), vec, vec],
        out_specs=(pl.BlockSpec((1, c, d), lambda bi, i: (bi, i, 0)),
                   pl.BlockSpec((1, c, d), lambda bi, i: (bi, nch - 1 - i, 0)),
                   state),
        scratch_shapes=[pltpu.VMEM((2, n_pair, LANES, dv), F32)],
        compiler_params=_params(("arbitrary", "arbitrary"), pipelined, resident),
        name="retention",
    )(rq, rk, proj, proj, rq, rk, proj, proj, s0, inner, xi, zeta, gc,
      gn_g.reshape(1, dv), gn_b.reshape(1, dv))


def _retention_tables(heads, kdim, dv):
    c = RET_CHUNK
    lg = jnp.log1p(-jnp.exp2(-5.0 - jnp.arange(heads, dtype=F32)))
    pos = jnp.arange(c, dtype=F32)
    diff = pos[:, None] - pos[None, :]
    inner_f = jnp.where(diff[None] >= 0, jnp.exp(jnp.maximum(diff, 0.0)[None] * lg[:, None, None]), 0.0)
    inner = jnp.stack([inner_f, jnp.swapaxes(inner_f, 1, 2)])
    xi_f = jnp.exp((pos + 1.0)[None, :] * lg[:, None])
    zeta_f = jnp.exp((c - 1.0 - pos)[None, :] * lg[:, None])
    xi = jnp.stack([xi_f, xi_f[:, ::-1]])
    zeta = jnp.stack([zeta_f, zeta_f[:, ::-1]])
    xi = jnp.broadcast_to(xi[..., None], (2, heads, c, dv))
    zeta = jnp.broadcast_to(zeta[..., None], (2, heads, c, kdim))
    zeta = zeta.reshape(2, heads // 2, 2, c, kdim).transpose(0, 1, 3, 2, 4).reshape(2, heads // 2, c, 2 * kdim)
    gc = jnp.exp(c * lg)
    gc = jnp.broadcast_to(gc[:, None, None], (heads, kdim, dv)).reshape(heads // 2, 2 * kdim, dv)
    return inner.astype(F32), xi.astype(F32), zeta.astype(F32), gc.astype(F32)


def _merge_kernel(h_ref, mod_ref, oa_ref, ob_ref, of_ref, obk_ref, gate_ref, bg_ref,
                  wa_ref, wb_ref, wc_ref, wo_ref, o_ref):
    d = h_ref.shape[2]
    g = _sigmoid(gate_ref[0] + bg_ref[...])
    oc = (of_ref[0] + obk_ref[0]).astype(BF16)
    y = g[:, 0:d] * jnp.dot(oa_ref[0], wa_ref[...], preferred_element_type=F32)
    y = y + g[:, d:2 * d] * jnp.dot(ob_ref[0], wb_ref[...], preferred_element_type=F32)
    y = y + g[:, 2 * d:3 * d] * jnp.dot(oc, wc_ref[...], preferred_element_type=F32)
    m = jnp.dot(y.astype(BF16), wo_ref[...], preferred_element_type=F32)
    o_ref[0] = h_ref[0] + mod_ref[0, 2:3, :] * m


def _merge(h, mod, oa, ob, of, obk, proj, b_gate, w_pa, w_pb, w_pc, w_out, layer, cols):
    b, n, d = h.shape
    tm = min(n, 512)
    assert n % tm == 0
    mb = mod.shape[0]
    gate_blk = cols["gate"] // (3 * d)
    row = lambda width: pl.BlockSpec((1, tm, width), lambda bi, i: (bi, i, 0))
    wspec = pl.BlockSpec((None, d, d), lambda bi, i: (layer, 0, 0))
    return pl.pallas_call(
        _merge_kernel,
        out_shape=jax.ShapeDtypeStruct((b, n, d), F32),
        grid=(b, n // tm),
        in_specs=[row(d), pl.BlockSpec((1, N_MOD, d), lambda bi, i: (bi % mb, 0, 0)),
                  row(d), row(d), row(d), row(d),
                  pl.BlockSpec((1, tm, 3 * d), lambda bi, i: (bi, i, gate_blk)),
                  pl.BlockSpec((1, 3 * d), lambda bi, i: (0, 0)),
                  wspec, wspec, wspec, wspec],
        out_specs=row(d),
        compiler_params=_params(("arbitrary", "arbitrary"),
                                tm * d * (4 + 2 + 2 + 4 + 4 + 12 + 4) + 4 * d * d * 2, 10 * tm * d * 4),
        name="merge",
    )(h, mod, oa, ob, of, obk, proj, b_gate.reshape(1, 3 * d), w_pa, w_pb, w_pc, w_out)


ROUTE_E, ROUTE_G, ROUTE_R = 0, 4, 8


def _route_kernel(h_ref, mod_ref, g_ref, wr_ref, br_ref, tri_ref, base_ref,
                  hx_ref, route_ref, cnt_ref, run_ref, *, n_exp):
    first = jnp.logical_and(pl.program_id(0) == 0, pl.program_id(1) == 0)

    @pl.when(first)
    def _():
        run_ref[...] = base_ref[...]

    h = h_ref[0]
    tm = h.shape[0]
    y = h * lax.rsqrt(jnp.mean(h * h, axis=-1, keepdims=True) + EPS) * g_ref[...]
    hx = y * (1.0 + mod_ref[0, 4:5, :]) + mod_ref[0, 3:4, :]
    hx_ref[0] = hx

    lane = _lane_iota((tm, LANES))
    lane_f = lane.astype(F32)
    logits = jnp.dot(hx, wr_ref[...], preferred_element_type=F32,
                     precision=lax.Precision.HIGHEST) + br_ref[...]
    logits = jnp.where(lane < n_exp, logits, NEG_INF)

    sels, vals = [], []
    for _ in range(TOP_K):
        mx = jnp.max(logits, axis=-1, keepdims=True)
        idx = jnp.min(jnp.where(logits == mx, lane_f, float(LANES)), axis=-1, keepdims=True)
        sel = lane_f == idx
        sels.append((sel, idx))
        vals.append(mx)
        logits = jnp.where(sel, NEG_INF, logits)

    es = [jnp.exp(v - vals[0]) for v in vals]
    denom = es[0]
    for e in es[1:]:
        denom = denom + e

    onehot = jnp.zeros((tm, LANES), F32)
    for sel, _ in sels:
        onehot = onehot + jnp.where(sel, 1.0, 0.0)
    before = run_ref[...] + jnp.dot(tri_ref[...], onehot.astype(BF16), preferred_element_type=F32)

    route = jnp.zeros((tm, LANES), F32)
    for k, (sel, idx) in enumerate(sels):
        rank = jnp.sum(jnp.where(sel, before, 0.0), axis=-1, keepdims=True)
        route = jnp.where(lane == ROUTE_E + k, idx, route)
        route = jnp.where(lane == ROUTE_G + k, es[k] / denom, route)
        route = jnp.where(lane == ROUTE_R + k, rank, route)
    route_ref[0] = route

    run_ref[...] = run_ref[...] + jnp.sum(onehot, axis=0, keepdims=True)
    cnt_ref[...] = run_ref[...]


def _route(h, mod, g, w_router, b_router, base):
    b, n, d = h.shape
    n_exp = w_router.shape[-1]
    tm = min(n, 256)
    assert n % tm == 0 and n_exp <= LANES
    mb = mod.shape[0]
    wr = jnp.zeros((d, LANES), F32).at[:, :n_exp].set(w_router)
    br = jnp.zeros((1, LANES), F32).at[0, :n_exp].set(b_router)
    tri = (jnp.arange(tm)[:, None] > jnp.arange(tm)[None, :]).astype(BF16)
    row = lambda width: pl.BlockSpec((1, tm, width), lambda bi, i: (bi, i, 0))
    const = lambda shape: pl.BlockSpec(shape, lambda bi, i: (0,) * len(shape))
    kern = functools.partial(_route_kernel, n_exp=n_exp)
    return pl.pallas_call(
        kern,
        out_shape=(jax.ShapeDtypeStruct((b, n, d), F32), jax.ShapeDtypeStruct((b, n, LANES), F32),
                   jax.ShapeDtypeStruct((1, LANES), F32)),
        grid=(b, n // tm),
        in_specs=[row(d), pl.BlockSpec((1, N_MOD, d), lambda bi, i: (bi % mb, 0, 0)), const((1, d)),
                  const((d, LANES)), const((1, LANES)), const((tm, tm)), const((1, LANES))],
        out_specs=(row(d), row(LANES), const((1, LANES))),
        scratch_shapes=[pltpu.VMEM((1, LANES), F32)],
        compiler_params=_params(("arbitrary", "arbitrary"),
                                2 * tm * d * 4 + tm * LANES * 4 + d * LANES * 4 + tm * tm * 2, 8 * tm * d * 4),
        name="moe_route",
    )(h, mod, g.reshape(1, d), wr, br, tri, base)


DMA_ISSUE_UNROLL = 8


def _tile_dest(dest, tm):
    t = dest.shape[0]
    return dest.reshape(t // tm, tm, TOP_K).transpose(0, 2, 1).reshape(t // tm, 1, TOP_K * tm)


def _dispatch_kernel(dest_ref, hx_ref, xs_in_ref, xs_ref, sem):
    del xs_in_ref
    tm = hx_ref.shape[0]

    def copy(k, t):
        return pltpu.make_async_copy(hx_ref.at[pl.ds(t, 1)],
                                     xs_ref.at[pl.ds(dest_ref[0, 0, k * tm + t], 1)], sem)

    for t in range(tm):
        for k in range(TOP_K):
            copy(k, t).start(priority=(t * TOP_K + k) % 2)
    for k in range(TOP_K):
        def wait(t, c, k=k):
            copy(k, t).wait()
            return c

        lax.fori_loop(0, tm, wait, 0, unroll=DMA_ISSUE_UNROLL)


def _dispatch(hx_flat, dest, xs):
    t, d = hx_flat.shape
    tm = min(t, 256)
    assert t % tm == 0
    dest3 = _tile_dest(dest, tm)
    return pl.pallas_call(
        _dispatch_kernel,
        out_shape=jax.ShapeDtypeStruct(xs.shape, xs.dtype),
        grid=(t // tm,),
        in_specs=[pl.BlockSpec((1, 1, tm * TOP_K), lambda i: (i, 0, 0), memory_space=pltpu.SMEM),
                  pl.BlockSpec((tm, d), lambda i: (i, 0)),
                  pl.BlockSpec(memory_space=pl.ANY)],
        out_specs=pl.BlockSpec(memory_space=pl.ANY),
        scratch_shapes=[pltpu.SemaphoreType.DMA],
        input_output_aliases={2: 0},
        compiler_params=pltpu.CompilerParams(dimension_semantics=("arbitrary",),
                                             vmem_limit_bytes=_vmem_limit(tm * d * 4, 4 * MIB),
                                             has_side_effects=True),
        name="moe_dispatch",
    )(dest3, hx_flat, xs)


def _split_kernel(w_ref, sel_ref, g_ref, u_ref):
    de = g_ref.shape[-1]
    r = jnp.dot(w_ref[...].astype(BF16), sel_ref[...], preferred_element_type=F32)
    g_ref[...] = r[:, :de].astype(BF16)
    u_ref[...] = r[:, de:].astype(BF16)


def _split_gate_up(w_gate_up):
    depth, n_exp, d, de2 = w_gate_up.shape
    de = de2 // 2
    tr = min(d, 512)
    assert d % tr == 0
    order = jnp.concatenate([jnp.arange(0, de2, 2), jnp.arange(1, de2, 2)])
    sel = (jnp.arange(de2)[:, None] == order[None, :]).astype(BF16)
    out = jax.ShapeDtypeStruct((depth, n_exp, d, de), BF16)
    ospec = pl.BlockSpec((None, None, tr, de), lambda l, e, r: (l, e, r, 0))
    return pl.pallas_call(
        _split_kernel,
        out_shape=(out, out),
        grid=(depth, n_exp, d // tr),
        in_specs=[pl.BlockSpec((None, None, tr, de2), lambda l, e, r: (l, e, r, 0)),
                  pl.BlockSpec((de2, de2), lambda l, e, r: (0, 0))],
        out_specs=(ospec, ospec),
        compiler_params=_params(("arbitrary", "arbitrary", "arbitrary"),
                                tr * de2 * 4 + de2 * de2 * 2 + 2 * tr * de * 2, tr * de2 * (2 + 4 + 4)),
        name="split_gate_up",
    )(w_gate_up, sel)


def _expert_kernel(blk_e_ref, n_used_ref, x_ref, wg_ref, wu_ref, wd_ref, bg_ref, bu_ref, bd_ref, y_ref):
    del blk_e_ref
    used = pl.program_id(0) < n_used_ref[0]

    @pl.when(used)
    def _():
        x = x_ref[...].astype(BF16)
        gl = jnp.dot(x, wg_ref[...], preferred_element_type=F32) + bg_ref[...]
        up = jnp.dot(x, wu_ref[...], preferred_element_type=F32) + bu_ref[...]
        gl = jnp.minimum(gl, SWIGLU_LIMIT)
        up = jnp.clip(up, -SWIGLU_LIMIT, SWIGLU_LIMIT)
        act = (up + 1.0) * (gl * _sigmoid(SWIGLU_ALPHA * gl))
        y_ref[...] = jnp.dot(act.astype(BF16), wd_ref[...], preferred_element_type=F32) + bd_ref[...]

    @pl.when(jnp.logical_not(used))
    def _():
        y_ref[...] = jnp.zeros(y_ref.shape, F32)


def _experts(blk_e, n_used, xs, wg, wu, wd, bg, bu, bd, layer, bm):
    n_rows, d = xs.shape
    de = wg.shape[-1]
    nb = n_rows // bm
    wspec = lambda a, c: pl.BlockSpec((None, None, a, c), lambda i, be, nu: (layer, be[i], 0, 0))
    return pl.pallas_call(
        _expert_kernel,
        out_shape=jax.ShapeDtypeStruct((n_rows, d), F32),
        grid_spec=pltpu.PrefetchScalarGridSpec(
            num_scalar_prefetch=2,
            grid=(nb,),
            in_specs=[pl.BlockSpec((bm, d), lambda i, be, nu: (i, 0)),
                      wspec(d, de), wspec(d, de), wspec(de, d),
                      wspec(1, de), wspec(1, de), wspec(1, d)],
            out_specs=pl.BlockSpec((bm, d), lambda i, be, nu: (i, 0))),
        compiler_params=_params(("arbitrary",), 2 * bm * d * 4 + 3 * d * de * 2, 8 * bm * de * 4),
        name="moe_experts",
    )(blk_e, n_used, xs, wg, wu, wd, bg, bu, bd)


def _combine_kernel(dest_ref, h_ref, mod_ref, route_ref, y_ref, o_ref, ybuf, sem):
    tm = h_ref.shape[1]

    def copy(r):
        return pltpu.make_async_copy(y_ref.at[pl.ds(dest_ref[0, 0, r], 1)], ybuf.at[pl.ds(r, 1)], sem)

    for r in range(TOP_K * tm):
        copy(r).start(priority=r % 2)

    def wait(r, c):
        copy(r).wait()
        return c

    lax.fori_loop(0, TOP_K * tm, wait, 0, unroll=DMA_ISSUE_UNROLL)

    route = route_ref[0]
    f = jnp.zeros((tm, h_ref.shape[2]), F32)
    for k in range(TOP_K):
        f = f + route[:, ROUTE_G + k:ROUTE_G + k + 1] * ybuf[k * tm:(k + 1) * tm, :]
    o_ref[0] = h_ref[0] + mod_ref[0, 5:6, :] * f


def _combine(h, mod, route, dest, y):
    b, n, d = h.shape
    tm = min(n, 256)
    assert n % tm == 0
    mb = mod.shape[0]
    nt = n // tm
    dest3 = _tile_dest(dest, tm)
    row = lambda width: pl.BlockSpec((1, tm, width), lambda bi, i: (bi, i, 0))
    return pl.pallas_call(
        _combine_kernel,
        out_shape=jax.ShapeDtypeStruct((b, n, d), F32),
        grid=(b, nt),
        in_specs=[pl.BlockSpec((1, 1, tm * TOP_K), lambda bi, i: (bi * nt + i, 0, 0), memory_space=pltpu.SMEM),
                  row(d), pl.BlockSpec((1, N_MOD, d), lambda bi, i: (bi % mb, 0, 0)), row(LANES),
                  pl.BlockSpec(memory_space=pl.ANY)],
        out_specs=row(d),
        scratch_shapes=[pltpu.VMEM((TOP_K * tm, d), F32), pltpu.SemaphoreType.DMA],
        compiler_params=_params(("arbitrary", "arbitrary"), 2 * tm * d * 4 + tm * LANES * 4,
                                TOP_K * tm * d * 4 + 4 * tm * d * 4),
        name="moe_combine",
    )(dest3, h, mod, route, y)


def _moe_rows(n_tok, n_exp, bm):
    return -(-(n_tok * TOP_K + n_exp * (bm - 1)) // bm) * bm


def _moe(parts, xs, g, w_router, b_router, wg, wu, wd, bg, bu, bd, layer, bm):
    n_exp = w_router.shape[-1]
    n_rows, d = xs.shape
    base = jnp.zeros((1, LANES), F32)
    hxs, routes = [], []
    for h, mod in parts:
        hx, route, base = _route(h, mod, g, w_router, b_router, base)
        hxs.append(hx)
        routes.append(route)
    counts = base[0, :n_exp].astype(jnp.int32)

    padded = (counts + bm - 1) // bm * bm
    pend = jnp.cumsum(padded)
    pstart = pend - padded
    n_tok = sum(h.shape[0] * h.shape[1] for h, _ in parts)
    assert n_rows >= _moe_rows(n_tok, n_exp, bm) and n_rows % bm == 0
    nb = n_rows // bm
    blk_start = jnp.arange(nb, dtype=jnp.int32) * bm
    blk_e = jnp.minimum(jnp.sum((pend[None, :] <= blk_start[:, None]).astype(jnp.int32), axis=1), n_exp - 1)
    n_used = (pend[-1] // bm).astype(jnp.int32).reshape(1)
    eids = jnp.arange(n_exp, dtype=jnp.int32)

    dests = []
    for hx, route in zip(hxs, routes):
        e = route[..., ROUTE_E:ROUTE_E + TOP_K].astype(jnp.int32)
        rank = route[..., ROUTE_R:ROUTE_R + TOP_K].astype(jnp.int32)
        start = jnp.sum(jnp.where(e[..., None] == eids, pstart, 0), axis=-1)
        dest = (start + rank).reshape(-1, TOP_K)
        dests.append(dest)
        xs = _dispatch(hx.reshape(-1, d), dest, xs)

    y = _experts(blk_e, n_used, xs, wg, wu, wd, bg, bu, bd, layer, bm)
    return [_combine(h, mod, route, dest, y)
            for (h, mod), route, dest in zip(parts, routes, dests)], xs


def kernel(x, c, ctx, c_ctx, w_mod, b_mod, g_norm1, g_norm2, w_in, b_gate, g_qa, g_ka, lam_q1, lam_k1, lam_q2, lam_k2, g_sub, conv_w, conv_b, ln_g, ln_b, gn_g, gn_b, w_pa, w_pb, w_pc, w_out, w_router, b_router, w_gate_up, b_gate_up, w_down, b_down):
    bsz, n_x, d = x.shape
    n_c = ctx.shape[1]
    depth = w_mod.shape[0]
    da_dim = g_qa.shape[-1]
    dv = g_sub.shape[-1]
    heads = d // dv
    ret_kdim = gn_g.shape[-1] // 2
    n_exp = w_router.shape[-1]
    de = w_down.shape[-2]
    moe_bm = 512

    cols = {"d": d, "ret_qk": heads * ret_kdim}
    off = 0
    for name, width in (("aq", d), ("ak", d), ("av", d), ("conv", 2 * d), ("rq", heads * ret_kdim),
                        ("rk", heads * ret_kdim), ("rv", d), ("rgf", d), ("rgb", d), ("gate", 3 * d)):
        cols[name] = off
        off += width
    assert off == w_in.shape[-1]

    w_in_bf = w_in.astype(BF16)
    w_pa_bf, w_pb_bf, w_pc_bf, w_out_bf = (w.astype(BF16) for w in (w_pa, w_pb, w_pc, w_out))
    wg, wu = _split_gate_up(w_gate_up)
    wd = w_down.astype(BF16)
    bg = b_gate_up[..., 0::2].reshape(depth, n_exp, 1, de)
    bu = b_gate_up[..., 1::2].reshape(depth, n_exp, 1, de)
    bd = b_down.reshape(depth, n_exp, 1, d)

    rows = -(-(bsz + 1) // SUBLANES) * SUBLANES
    cvecs = jnp.zeros((rows, d), F32).at[:bsz].set(c).at[bsz].set(c_ctx)
    mods = _modulation(cvecs, w_mod, b_mod).reshape(depth, rows, N_MOD, d)

    tabs_x = _rope_tables(n_x, da_dim, ret_kdim)
    tabs_c = tuple(jnp.zeros((n_c, LANES), F32) for _ in range(4))
    ret_tabs = _retention_tables(heads, ret_kdim, dv)
    s_zero = jnp.zeros((bsz, 2, heads // 2, LANES, dv), F32)
    xs_rows = jnp.zeros((_moe_rows(bsz * (n_c + n_x), n_exp, moe_bm), d), F32)

    h_ctx = ctx
    for l in range(depth):
        need_ctx = l < depth - 1
        lam_init = 0.8 - 0.6 * math.exp(-0.3 * l)
        lam = (jnp.exp(jnp.sum(lam_q1[l] * lam_k1[l])) - jnp.exp(jnp.sum(lam_q2[l] * lam_k2[l])) + lam_init)
        scal = jnp.stack([lam, jnp.asarray(1.0 - lam_init, F32)]).astype(F32)
        mod_x = mods[l, :bsz]
        mod_c = mods[l, bsz:bsz + 1]

        proj_x = _in_proj(x, mod_x, g_norm1[l], w_in_bf, l)
        proj_c = _in_proj(h_ctx, mod_c, g_norm1[l], w_in_bf, l)
        aq_x, ak_x, rq_x, rk_x = _qk_prep(proj_x, g_qa[l], g_ka[l], tabs_x, cols, rope=True,
                                          da_dim=da_dim, ret_kdim=ret_kdim)
        aq_c, ak_c, rq_c, rk_c = _qk_prep(proj_c, g_qa[l], g_ka[l], tabs_c, cols, rope=False,
                                          da_dim=da_dim, ret_kdim=ret_kdim)
        oa_x = _attention(scal, aq_x, ak_x, proj_x, ak_c, proj_c, g_sub[l], cols, with_x=True, da_dim=da_dim)
        ob_x = _conformer_conv(proj_x, conv_w[l], conv_b[l], ln_g[l], ln_b[l], cols)
        of_c, ob_c, s_ctx = _retention(rq_c, rk_c, proj_c, s_zero, ret_tabs, gn_g[l], gn_b[l], cols)
        of_x, obk_x, _ = _retention(rq_x, rk_x, proj_x, s_ctx, ret_tabs, gn_g[l], gn_b[l], cols)
        x = _merge(x, mod_x, oa_x, ob_x, of_x, obk_x, proj_x, b_gate[l],
                   w_pa_bf, w_pb_bf, w_pc_bf, w_out_bf, l, cols)
        if need_ctx:
            oa_c = _attention(scal, aq_c, ak_c, proj_c, ak_c, proj_c, g_sub[l], cols, with_x=False, da_dim=da_dim)
            cb_c = _conformer_conv(proj_c, conv_w[l], conv_b[l], ln_g[l], ln_b[l], cols)
            h_ctx = _merge(h_ctx, mod_c, oa_c, cb_c, of_c, ob_c, proj_c, b_gate[l],
                           w_pa_bf, w_pb_bf, w_pc_bf, w_out_bf, l, cols)

        parts = [(h_ctx, mod_c), (x, mod_x)] if need_ctx else [(x, mod_x)]
        outs, xs_rows = _moe(parts, xs_rows, g_norm2[l], w_router[l], b_router[l], wg, wu, wd, bg, bu, bd,
                             l, moe_bm)
        if need_ctx:
            h_ctx, x = outs
        else:
            (x,) = outs
    return x
```

```python
import functools
import math

import jax
import jax.numpy as jnp
from jax import lax
from jax.experimental import pallas as pl
from jax.experimental.pallas import tpu as pltpu

F32 = jnp.float32
BF16 = jnp.bfloat16

GRID_W = 64
N_MOD = 6
EPS = 1e-6
ROPE_BASE = 10000.0
RET_CHUNK = 128
TOP_K = 4
SWIGLU_ALPHA = 1.702
SWIGLU_LIMIT = 7.0

LANES = 128
SUBLANES = 8
V7X_VMEM_BYTES = 64 * 2**20
VMEM_LIMIT_CAP = 56 * 2**20
MIB = 2**20

NEG_INF = float("-inf")


def _sigmoid(x):
    return 1.0 / (1.0 + jnp.exp(-x))


def _vmem_limit(pipelined_bytes, resident_bytes):
    need = 2 * pipelined_bytes + resident_bytes
    return int(min(max(need, 16 * MIB), VMEM_LIMIT_CAP))


def _params(sem, pipelined_bytes, resident_bytes):
    return pltpu.CompilerParams(dimension_semantics=sem,
                                vmem_limit_bytes=_vmem_limit(pipelined_bytes, resident_bytes))


def _lane_iota(shape):
    return lax.broadcasted_iota(jnp.int32, shape, len(shape) - 1)


def _mod_kernel(c_ref, w_ref, b_ref, o_ref):
    c = c_ref[...]
    a = (c * _sigmoid(c)).astype(BF16)
    o_ref[...] = jnp.dot(a, w_ref[...].astype(BF16), preferred_element_type=F32) + b_ref[...]


def _modulation(cvecs, w_mod, b_mod):
    depth, d, cols = w_mod.shape
    rows = cvecs.shape[0]
    tn = 1536
    assert cols % tn == 0
    return pl.pallas_call(
        _mod_kernel,
        out_shape=jax.ShapeDtypeStruct((depth, rows, cols), F32),
        grid=(depth, cols // tn),
        in_specs=[pl.BlockSpec((rows, d), lambda l, j: (0, 0)),
                  pl.BlockSpec((None, d, tn), lambda l, j: (l, 0, j)),
                  pl.BlockSpec((None, 1, tn), lambda l, j: (l, 0, j))],
        out_specs=pl.BlockSpec((None, rows, tn), lambda l, j: (l, 0, j)),
        compiler_params=_params(("arbitrary", "arbitrary"), d * tn * 4 + rows * tn * 4, 4 * MIB),
        name="modulation",
    )(cvecs, w_mod, b_mod.reshape(depth, 1, cols))


def _inproj_kernel(h_ref, mod_ref, g_ref, w_ref, o_ref, xs_ref):
    @pl.when(pl.program_id(2) == 0)
    def _():
        h = h_ref[0]
        y = h * lax.rsqrt(jnp.mean(h * h, axis=-1, keepdims=True) + EPS) * g_ref[...]
        xs_ref[...] = (y * (1.0 + mod_ref[0, 1:2, :]) + mod_ref[0, 0:1, :]).astype(BF16)

    o_ref[0] = jnp.dot(xs_ref[...], w_ref[...], preferred_element_type=F32)


def _in_proj(h, mod, g, w_in_bf, layer):
    b, n, d = h.shape
    cols = w_in_bf.shape[-1]
    tm = min(n, 1024)
    tn = 2048
    assert n % tm == 0 and cols % tn == 0
    mb = mod.shape[0]
    return pl.pallas_call(
        _inproj_kernel,
        out_shape=jax.ShapeDtypeStruct((b, n, cols), F32),
        grid=(b, n // tm, cols // tn),
        in_specs=[pl.BlockSpec((1, tm, d), lambda bi, i, j: (bi, i, 0)),
                  pl.BlockSpec((1, N_MOD, d), lambda bi, i, j: (bi % mb, 0, 0)),
                  pl.BlockSpec((1, d), lambda bi, i, j: (0, 0)),
                  pl.BlockSpec((None, d, tn), lambda bi, i, j: (layer, 0, j))],
        out_specs=pl.BlockSpec((1, tm, tn), lambda bi, i, j: (bi, i, j)),
        scratch_shapes=[pltpu.VMEM((tm, d), BF16)],
        compiler_params=_params(("arbitrary", "arbitrary", "arbitrary"),
                                tm * d * 4 + d * tn * 2 + tm * tn * 4, tm * d * 2 + 3 * tm * d * 4),
        name="in_proj",
    )(h, mod, g.reshape(1, d), w_in_bf)


def _prep_kernel(aq_ref, ak_ref, rq_ref, rk_ref, gq_ref, gk_ref, ca_ref, sa_ref, cr_ref, sr_ref,
                 oq_ref, ok_ref, orq_ref, ork_ref, *, rope, da_dim, ret_kdim):
    tm = aq_ref.shape[1]
    lane = _lane_iota((tm, LANES))
    lo = lane < da_dim

    def rms(x, g):
        x2 = x * x
        s_lo = jnp.sum(jnp.where(lo, x2, 0.0), axis=-1, keepdims=True)
        s_hi = jnp.sum(jnp.where(lo, 0.0, x2), axis=-1, keepdims=True)
        ms = jnp.where(lo, s_lo, s_hi) * (1.0 / da_dim)
        return x * lax.rsqrt(ms + EPS) * g

    def rot(x, c, s, half):
        first = (lane % (2 * half)) < half
        partner = jnp.where(first, pltpu.roll(x, LANES - half, 1), pltpu.roll(x, half, 1))
        return x * c + partner * s

    q_scale = da_dim ** -0.5
    k_scale = ret_kdim ** -0.5
    for j in range(aq_ref.shape[2] // LANES):
        sl = slice(j * LANES, (j + 1) * LANES)
        q = rms(aq_ref[0, :, sl], gq_ref[...])
        k = rms(ak_ref[0, :, sl], gk_ref[...])
        if rope:
            q = rot(q, ca_ref[...], sa_ref[...], da_dim // 4)
            k = rot(k, ca_ref[...], sa_ref[...], da_dim // 4)
        oq_ref[0, :, sl] = (q * q_scale).astype(BF16)
        ok_ref[0, :, sl] = k.astype(BF16)
    for j in range(rq_ref.shape[2] // LANES):
        sl = slice(j * LANES, (j + 1) * LANES)
        q = rq_ref[0, :, sl]
        k = rk_ref[0, :, sl]
        if rope:
            q = rot(q, cr_ref[...], sr_ref[...], ret_kdim // 2)
            k = rot(k, cr_ref[...], sr_ref[...], ret_kdim // 2)
        orq_ref[0, :, sl] = q
        ork_ref[0, :, sl] = k * k_scale


def _qk_prep(proj, g_q, g_k, tabs, cols, *, rope, da_dim, ret_kdim):
    b, n, _ = proj.shape
    d = cols["d"]
    rw = cols["ret_qk"]
    tm = min(n, 512)
    assert n % tm == 0
    gq = jnp.tile(g_q, LANES // da_dim).reshape(1, LANES)
    gk = jnp.tile(g_k, LANES // da_dim).reshape(1, LANES)
    tab_spec = pl.BlockSpec((tm, LANES), lambda bi, i: (i, 0))
    vec_spec = pl.BlockSpec((1, LANES), lambda bi, i: (0, 0))

    def col_spec(width, off):
        assert off % width == 0
        return pl.BlockSpec((1, tm, width), lambda bi, i: (bi, i, off // width))

    kern = functools.partial(_prep_kernel, rope=rope, da_dim=da_dim, ret_kdim=ret_kdim)
    return pl.pallas_call(
        kern,
        out_shape=(jax.ShapeDtypeStruct((b, n, d), BF16), jax.ShapeDtypeStruct((b, n, d), BF16),
                   jax.ShapeDtypeStruct((b, n, rw), F32), jax.ShapeDtypeStruct((b, n, rw), F32)),
        grid=(b, n // tm),
        in_specs=[col_spec(d, cols["aq"]), col_spec(d, cols["ak"]),
                  col_spec(rw, cols["rq"]), col_spec(rw, cols["rk"]),
                  vec_spec, vec_spec, tab_spec, tab_spec, tab_spec, tab_spec],
        out_specs=(pl.BlockSpec((1, tm, d), lambda bi, i: (bi, i, 0)),
                   pl.BlockSpec((1, tm, d), lambda bi, i: (bi, i, 0)),
                   pl.BlockSpec((1, tm, rw), lambda bi, i: (bi, i, 0)),
                   pl.BlockSpec((1, tm, rw), lambda bi, i: (bi, i, 0))),
        compiler_params=_params(("arbitrary", "arbitrary"),
                                tm * (2 * d + 2 * rw) * 4 + tm * (2 * d * 2 + 2 * rw * 4)
                                + 4 * tm * LANES * 4, 8 * MIB),
        name="qk_prep",
    )(proj, proj, proj, proj, gq, gk, *tabs)


def _rope_tables(n_tok, da_dim, ret_kdim):
    t = jnp.arange(n_tok)

    def angles(pos, dim):
        inv = ROPE_BASE ** (-jnp.arange(0, dim, 2, dtype=F32) / dim)
        return pos.astype(F32)[:, None] * inv[None, :]

    def cs(ang):
        c, s = jnp.cos(ang), jnp.sin(ang)
        return jnp.concatenate([c, c], axis=-1), jnp.concatenate([-s, s], axis=-1)

    half = da_dim // 2
    cr_, sr_ = cs(angles(t // GRID_W, half))
    cc_, sc_ = cs(angles(t % GRID_W, half))
    ca = jnp.tile(jnp.concatenate([cr_, cc_], axis=-1), (1, LANES // da_dim))
    sa = jnp.tile(jnp.concatenate([sr_, sc_], axis=-1), (1, LANES // da_dim))
    c1, s1 = cs(angles(t, ret_kdim))
    cr = jnp.tile(c1, (1, LANES // ret_kdim))
    sr = jnp.tile(s1, (1, LANES // ret_kdim))
    return ca, sa, cr, sr


ATTN_MAX_KEY_TILE = 768
ATTN_ONES_ROWS = 16


def _attn_kernel(scal_ref, q_ref, kx_ref, vx_ref, kc_ref, vc_ref, gsub_ref, o_ref,
                 k_all, v_all, s0_ref, s1_ref, s2_ref, p0_ref, p1_ref, p2_ref, a0_ref, a1_ref, a2_ref,
                 m_ref, acc_ref, *, tq, tk, with_x, da_dim):
    dv = vc_ref.shape[2]
    nc = kc_ref.shape[1]
    nx = kx_ref.shape[1] if with_x else 0
    n_steps = (nx + nc) // tk
    n_tiles = q_ref.shape[1] // tq
    rows = 2 * tq

    k_all[nx:nx + nc, :] = kc_ref[0]
    v_all[0:dv, nx:nx + nc] = vc_ref[0].T.astype(BF16)
    v_all[dv:, :] = jnp.ones((v_all.shape[0] - dv, nx + nc), BF16)
    if with_x:
        chunk = min(512, nx)

        def fill(j, c):
            off = pl.multiple_of(j * chunk, chunk)
            k_all[pl.ds(off, chunk), :] = kx_ref[0, pl.ds(off, chunk), :]
            v_all[0:dv, pl.ds(off, chunk)] = vx_ref[0, pl.ds(off, chunk), :].T.astype(BF16)
            return c

        lax.fori_loop(0, nx // chunk, fill, 0)

    s_bufs, p_bufs, a_bufs = (s0_ref, s1_ref, s2_ref), (p0_ref, p1_ref, p2_ref), (a0_ref, a1_ref, a2_ref)
    lo = _lane_iota((tq, LANES)) < da_dim

    def buf(j):
        return 2 if (n_steps % 2 == 1 and n_steps > 1 and j == n_steps - 1) else j % 2

    def q_start(t):
        return t * tq if isinstance(t, int) else pl.multiple_of(t * tq, tq)

    def qk(t, j):
        q = q_ref[0, pl.ds(q_start(t), tq), :]
        zero = jnp.zeros_like(q)
        q2 = jnp.concatenate([jnp.where(lo, q, zero), jnp.where(lo, zero, q)], axis=0)
        s_bufs[buf(j)][...] = lax.dot_general(k_all[j * tk:(j + 1) * tk, :], q2,
                                              (((1,), (1,)), ((), ())), preferred_element_type=F32)

    def softmax(j):
        b = buf(j)
        s = s_bufs[b][...]
        col_max = jnp.max(s, axis=0, keepdims=True)
        if j == 0:
            m_new = col_max
        else:
            m_prev = m_ref[...]
            m_new = jnp.maximum(m_prev, col_max)
            a_bufs[b][...] = jnp.exp(m_prev - m_new)
        p_bufs[b][...] = jnp.exp(s - m_new).astype(BF16)
        m_ref[...] = m_new

    def pv(j):
        b = buf(j)
        r = jnp.dot(v_all[:, j * tk:(j + 1) * tk], p_bufs[b][...], preferred_element_type=F32)
        if j == 0:
            acc_ref[...] = r
        else:
            acc_ref[...] = a_bufs[b][...] * acc_ref[...] + r

    def finalize(t):
        acc = acc_ref[...]
        o0 = acc[0:dv, 0:tq] / acc[dv:dv + 1, 0:tq]
        o1 = acc[0:dv, tq:2 * tq] / acc[dv:dv + 1, tq:2 * tq]
        o = o0 - scal_ref[0] * o1
        y = o * lax.rsqrt(jnp.mean(o * o, axis=0, keepdims=True) + EPS) * gsub_ref[...]
        o_ref[0, pl.ds(q_start(t), tq), :] = (y * scal_ref[1]).T.astype(BF16)

    if n_steps == 1:
        def lone(t, c):
            qk(t, 0)
            softmax(0)
            pv(0)
            finalize(t)
            return c

        lax.fori_loop(0, n_tiles, lone, 0)
    else:
        qk(0, 0)
        last = buf(n_steps - 1)
        p_bufs[last][...] = jnp.zeros(p_bufs[last].shape, BF16)
        a_bufs[last][...] = jnp.ones(a_bufs[last].shape, F32)
        acc_ref[...] = jnp.ones(acc_ref.shape, F32)

        def tile(t, c):
            qk(t, 1)
            softmax(0)
            pv(n_steps - 1)
            finalize(jnp.maximum(t - 1, 0))
            for j in range(1, n_steps - 1):
                qk(t, j + 1)
                softmax(j)
                pv(j - 1)
            qk(jnp.minimum(t + 1, n_tiles - 1), 0)
            softmax(n_steps - 1)
            pv(n_steps - 2)
            return c

        lax.fori_loop(0, n_tiles, tile, 0)
        pv(n_steps - 1)
        finalize(n_tiles - 1)


def _attention(scal, q, kx, proj_x, kc, proj_c, g_sub, cols, *, with_x, da_dim):
    b, nq, d = q.shape
    dv = g_sub.shape[0]
    heads = d // dv
    nc = kc.shape[1]
    nx = kx.shape[1]
    tq = min(nq, 256)
    n_keys = (nx if with_x else 0) + nc
    tk = max(t for t in range(LANES, ATTN_MAX_KEY_TILE + 1, LANES) if n_keys % t == 0)
    assert nq % tq == 0 and 2 * da_dim == LANES and dv == LANES
    v_blk = cols["av"] // dv
    kern = functools.partial(_attn_kernel, tq=tq, tk=tk, with_x=with_x, da_dim=da_dim)
    rows = 2 * tq
    pipelined = 2 * nq * dv * 2 + nx * dv * (2 + 4) + nc * dv * (2 + 4)
    resident = (n_keys * 3 * dv * 2 + 3 * rows * tk * (4 + 2) + 4 * rows * LANES * 4
                + rows * 2 * dv * 4 + 3 * rows * tk * 4)
    return pl.pallas_call(
        kern,
        out_shape=jax.ShapeDtypeStruct((b, nq, d), BF16),
        grid=(b, heads),
        in_specs=[pl.BlockSpec(memory_space=pltpu.SMEM),
                  pl.BlockSpec((1, nq, dv), lambda bi, h: (bi, 0, h)),
                  pl.BlockSpec((1, nx, dv), lambda bi, h: (bi, 0, h)),
                  pl.BlockSpec((1, nx, dv), lambda bi, h: (bi, 0, v_blk + h)),
                  pl.BlockSpec((1, nc, dv), lambda bi, h: (bi, 0, h)),
                  pl.BlockSpec((1, nc, dv), lambda bi, h: (bi, 0, v_blk + h)),
                  pl.BlockSpec((dv, 1), lambda bi, h: (0, 0))],
        out_specs=pl.BlockSpec((1, nq, dv), lambda bi, h: (bi, 0, h)),
        scratch_shapes=[pltpu.VMEM((n_keys, dv), BF16), pltpu.VMEM((dv + ATTN_ONES_ROWS, n_keys), BF16)]
        + [pltpu.VMEM((tk, rows), F32)] * 3 + [pltpu.VMEM((tk, rows), BF16)] * 3
        + [pltpu.VMEM((1, rows), F32)] * 4 + [pltpu.VMEM((dv + ATTN_ONES_ROWS, rows), F32)],
        compiler_params=_params(("arbitrary", "arbitrary"), pipelined, resident),
        name="diff_attention" if with_x else "diff_attention_ctx",
    )(scal, q, kx, proj_x, kc, proj_c, g_sub.reshape(dv, 1))


CONV_HALO = 16
CONV_ROW_CHUNK = 64


def _conv_kernel(a_ref, g_ref, ap_ref, gp_ref, an_ref, gn_ref, w_ref, cb_ref, lg_ref, lb_ref,
                 o_ref, ext_ref, cv_ref, sh_ref):
    i = pl.program_id(1)
    last = pl.num_programs(1) - 1
    tm = a_ref.shape[1]
    ch = a_ref.shape[2]
    width = w_ref.shape[0]
    pad = width // 2

    def glu(a, g):
        return a * _sigmoid(g)

    ext_ref[0:CONV_HALO, :] = jnp.where(i > 0, glu(ap_ref[0], gp_ref[0]), 0.0)
    ext_ref[CONV_HALO:CONV_HALO + tm, :] = glu(a_ref[0], g_ref[0])
    ext_ref[CONV_HALO + tm:CONV_HALO + tm + CONV_HALO, :] = jnp.where(i < last, glu(an_ref[0], gn_ref[0]), 0.0)

    span = sh_ref.shape[1]
    for c in range(ch // LANES):
        sl = slice(c * LANES, (c + 1) * LANES)
        for s in range(SUBLANES):
            sh_ref[s] = ext_ref[s:s + span, sl]
        def rows(i, carry, sl=sl):
            r0 = pl.multiple_of(i * CONV_ROW_CHUNK, CONV_ROW_CHUNK)
            acc = jnp.zeros((CONV_ROW_CHUNK, LANES), F32)
            for j in range(width):
                start = CONV_HALO - pad + j
                base = start - start % SUBLANES
                acc = acc + w_ref[j:j + 1, sl] * sh_ref[start % SUBLANES, pl.ds(base + r0, CONV_ROW_CHUNK), :]
            cv_ref[pl.ds(r0, CONV_ROW_CHUNK), sl] = acc + cb_ref[:, sl]
            return carry

        lax.fori_loop(0, tm // CONV_ROW_CHUNK, rows, 0)

    v = cv_ref[...]
    mu = jnp.mean(v, axis=-1, keepdims=True)
    vc = v - mu
    var = jnp.mean(vc * vc, axis=-1, keepdims=True)
    y = vc * lax.rsqrt(var + EPS) * lg_ref[...] + lb_ref[...]
    o_ref[0] = (y * _sigmoid(y)).astype(BF16)


def _conformer_conv(proj, conv_w, conv_b, ln_g, ln_b, cols):
    b, n, _ = proj.shape
    width, ch = conv_w.shape
    tm = min(n, 256)
    assert n % tm == 0 and tm % CONV_HALO == 0 and width // 2 < CONV_HALO
    a_blk = cols["conv"] // ch
    g_blk = a_blk + 1
    r = tm // CONV_HALO
    n_halo = n // CONV_HALO

    def cur(blk):
        return pl.BlockSpec((1, tm, ch), lambda bi, i: (bi, i, blk))

    def prev(blk):
        return pl.BlockSpec((1, CONV_HALO, ch), lambda bi, i: (bi, jnp.maximum(i * r - 1, 0), blk))

    def nxt(blk):
        return pl.BlockSpec((1, CONV_HALO, ch), lambda bi, i: (bi, jnp.minimum((i + 1) * r, n_halo - 1), blk))

    vec = pl.BlockSpec((1, ch), lambda bi, i: (0, 0))
    return pl.pallas_call(
        _conv_kernel,
        out_shape=jax.ShapeDtypeStruct((b, n, ch), BF16),
        grid=(b, n // tm),
        in_specs=[cur(a_blk), cur(g_blk), prev(a_blk), prev(g_blk), nxt(a_blk), nxt(g_blk),
                  pl.BlockSpec((width, ch), lambda bi, i: (0, 0)), vec, vec, vec],
        out_specs=pl.BlockSpec((1, tm, ch), lambda bi, i: (bi, i, 0)),
        scratch_shapes=[pltpu.VMEM((tm + 2 * CONV_HALO, ch), F32), pltpu.VMEM((tm, ch), F32),
                        pltpu.VMEM((SUBLANES, tm + 2 * CONV_HALO - SUBLANES, LANES), F32)],
        compiler_params=_params(("arbitrary", "arbitrary"),
                                (2 * tm + 4 * CONV_HALO) * ch * 4 + tm * ch * 2 + 40 * ch * 4,
                                (2 * tm + 2 * CONV_HALO) * ch * 4 + 4 * tm * ch * 4
                                + SUBLANES * (tm + 2 * CONV_HALO) * LANES * 4),
        name="conformer_conv",
    )(proj, proj, proj, proj, proj, proj, conv_w, conv_b.reshape(1, ch), ln_g.reshape(1, ch),
      ln_b.reshape(1, ch))


def _ret_kernel(qf_ref, kf_ref, vf_ref, gf_ref, qb_ref, kb_ref, vb_ref, gb_ref, s0_ref,
                inner_ref, xi_ref, zeta_ref, gc_ref, gng_ref, gnb_ref,
                of_ref, ob_ref, sfin_ref, s_ref, *, kdim):
    i = pl.program_id(1)
    c = qf_ref.shape[1]
    dv = gng_ref.shape[1]
    n_pair = qf_ref.shape[2] // LANES

    @pl.when(i == 0)
    def _():
        s_ref[...] = s0_ref[0]

    lo = _lane_iota((c, LANES)) < kdim
    dirs = ((qf_ref, kf_ref, vf_ref, gf_ref, of_ref), (qb_ref, kb_ref, vb_ref, gb_ref, ob_ref))
    for d, (q_ref, k_ref, v_ref, g_ref, o_ref) in enumerate(dirs):
        for p in range(n_pair):
            sl = slice(p * LANES, (p + 1) * LANES)
            q = q_ref[0, :, sl]
            k = k_ref[0, :, sl]
            kz = k * zeta_ref[d, p]
            kb16 = k.astype(BF16)
            s_pair = s_ref[d, p]
            s16 = s_pair.astype(BF16)
            upd = jnp.zeros((LANES, dv), F32)
            for hh in range(2):
                h = 2 * p + hh
                hsl = slice(h * dv, (h + 1) * dv)
                keep = lo if hh == 0 else jnp.logical_not(lo)
                qh = jnp.where(keep, q, 0.0).astype(BF16)
                v16 = v_ref[0, :, hsl].astype(BF16)
                att = lax.dot_general(qh, kb16, (((1,), (1,)), ((), ())), preferred_element_type=F32)
                att = (att * inner_ref[d, h]).astype(BF16)
                o = (jnp.dot(att, v16, preferred_element_type=F32)
                     + jnp.dot(qh, s16, preferred_element_type=F32) * xi_ref[d, h])
                kzh = jnp.where(keep, kz, 0.0).astype(BF16)
                upd = upd + lax.dot_general(kzh, v16, (((0,), (0,)), ((), ())), preferred_element_type=F32)
                mu = jnp.mean(o, axis=-1, keepdims=True)
                oc = o - mu
                var = jnp.mean(oc * oc, axis=-1, keepdims=True)
                y = oc * lax.rsqrt(var + EPS) * gng_ref[...] + gnb_ref[...]
                g = g_ref[0, :, hsl]
                o_ref[0, :, hsl] = (g * _sigmoid(g)) * y
            s_ref[d, p] = s_pair * gc_ref[p] + upd

    sfin_ref[0] = s_ref[...]


def _retention(rq, rk, proj, s0, tabs, gn_g, gn_b, cols):
    b, n, rw = rq.shape
    dv = gn_g.shape[0]
    d = cols["d"]
    c = RET_CHUNK
    nch = n // c
    kdim = dv // 2
    assert n % c == 0 and c == LANES and 2 * kdim == LANES
    n_pair = rw // LANES
    inner, xi, zeta, gc = tabs
    v_blk, gf_blk, gb_blk = cols["rv"] // d, cols["rgf"] // d, cols["rgb"] // d

    def fwd(width, blk):
        return pl.BlockSpec((1, c, width), lambda bi, i: (bi, i, blk))

    def bwd(width, blk):
        return pl.BlockSpec((1, c, width), lambda bi, i: (bi, nch - 1 - i, blk))

    def whole(a):
        return pl.BlockSpec(a.shape, lambda bi, i: (0,) * a.ndim)

    state = pl.BlockSpec((1, 2, n_pair, LANES, dv), lambda bi, i: (bi, 0, 0, 0, 0))
    vec = pl.BlockSpec((1, dv), lambda bi, i: (0, 0))
    kern = functools.partial(_ret_kernel, kdim=kdim)
    pipelined = 2 * c * (2 * rw + 2 * d) * 4 + 2 * c * d * 4 + 2 * 2 * n_pair * LANES * dv * 4
    resident = (inner.size + xi.size + zeta.size + gc.size) * 4 * 2 + 2 * n_pair * LANES * dv * 4 + 8 * MIB
    return pl.pallas_call(
        kern,
        out_shape=(jax.ShapeDtypeStruct((b, n, d), F32), jax.ShapeDtypeStruct((b, n, d), F32),
                   jax.ShapeDtypeStruct((b, 2, n_pair, LANES, dv), F32)),
        grid=(b, nch),
        in_specs=[fwd(rw, 0), fwd(rw, 0), fwd(d, v_blk), fwd(d, gf_blk),
                  bwd(rw, 0), bwd(rw, 0), bwd(d, v_blk), bwd(d, gb_blk),
                  state, whole(inner), whole(xi), whole(zeta), whole(gc), vec, vec],
        out_specs=(pl.BlockSpec((1, c, d), lambda bi, i: (bi, i, 0)),
                   pl.BlockSpec((1, c, d), lambda bi, i: (bi, nch - 1 - i, 0)),
                   state),
        scratch_shapes=[pltpu.VMEM((2, n_pair, LANES, dv), F32)],
        compiler_params=_params(("arbitrary", "arbitrary"), pipelined, resident),
        name="retention",
    )(rq, rk, proj, proj, rq, rk, proj, proj, s0, inner, xi, zeta, gc,
      gn_g.reshape(1, dv), gn_b.reshape(1, dv))


def _retention_tables(heads, kdim, dv):
    c = RET_CHUNK
    lg = jnp.log1p(-jnp.exp2(-5.0 - jnp.arange(heads, dtype=F32)))
    pos = jnp.arange(c, dtype=F32)
    diff = pos[:, None] - pos[None, :]
    inner_f = jnp.where(diff[None] >= 0, jnp.exp(jnp.maximum(diff, 0.0)[None] * lg[:, None, None]), 0.0)
    inner = jnp.stack([inner_f, jnp.swapaxes(inner_f, 1, 2)])
    xi_f = jnp.exp((pos + 1.0)[None, :] * lg[:, None])
    zeta_f = jnp.exp((c - 1.0 - pos)[None, :] * lg[:, None])
    xi = jnp.stack([xi_f, xi_f[:, ::-1]])
    zeta = jnp.stack([zeta_f, zeta_f[:, ::-1]])
    xi = jnp.broadcast_to(xi[..., None], (2, heads, c, dv))
    zeta = jnp.broadcast_to(zeta[..., None], (2, heads, c, kdim))
    zeta = zeta.reshape(2, heads // 2, 2, c, kdim).transpose(0, 1, 3, 2, 4).reshape(2, heads // 2, c, 2 * kdim)
    gc = jnp.exp(c * lg)
    gc = jnp.broadcast_to(gc[:, None, None], (heads, kdim, dv)).reshape(heads // 2, 2 * kdim, dv)
    return inner.astype(F32), xi.astype(F32), zeta.astype(F32), gc.astype(F32)


def _merge_kernel(h_ref, mod_ref, oa_ref, ob_ref, of_ref, obk_ref, gate_ref, bg_ref,
                  wa_ref, wb_ref, wc_ref, wo_ref, o_ref):
    d = h_ref.shape[2]
    g = _sigmoid(gate_ref[0] + bg_ref[...])
    oc = (of_ref[0] + obk_ref[0]).astype(BF16)
    y = g[:, 0:d] * jnp.dot(oa_ref[0], wa_ref[...], preferred_element_type=F32)
    y = y + g[:, d:2 * d] * jnp.dot(ob_ref[0], wb_ref[...], preferred_element_type=F32)
    y = y + g[:, 2 * d:3 * d] * jnp.dot(oc, wc_ref[...], preferred_element_type=F32)
    m = jnp.dot(y.astype(BF16), wo_ref[...], preferred_element_type=F32)
    o_ref[0] = h_ref[0] + mod_ref[0, 2:3, :] * m


def _merge(h, mod, oa, ob, of, obk, proj, b_gate, w_pa, w_pb, w_pc, w_out, layer, cols):
    b, n, d = h.shape
    tm = min(n, 512)
    assert n % tm == 0
    mb = mod.shape[0]
    gate_blk = cols["gate"] // (3 * d)
    row = lambda width: pl.BlockSpec((1, tm, width), lambda bi, i: (bi, i, 0))
    wspec = pl.BlockSpec((None, d, d), lambda bi, i: (layer, 0, 0))
    return pl.pallas_call(
        _merge_kernel,
        out_shape=jax.ShapeDtypeStruct((b, n, d), F32),
        grid=(b, n // tm),
        in_specs=[row(d), pl.BlockSpec((1, N_MOD, d), lambda bi, i: (bi % mb, 0, 0)),
                  row(d), row(d), row(d), row(d),
                  pl.BlockSpec((1, tm, 3 * d), lambda bi, i: (bi, i, gate_blk)),
                  pl.BlockSpec((1, 3 * d), lambda bi, i: (0, 0)),
                  wspec, wspec, wspec, wspec],
        out_specs=row(d),
        compiler_params=_params(("arbitrary", "arbitrary"),
                                tm * d * (4 + 2 + 2 + 4 + 4 + 12 + 4) + 4 * d * d * 2, 10 * tm * d * 4),
        name="merge",
    )(h, mod, oa, ob, of, obk, proj, b_gate.reshape(1, 3 * d), w_pa, w_pb, w_pc, w_out)


ROUTE_E, ROUTE_G, ROUTE_R = 0, 4, 8


def _route_kernel(h_ref, mod_ref, g_ref, wr_ref, br_ref, tri_ref, base_ref,
                  hx_ref, route_ref, cnt_ref, run_ref, *, n_exp):
    first = jnp.logical_and(pl.program_id(0) == 0, pl.program_id(1) == 0)

    @pl.when(first)
    def _():
        run_ref[...] = base_ref[...]

    h = h_ref[0]
    tm = h.shape[0]
    y = h * lax.rsqrt(jnp.mean(h * h, axis=-1, keepdims=True) + EPS) * g_ref[...]
    hx = y * (1.0 + mod_ref[0, 4:5, :]) + mod_ref[0, 3:4, :]
    hx_ref[0] = hx

    lane = _lane_iota((tm, LANES))
    lane_f = lane.astype(F32)
    logits = jnp.dot(hx, wr_ref[...], preferred_element_type=F32,
                     precision=lax.Precision.HIGHEST) + br_ref[...]
    logits = jnp.where(lane < n_exp, logits, NEG_INF)

    sels, vals = [], []
    for _ in range(TOP_K):
        mx = jnp.max(logits, axis=-1, keepdims=True)
        idx = jnp.min(jnp.where(logits == mx, lane_f, float(LANES)), axis=-1, keepdims=True)
        sel = lane_f == idx
        sels.append((sel, idx))
        vals.append(mx)
        logits = jnp.where(sel, NEG_INF, logits)

    es = [jnp.exp(v - vals[0]) for v in vals]
    denom = es[0]
    for e in es[1:]:
        denom = denom + e

    onehot = jnp.zeros((tm, LANES), F32)
    for sel, _ in sels:
        onehot = onehot + jnp.where(sel, 1.0, 0.0)
    before = run_ref[...] + jnp.dot(tri_ref[...], onehot.astype(BF16), preferred_element_type=F32)

    route = jnp.zeros((tm, LANES), F32)
    for k, (sel, idx) in enumerate(sels):
        rank = jnp.sum(jnp.where(sel, before, 0.0), axis=-1, keepdims=True)
        route = jnp.where(lane == ROUTE_E + k, idx, route)
        route = jnp.where(lane == ROUTE_G + k, es[k] / denom, route)
        route = jnp.where(lane == ROUTE_R + k, rank, route)
    route_ref[0] = route

    run_ref[...] = run_ref[...] + jnp.sum(onehot, axis=0, keepdims=True)
    cnt_ref[...] = run_ref[...]


def _route(h, mod, g, w_router, b_router, base):
    b, n, d = h.shape
    n_exp = w_router.shape[-1]
    tm = min(n, 256)
    assert n % tm == 0 and n_exp <= LANES
    mb = mod.shape[0]
    wr = jnp.zeros((d, LANES), F32).at[:, :n_exp].set(w_router)
    br = jnp.zeros((1, LANES), F32).at[0, :n_exp].set(b_router)
    tri = (jnp.arange(tm)[:, None] > jnp.arange(tm)[None, :]).astype(BF16)
    row = lambda width: pl.BlockSpec((1, tm, width), lambda bi, i: (bi, i, 0))
    const = lambda shape: pl.BlockSpec(shape, lambda bi, i: (0,) * len(shape))
    kern = functools.partial(_route_kernel, n_exp=n_exp)
    return pl.pallas_call(
        kern,
        out_shape=(jax.ShapeDtypeStruct((b, n, d), F32), jax.ShapeDtypeStruct((b, n, LANES), F32),
                   jax.ShapeDtypeStruct((1, LANES), F32)),
        grid=(b, n // tm),
        in_specs=[row(d), pl.BlockSpec((1, N_MOD, d), lambda bi, i: (bi % mb, 0, 0)), const((1, d)),
                  const((d, LANES)), const((1, LANES)), const((tm, tm)), const((1, LANES))],
        out_specs=(row(d), row(LANES), const((1, LANES))),
        scratch_shapes=[pltpu.VMEM((1, LANES), F32)],
        compiler_params=_params(("arbitrary", "arbitrary"),
                                2 * tm * d * 4 + tm * LANES * 4 + d * LANES * 4 + tm * tm * 2, 8 * tm * d * 4),
        name="moe_route",
    )(h, mod, g.reshape(1, d), wr, br, tri, base)


DMA_ISSUE_UNROLL = 8


def _tile_dest(dest, tm):
    t = dest.shape[0]
    return dest.reshape(t // tm, tm, TOP_K).transpose(0, 2, 1).reshape(t // tm, 1, TOP_K * tm)


def _dispatch_kernel(dest_ref, hx_ref, xs_in_ref, xs_ref, sem):
    del xs_in_ref
    tm = hx_ref.shape[0]

    def copy(k, t):
        return pltpu.make_async_copy(hx_ref.at[pl.ds(t, 1)],
                                     xs_ref.at[pl.ds(dest_ref[0, 0, k * tm + t], 1)], sem)

    for t in range(tm):
        for k in range(TOP_K):
            copy(k, t).start(priority=(t * TOP_K + k) % 2)
    for k in range(TOP_K):
        def wait(t, c, k=k):
            copy(k, t).wait()
            return c

        lax.fori_loop(0, tm, wait, 0, unroll=DMA_ISSUE_UNROLL)


def _dispatch(hx_flat, dest, xs):
    t, d = hx_flat.shape
    tm = min(t, 256)
    assert t % tm == 0
    dest3 = _tile_dest(dest, tm)
    return pl.pallas_call(
        _dispatch_kernel,
        out_shape=jax.ShapeDtypeStruct(xs.shape, xs.dtype),
        grid=(t // tm,),
        in_specs=[pl.BlockSpec((1, 1, tm * TOP_K), lambda i: (i, 0, 0), memory_space=pltpu.SMEM),
                  pl.BlockSpec((tm, d), lambda i: (i, 0)),
                  pl.BlockSpec(memory_space=pl.ANY)],
        out_specs=pl.BlockSpec(memory_space=pl.ANY),
        scratch_shapes=[pltpu.SemaphoreType.DMA],
        input_output_aliases={2: 0},
        compiler_params=pltpu.CompilerParams(dimension_semantics=("arbitrary",),
                                             vmem_limit_bytes=_vmem_limit(tm * d * 4, 4 * MIB),
                                             has_side_effects=True),
        name="moe_dispatch",
    )(dest3, hx_flat, xs)


def _split_kernel(w_ref, sel_ref, g_ref, u_ref):
    blk = sel_ref.shape[0]
    half = blk // 2
    for i in range(w_ref.shape[1] // blk):
        r = jnp.dot(w_ref[:, i * blk:(i + 1) * blk].astype(BF16), sel_ref[...], preferred_element_type=F32)
        g_ref[:, i * half:(i + 1) * half] = r[:, :half].astype(BF16)
        u_ref[:, i * half:(i + 1) * half] = r[:, half:].astype(BF16)


def _split_gate_up(w_gate_up):
    depth, n_exp, d, de2 = w_gate_up.shape
    de = de2 // 2
    tr = min(d, 512)
    assert d % tr == 0
    blk = 2 * LANES
    assert de2 % blk == 0
    order = jnp.concatenate([jnp.arange(0, blk, 2), jnp.arange(1, blk, 2)])
    sel = (jnp.arange(blk)[:, None] == order[None, :]).astype(BF16)
    out = jax.ShapeDtypeStruct((depth, n_exp, d, de), BF16)
    ospec = pl.BlockSpec((None, None, tr, de), lambda l, e, r: (l, e, r, 0))
    return pl.pallas_call(
        _split_kernel,
        out_shape=(out, out),
        grid=(depth, n_exp, d // tr),
        in_specs=[pl.BlockSpec((None, None, tr, de2), lambda l, e, r: (l, e, r, 0)),
                  pl.BlockSpec((blk, blk), lambda l, e, r: (0, 0))],
        out_specs=(ospec, ospec),
        compiler_params=_params(("arbitrary", "arbitrary", "arbitrary"),
                                tr * de2 * 4 + blk * blk * 2 + 2 * tr * de * 2, tr * de2 * (2 + 4 + 4)),
        name="split_gate_up",
    )(w_gate_up, sel)


def _expert_kernel(blk_e_ref, n_used_ref, x_ref, wg_ref, wu_ref, wd_ref, bg_ref, bu_ref, bd_ref, y_ref):
    del blk_e_ref
    used = pl.program_id(0) < n_used_ref[0]

    @pl.when(used)
    def _():
        x = x_ref[...].astype(BF16)
        gl = jnp.dot(x, wg_ref[...], preferred_element_type=F32) + bg_ref[...]
        up = jnp.dot(x, wu_ref[...], preferred_element_type=F32) + bu_ref[...]
        gl = jnp.minimum(gl, SWIGLU_LIMIT)
        up = jnp.clip(up, -SWIGLU_LIMIT, SWIGLU_LIMIT)
        act = (up + 1.0) * (gl * _sigmoid(SWIGLU_ALPHA * gl))
        y_ref[...] = jnp.dot(act.astype(BF16), wd_ref[...], preferred_element_type=F32) + bd_ref[...]

    @pl.when(jnp.logical_not(used))
    def _():
        y_ref[...] = jnp.zeros(y_ref.shape, F32)


def _experts(blk_e, n_used, xs, wg, wu, wd, bg, bu, bd, layer, bm):
    n_rows, d = xs.shape
    de = wg.shape[-1]
    nb = n_rows // bm
    wspec = lambda a, c: pl.BlockSpec((None, None, a, c), lambda i, be, nu: (layer, be[i], 0, 0))
    return pl.pallas_call(
        _expert_kernel,
        out_shape=jax.ShapeDtypeStruct((n_rows, d), F32),
        grid_spec=pltpu.PrefetchScalarGridSpec(
            num_scalar_prefetch=2,
            grid=(nb,),
            in_specs=[pl.BlockSpec((bm, d), lambda i, be, nu: (i, 0)),
                      wspec(d, de), wspec(d, de), wspec(de, d),
                      wspec(1, de), wspec(1, de), wspec(1, d)],
            out_specs=pl.BlockSpec((bm, d), lambda i, be, nu: (i, 0))),
        compiler_params=_params(("arbitrary",), 2 * bm * d * 4 + 3 * d * de * 2, 8 * bm * de * 4),
        name="moe_experts",
    )(blk_e, n_used, xs, wg, wu, wd, bg, bu, bd)


def _combine_kernel(dest_ref, h_ref, mod_ref, route_ref, y_ref, o_ref, ybuf, sem):
    tm = h_ref.shape[1]

    def copy(r):
        return pltpu.make_async_copy(y_ref.at[pl.ds(dest_ref[0, 0, r], 1)], ybuf.at[pl.ds(r, 1)], sem)

    for r in range(TOP_K * tm):
        copy(r).start(priority=r % 2)

    def wait(r, c):
        copy(r).wait()
        return c

    lax.fori_loop(0, TOP_K * tm, wait, 0, unroll=DMA_ISSUE_UNROLL)

    route = route_ref[0]
    f = jnp.zeros((tm, h_ref.shape[2]), F32)
    for k in range(TOP_K):
        f = f + route[:, ROUTE_G + k:ROUTE_G + k + 1] * ybuf[k * tm:(k + 1) * tm, :]
    o_ref[0] = h_ref[0] + mod_ref[0, 5:6, :] * f


def _combine(h, mod, route, dest, y):
    b, n, d = h.shape
    tm = min(n, 256)
    assert n % tm == 0
    mb = mod.shape[0]
    nt = n // tm
    dest3 = _tile_dest(dest, tm)
    row = lambda width: pl.BlockSpec((1, tm, width), lambda bi, i: (bi, i, 0))
    return pl.pallas_call(
        _combine_kernel,
        out_shape=jax.ShapeDtypeStruct((b, n, d), F32),
        grid=(b, nt),
        in_specs=[pl.BlockSpec((1, 1, tm * TOP_K), lambda bi, i: (bi * nt + i, 0, 0), memory_space=pltpu.SMEM),
                  row(d), pl.BlockSpec((1, N_MOD, d), lambda bi, i: (bi % mb, 0, 0)), row(LANES),
                  pl.BlockSpec(memory_space=pl.ANY)],
        out_specs=row(d),
        scratch_shapes=[pltpu.VMEM((TOP_K * tm, d), F32), pltpu.SemaphoreType.DMA],
        compiler_params=_params(("arbitrary", "arbitrary"), 2 * tm * d * 4 + tm * LANES * 4,
                                TOP_K * tm * d * 4 + 4 * tm * d * 4),
        name="moe_combine",
    )(dest3, h, mod, route, y)


def _moe_rows(n_tok, n_exp, bm):
    return -(-(n_tok * TOP_K + n_exp * (bm - 1)) // bm) * bm


def _moe(parts, xs, g, w_router, b_router, wg, wu, wd, bg, bu, bd, layer, bm):
    n_exp = w_router.shape[-1]
    n_rows, d = xs.shape
    base = jnp.zeros((1, LANES), F32)
    hxs, routes = [], []
    for h, mod in parts:
        hx, route, base = _route(h, mod, g, w_router, b_router, base)
        hxs.append(hx)
        routes.append(route)
    counts = base[0, :n_exp].astype(jnp.int32)

    padded = (counts + bm - 1) // bm * bm
    pend = jnp.cumsum(padded)
    pstart = pend - padded
    n_tok = sum(h.shape[0] * h.shape[1] for h, _ in parts)
    assert n_rows >= _moe_rows(n_tok, n_exp, bm) and n_rows % bm == 0
    nb = n_rows // bm
    blk_start = jnp.arange(nb, dtype=jnp.int32) * bm
    blk_e = jnp.minimum(jnp.sum((pend[None, :] <= blk_start[:, None]).astype(jnp.int32), axis=1), n_exp - 1)
    n_used = (pend[-1] // bm).astype(jnp.int32).reshape(1)
    eids = jnp.arange(n_exp, dtype=jnp.int32)

    dests = []
    for hx, route in zip(hxs, routes):
        e = route[..., ROUTE_E:ROUTE_E + TOP_K].astype(jnp.int32)
        rank = route[..., ROUTE_R:ROUTE_R + TOP_K].astype(jnp.int32)
        start = jnp.sum(jnp.where(e[..., None] == eids, pstart, 0), axis=-1)
        dest = (start + rank).reshape(-1, TOP_K)
        dests.append(dest)
        xs = _dispatch(hx.reshape(-1, d), dest, xs)

    y = _experts(blk_e, n_used, xs, wg, wu, wd, bg, bu, bd, layer, bm)
    return [_combine(h, mod, route, dest, y)
            for (h, mod), route, dest in zip(parts, routes, dests)], xs


def kernel(x, c, ctx, c_ctx, w_mod, b_mod, g_norm1, g_norm2, w_in, b_gate, g_qa, g_ka, lam_q1, lam_k1, lam_q2, lam_k2, g_sub, conv_w, conv_b, ln_g, ln_b, gn_g, gn_b, w_pa, w_pb, w_pc, w_out, w_router, b_router, w_gate_up, b_gate_up, w_down, b_down):
    bsz, n_x, d = x.shape
    n_c = ctx.shape[1]
    depth = w_mod.shape[0]
    da_dim = g_qa.shape[-1]
    dv = g_sub.shape[-1]
    heads = d // dv
    ret_kdim = gn_g.shape[-1] // 2
    n_exp = w_router.shape[-1]
    de = w_down.shape[-2]
    moe_bm = 512

    cols = {"d": d, "ret_qk": heads * ret_kdim}
    off = 0
    for name, width in (("aq", d), ("ak", d), ("av", d), ("conv", 2 * d), ("rq", heads * ret_kdim),
                        ("rk", heads * ret_kdim), ("rv", d), ("rgf", d), ("rgb", d), ("gate", 3 * d)):
        cols[name] = off
        off += width
    assert off == w_in.shape[-1]

    w_in_bf = w_in.astype(BF16)
    w_pa_bf, w_pb_bf, w_pc_bf, w_out_bf = (w.astype(BF16) for w in (w_pa, w_pb, w_pc, w_out))
    wg, wu = _split_gate_up(w_gate_up)
    wd = w_down.astype(BF16)
    bg = b_gate_up[..., 0::2].reshape(depth, n_exp, 1, de)
    bu = b_gate_up[..., 1::2].reshape(depth, n_exp, 1, de)
    bd = b_down.reshape(depth, n_exp, 1, d)

    rows = -(-(bsz + 1) // SUBLANES) * SUBLANES
    cvecs = jnp.zeros((rows, d), F32).at[:bsz].set(c).at[bsz].set(c_ctx)
    mods = _modulation(cvecs, w_mod, b_mod).reshape(depth, rows, N_MOD, d)

    tabs_x = _rope_tables(n_x, da_dim, ret_kdim)
    tabs_c = tuple(jnp.zeros((n_c, LANES), F32) for _ in range(4))
    ret_tabs = _retention_tables(heads, ret_kdim, dv)
    s_zero = jnp.zeros((bsz, 2, heads // 2, LANES, dv), F32)
    xs_rows = jnp.zeros((_moe_rows(bsz * (n_c + n_x), n_exp, moe_bm), d), F32)

    h_ctx = ctx
    for l in range(depth):
        need_ctx = l < depth - 1
        lam_init = 0.8 - 0.6 * math.exp(-0.3 * l)
        lam = (jnp.exp(jnp.sum(lam_q1[l] * lam_k1[l])) - jnp.exp(jnp.sum(lam_q2[l] * lam_k2[l])) + lam_init)
        scal = jnp.stack([lam, jnp.asarray(1.0 - lam_init, F32)]).astype(F32)
        mod_x = mods[l, :bsz]
        mod_c = mods[l, bsz:bsz + 1]

        proj_x = _in_proj(x, mod_x, g_norm1[l], w_in_bf, l)
        proj_c = _in_proj(h_ctx, mod_c, g_norm1[l], w_in_bf, l)
        aq_x, ak_x, rq_x, rk_x = _qk_prep(proj_x, g_qa[l], g_ka[l], tabs_x, cols, rope=True,
                                          da_dim=da_dim, ret_kdim=ret_kdim)
        aq_c, ak_c, rq_c, rk_c = _qk_prep(proj_c, g_qa[l], g_ka[l], tabs_c, cols, rope=False,
                                          da_dim=da_dim, ret_kdim=ret_kdim)
        oa_x = _attention(scal, aq_x, ak_x, proj_x, ak_c, proj_c, g_sub[l], cols, with_x=True, da_dim=da_dim)
        ob_x = _conformer_conv(proj_x, conv_w[l], conv_b[l], ln_g[l], ln_b[l], cols)
        of_c, ob_c, s_ctx = _retention(rq_c, rk_c, proj_c, s_zero, ret_tabs, gn_g[l], gn_b[l], cols)
        of_x, obk_x, _ = _retention(rq_x, rk_x, proj_x, s_ctx, ret_tabs, gn_g[l], gn_b[l], cols)
        x = _merge(x, mod_x, oa_x, ob_x, of_x, obk_x, proj_x, b_gate[l],
                   w_pa_bf, w_pb_bf, w_pc_bf, w_out_bf, l, cols)
        if need_ctx:
            oa_c = _attention(scal, aq_c, ak_c, proj_c, ak_c, proj_c, g_sub[l], cols, with_x=False, da_dim=da_dim)
            cb_c = _conformer_conv(proj_c, conv_w[l], conv_b[l], ln_g[l], ln_b[l], cols)
            h_ctx = _merge(h_ctx, mod_c, oa_c, cb_c, of_c, ob_c, proj_c, b_gate[l],
                           w_pa_bf, w_pb_bf, w_pc_bf, w_out_bf, l, cols)

        parts = [(h_ctx, mod_c), (x, mod_x)] if need_ctx else [(x, mod_x)]
        outs, xs_rows = _moe(parts, xs_rows, g_norm2[l], w_router[l], b_router[l], wg, wu, wd, bg, bu, bd,
                             l, moe_bm)
        if need_ctx:
            h_ctx, x = outs
        else:
            (x,) = outs
    return x
```

```python
import functools
import math

import jax
import jax.numpy as jnp
from jax import lax
from jax.experimental import pallas as pl
from jax.experimental.pallas import tpu as pltpu

F32 = jnp.float32
BF16 = jnp.bfloat16

GRID_W = 64
N_MOD = 6
EPS = 1e-6
ROPE_BASE = 10000.0
RET_CHUNK = 128
TOP_K = 4
SWIGLU_ALPHA = 1.702
SWIGLU_LIMIT = 7.0

LANES = 128
SUBLANES = 8
V7X_VMEM_BYTES = 64 * 2**20
VMEM_LIMIT_CAP = 56 * 2**20
MIB = 2**20

NEG_INF = float("-inf")


def _sigmoid(x):
    return 1.0 / (1.0 + jnp.exp(-x))


def _vmem_limit(pipelined_bytes, resident_bytes):
    need = 2 * pipelined_bytes + resident_bytes
    return int(min(max(need, 16 * MIB), VMEM_LIMIT_CAP))


def _params(sem, pipelined_bytes, resident_bytes):
    return pltpu.CompilerParams(dimension_semantics=sem,
                                vmem_limit_bytes=_vmem_limit(pipelined_bytes, resident_bytes))


def _lane_iota(shape):
    return lax.broadcasted_iota(jnp.int32, shape, len(shape) - 1)


def _mod_kernel(c_ref, w_ref, b_ref, o_ref):
    c = c_ref[...]
    a = (c * _sigmoid(c)).astype(BF16)
    o_ref[...] = jnp.dot(a, w_ref[...].astype(BF16), preferred_element_type=F32) + b_ref[...]


def _modulation(cvecs, w_mod, b_mod):
    depth, d, cols = w_mod.shape
    rows = cvecs.shape[0]
    tn = 1536
    assert cols % tn == 0
    return pl.pallas_call(
        _mod_kernel,
        out_shape=jax.ShapeDtypeStruct((depth, rows, cols), F32),
        grid=(depth, cols // tn),
        in_specs=[pl.BlockSpec((rows, d), lambda l, j: (0, 0)),
                  pl.BlockSpec((None, d, tn), lambda l, j: (l, 0, j)),
                  pl.BlockSpec((None, 1, tn), lambda l, j: (l, 0, j))],
        out_specs=pl.BlockSpec((None, rows, tn), lambda l, j: (l, 0, j)),
        compiler_params=_params(("arbitrary", "arbitrary"), d * tn * 4 + rows * tn * 4, 4 * MIB),
        name="modulation",
    )(cvecs, w_mod, b_mod.reshape(depth, 1, cols))


def _inproj_kernel(h_ref, mod_ref, g_ref, w_ref, gq_ref, gk_ref, ca_ref, sa_ref, cr_ref, sr_ref,
                   o_ref, oq_ref, ok_ref, orq_ref, ork_ref, xs_ref,
                   *, rope, da_dim, ret_kdim, attn_at, ret_at):
    j = pl.program_id(2)
    tm = h_ref.shape[1]

    @pl.when(j == 0)
    def _():
        h = h_ref[0]
        y = h * lax.rsqrt(jnp.mean(h * h, axis=-1, keepdims=True) + EPS) * g_ref[...]
        xs_ref[...] = (y * (1.0 + mod_ref[0, 1:2, :]) + mod_ref[0, 0:1, :]).astype(BF16)

    def project():
        acc = jnp.dot(xs_ref[...], w_ref[...], preferred_element_type=F32)
        o_ref[0] = acc
        return acc

    lane = _lane_iota((tm, LANES))
    lo = lane < da_dim

    def rms(x, gain):
        x2 = x * x
        s_lo = jnp.sum(jnp.where(lo, x2, 0.0), axis=-1, keepdims=True)
        s_hi = jnp.sum(jnp.where(lo, 0.0, x2), axis=-1, keepdims=True)
        ms = jnp.where(lo, s_lo, s_hi) * (1.0 / da_dim)
        return x * lax.rsqrt(ms + EPS) * gain

    def rot(x, c, s, half):
        first = (lane % (2 * half)) < half
        partner = jnp.where(first, pltpu.roll(x, LANES - half, 1), pltpu.roll(x, half, 1))
        return x * c + partner * s

    (attn_j, q_off, k_off), (ret_j, rq_off, rk_off) = attn_at, ret_at

    assert attn_j != ret_j

    @pl.when(jnp.logical_and(j != attn_j, j != ret_j))
    def _():
        project()

    @pl.when(j == attn_j)
    def _():
        acc = project()
        for s in range(oq_ref.shape[2] // LANES):
            q = rms(acc[:, q_off + s * LANES:q_off + (s + 1) * LANES], gq_ref[...])
            k = rms(acc[:, k_off + s * LANES:k_off + (s + 1) * LANES], gk_ref[...])
            if rope:
                q = rot(q, ca_ref[...], sa_ref[...], da_dim // 4)
                k = rot(k, ca_ref[...], sa_ref[...], da_dim // 4)
            oq_ref[0, :, s * LANES:(s + 1) * LANES] = (q * da_dim ** -0.5).astype(BF16)
            ok_ref[0, :, s * LANES:(s + 1) * LANES] = k.astype(BF16)

    @pl.when(j == ret_j)
    def _():
        acc = project()
        for s in range(orq_ref.shape[2] // LANES):
            q = acc[:, rq_off + s * LANES:rq_off + (s + 1) * LANES]
            k = acc[:, rk_off + s * LANES:rk_off + (s + 1) * LANES]
            if rope:
                q = rot(q, cr_ref[...], sr_ref[...], ret_kdim // 2)
                k = rot(k, cr_ref[...], sr_ref[...], ret_kdim // 2)
            orq_ref[0, :, s * LANES:(s + 1) * LANES] = q
            ork_ref[0, :, s * LANES:(s + 1) * LANES] = k * ret_kdim ** -0.5


def _in_proj(h, mod, g, w_in_bf, layer, g_q, g_k, tabs, cols, *, rope, da_dim, ret_kdim):
    b, n, d = h.shape
    width = w_in_bf.shape[-1]
    rw = cols["ret_qk"]
    tm = min(n, 1024)
    tn = 2048
    assert n % tm == 0 and width % tn == 0
    mb = mod.shape[0]

    def place(a, a_width, c, c_width):
        ja, oa = divmod(cols[a], tn)
        jc, oc = divmod(cols[c], tn)
        assert ja == jc and oa + a_width <= tn and oc + c_width <= tn
        return ja, oa, oc

    gq = jnp.tile(g_q, LANES // da_dim).reshape(1, LANES)
    gk = jnp.tile(g_k, LANES // da_dim).reshape(1, LANES)
    kern = functools.partial(_inproj_kernel, rope=rope, da_dim=da_dim, ret_kdim=ret_kdim,
                             attn_at=place("aq", d, "ak", d), ret_at=place("rq", rw, "rk", rw))
    vec = pl.BlockSpec((1, LANES), lambda bi, i, j: (0, 0))
    tab = pl.BlockSpec((tm, LANES), lambda bi, i, j: (i, 0))
    side = lambda w: pl.BlockSpec((1, tm, w), lambda bi, i, j: (bi, i, 0), pipeline_mode=pl.Buffered(1))
    return pl.pallas_call(
        kern,
        out_shape=(jax.ShapeDtypeStruct((b, n, width), F32),
                   jax.ShapeDtypeStruct((b, n, d), BF16), jax.ShapeDtypeStruct((b, n, d), BF16),
                   jax.ShapeDtypeStruct((b, n, rw), F32), jax.ShapeDtypeStruct((b, n, rw), F32)),
        grid=(b, n // tm, width // tn),
        in_specs=[pl.BlockSpec((1, tm, d), lambda bi, i, j: (bi, i, 0)),
                  pl.BlockSpec((1, N_MOD, d), lambda bi, i, j: (bi % mb, 0, 0)),
                  pl.BlockSpec((1, d), lambda bi, i, j: (0, 0)),
                  pl.BlockSpec((None, d, tn), lambda bi, i, j: (layer, 0, j)),
                  vec, vec, tab, tab, tab, tab],
        out_specs=(pl.BlockSpec((1, tm, tn), lambda bi, i, j: (bi, i, j)),
                   side(d), side(d), side(rw), side(rw)),
        scratch_shapes=[pltpu.VMEM((tm, d), BF16)],
        compiler_params=_params(("arbitrary", "arbitrary", "arbitrary"),
                                tm * d * 4 + d * tn * 2 + tm * tn * 4 + 4 * tm * LANES * 4,
                                tm * d * 2 + tm * (2 * d * 2 + 2 * rw * 4) + 3 * tm * d * 4),
        name="in_proj",
    )(h, mod, g.reshape(1, d), w_in_bf, gq, gk, *tabs)


def _rope_tables(n_tok, da_dim, ret_kdim):
    t = jnp.arange(n_tok)

    def angles(pos, dim):
        inv = ROPE_BASE ** (-jnp.arange(0, dim, 2, dtype=F32) / dim)
        return pos.astype(F32)[:, None] * inv[None, :]

    def cs(ang):
        c, s = jnp.cos(ang), jnp.sin(ang)
        return jnp.concatenate([c, c], axis=-1), jnp.concatenate([-s, s], axis=-1)

    half = da_dim // 2
    cr_, sr_ = cs(angles(t // GRID_W, half))
    cc_, sc_ = cs(angles(t % GRID_W, half))
    ca = jnp.tile(jnp.concatenate([cr_, cc_], axis=-1), (1, LANES // da_dim))
    sa = jnp.tile(jnp.concatenate([sr_, sc_], axis=-1), (1, LANES // da_dim))
    c1, s1 = cs(angles(t, ret_kdim))
    cr = jnp.tile(c1, (1, LANES // ret_kdim))
    sr = jnp.tile(s1, (1, LANES // ret_kdim))
    return ca, sa, cr, sr


ATTN_MAX_KEY_TILE = 768
ATTN_ONES_ROWS = 16


def _attn_kernel(scal_ref, q_ref, kx_ref, vx_ref, kc_ref, vc_ref, gsub_ref, o_ref,
                 k_all, v_all, s0_ref, s1_ref, s2_ref, p0_ref, p1_ref, p2_ref, a0_ref, a1_ref, a2_ref,
                 m_ref, acc_ref, *, tq, tk, with_x, da_dim):
    dv = vc_ref.shape[2]
    nc = kc_ref.shape[1]
    nx = kx_ref.shape[1] if with_x else 0
    n_steps = (nx + nc) // tk
    n_tiles = q_ref.shape[1] // tq
    rows = 2 * tq

    k_all[nx:nx + nc, :] = kc_ref[0]
    v_all[0:dv, nx:nx + nc] = vc_ref[0].T.astype(BF16)
    v_all[dv:, :] = jnp.ones((v_all.shape[0] - dv, nx + nc), BF16)
    if with_x:
        chunk = min(512, nx)

        def fill(j, c):
            off = pl.multiple_of(j * chunk, chunk)
            k_all[pl.ds(off, chunk), :] = kx_ref[0, pl.ds(off, chunk), :]
            v_all[0:dv, pl.ds(off, chunk)] = vx_ref[0, pl.ds(off, chunk), :].T.astype(BF16)
            return c

        lax.fori_loop(0, nx // chunk, fill, 0)

    s_bufs, p_bufs, a_bufs = (s0_ref, s1_ref, s2_ref), (p0_ref, p1_ref, p2_ref), (a0_ref, a1_ref, a2_ref)
    lo = _lane_iota((tq, LANES)) < da_dim

    def buf(j):
        return 2 if (n_steps % 2 == 1 and n_steps > 1 and j == n_steps - 1) else j % 2

    def q_start(t):
        return t * tq if isinstance(t, int) else pl.multiple_of(t * tq, tq)

    def qk(t, j):
        q = q_ref[0, pl.ds(q_start(t), tq), :]
        zero = jnp.zeros_like(q)
        q2 = jnp.concatenate([jnp.where(lo, q, zero), jnp.where(lo, zero, q)], axis=0)
        s_bufs[buf(j)][...] = lax.dot_general(k_all[j * tk:(j + 1) * tk, :], q2,
                                              (((1,), (1,)), ((), ())), preferred_element_type=F32)

    def softmax(j):
        b = buf(j)
        s = s_bufs[b][...]
        col_max = jnp.max(s, axis=0, keepdims=True)
        if j == 0:
            m_new = col_max
        else:
            m_prev = m_ref[...]
            m_new = jnp.maximum(m_prev, col_max)
            a_bufs[b][...] = jnp.exp(m_prev - m_new)
        p_bufs[b][...] = jnp.exp(s - m_new).astype(BF16)
        m_ref[...] = m_new

    def pv(j):
        b = buf(j)
        r = jnp.dot(v_all[:, j * tk:(j + 1) * tk], p_bufs[b][...], preferred_element_type=F32)
        if j == 0:
            acc_ref[...] = r
        else:
            acc_ref[...] = a_bufs[b][...] * acc_ref[...] + r

    def finalize(t):
        acc = acc_ref[...]
        o0 = acc[0:dv, 0:tq] / acc[dv:dv + 1, 0:tq]
        o1 = acc[0:dv, tq:2 * tq] / acc[dv:dv + 1, tq:2 * tq]
        o = o0 - scal_ref[0] * o1
        y = o * lax.rsqrt(jnp.mean(o * o, axis=0, keepdims=True) + EPS) * gsub_ref[...]
        o_ref[0, pl.ds(q_start(t), tq), :] = (y * scal_ref[1]).T.astype(BF16)

    if n_steps == 1:
        def lone(t, c):
            qk(t, 0)
            softmax(0)
            pv(0)
            finalize(t)
            return c

        lax.fori_loop(0, n_tiles, lone, 0)
    else:
        qk(0, 0)
        last = buf(n_steps - 1)
        p_bufs[last][...] = jnp.zeros(p_bufs[last].shape, BF16)
        a_bufs[last][...] = jnp.ones(a_bufs[last].shape, F32)
        acc_ref[...] = jnp.ones(acc_ref.shape, F32)

        def tile(t, c):
            qk(t, 1)
            softmax(0)
            pv(n_steps - 1)
            finalize(jnp.maximum(t - 1, 0))
            for j in range(1, n_steps - 1):
                qk(t, j + 1)
                softmax(j)
                pv(j - 1)
            qk(jnp.minimum(t + 1, n_tiles - 1), 0)
            softmax(n_steps - 1)
            pv(n_steps - 2)
            return c

        lax.fori_loop(0, n_tiles, tile, 0)
        pv(n_steps - 1)
        finalize(n_tiles - 1)


def _attention(scal, q, kx, proj_x, kc, proj_c, g_sub, cols, *, with_x, da_dim):
    b, nq, d = q.shape
    dv = g_sub.shape[0]
    heads = d // dv
    nc = kc.shape[1]
    nx = kx.shape[1]
    tq = min(nq, 256)
    n_keys = (nx if with_x else 0) + nc
    tk = max(t for t in range(LANES, ATTN_MAX_KEY_TILE + 1, LANES) if n_keys % t == 0)
    assert nq % tq == 0 and 2 * da_dim == LANES and dv == LANES
    v_blk = cols["av"] // dv
    kern = functools.partial(_attn_kernel, tq=tq, tk=tk, with_x=with_x, da_dim=da_dim)
    rows = 2 * tq
    pipelined = 2 * nq * dv * 2 + nx * dv * (2 + 4) + nc * dv * (2 + 4)
    resident = (n_keys * 3 * dv * 2 + 3 * rows * tk * (4 + 2) + 4 * rows * LANES * 4
                + rows * 2 * dv * 4 + 3 * rows * tk * 4)
    return pl.pallas_call(
        kern,
        out_shape=jax.ShapeDtypeStruct((b, nq, d), BF16),
        grid=(b, heads),
        in_specs=[pl.BlockSpec(memory_space=pltpu.SMEM),
                  pl.BlockSpec((1, nq, dv), lambda bi, h: (bi, 0, h)),
                  pl.BlockSpec((1, nx, dv), lambda bi, h: (bi, 0, h)),
                  pl.BlockSpec((1, nx, dv), lambda bi, h: (bi, 0, v_blk + h)),
                  pl.BlockSpec((1, nc, dv), lambda bi, h: (bi, 0, h)),
                  pl.BlockSpec((1, nc, dv), lambda bi, h: (bi, 0, v_blk + h)),
                  pl.BlockSpec((dv, 1), lambda bi, h: (0, 0))],
        out_specs=pl.BlockSpec((1, nq, dv), lambda bi, h: (bi, 0, h)),
        scratch_shapes=[pltpu.VMEM((n_keys, dv), BF16), pltpu.VMEM((dv + ATTN_ONES_ROWS, n_keys), BF16)]
        + [pltpu.VMEM((tk, rows), F32)] * 3 + [pltpu.VMEM((tk, rows), BF16)] * 3
        + [pltpu.VMEM((1, rows), F32)] * 4 + [pltpu.VMEM((dv + ATTN_ONES_ROWS, rows), F32)],
        compiler_params=_params(("arbitrary", "arbitrary"), pipelined, resident),
        name="diff_attention" if with_x else "diff_attention_ctx",
    )(scal, q, kx, proj_x, kc, proj_c, g_sub.reshape(dv, 1))


CONV_HALO = 16
CONV_ROW_CHUNK = 64


def _conv_kernel(a_ref, g_ref, ap_ref, gp_ref, an_ref, gn_ref, w_ref, cb_ref, lg_ref, lb_ref,
                 o_ref, ext_ref, cv_ref, sh_ref):
    i = pl.program_id(1)
    last = pl.num_programs(1) - 1
    tm = a_ref.shape[1]
    ch = a_ref.shape[2]
    width = w_ref.shape[0]
    pad = width // 2

    def glu(a, g):
        return a * _sigmoid(g)

    ext_ref[0:CONV_HALO, :] = jnp.where(i > 0, glu(ap_ref[0], gp_ref[0]), 0.0)
    ext_ref[CONV_HALO:CONV_HALO + tm, :] = glu(a_ref[0], g_ref[0])
    ext_ref[CONV_HALO + tm:CONV_HALO + tm + CONV_HALO, :] = jnp.where(i < last, glu(an_ref[0], gn_ref[0]), 0.0)

    span = sh_ref.shape[1]
    for c in range(ch // LANES):
        sl = slice(c * LANES, (c + 1) * LANES)
        for s in range(SUBLANES):
            sh_ref[s] = ext_ref[s:s + span, sl]
        def rows(i, carry, sl=sl):
            r0 = pl.multiple_of(i * CONV_ROW_CHUNK, CONV_ROW_CHUNK)
            acc = jnp.zeros((CONV_ROW_CHUNK, LANES), F32)
            for j in range(width):
                start = CONV_HALO - pad + j
                base = start - start % SUBLANES
                acc = acc + w_ref[j:j + 1, sl] * sh_ref[start % SUBLANES, pl.ds(base + r0, CONV_ROW_CHUNK), :]
            cv_ref[pl.ds(r0, CONV_ROW_CHUNK), sl] = acc + cb_ref[:, sl]
            return carry

        lax.fori_loop(0, tm // CONV_ROW_CHUNK, rows, 0)

    v = cv_ref[...]
    mu = jnp.mean(v, axis=-1, keepdims=True)
    vc = v - mu
    var = jnp.mean(vc * vc, axis=-1, keepdims=True)
    y = vc * lax.rsqrt(var + EPS) * lg_ref[...] + lb_ref[...]
    o_ref[0] = (y * _sigmoid(y)).astype(BF16)


def _conformer_conv(proj, conv_w, conv_b, ln_g, ln_b, cols):
    b, n, _ = proj.shape
    width, ch = conv_w.shape
    tm = min(n, 256)
    assert n % tm == 0 and tm % CONV_HALO == 0 and width // 2 < CONV_HALO
    a_blk = cols["conv"] // ch
    g_blk = a_blk + 1
    r = tm // CONV_HALO
    n_halo = n // CONV_HALO

    def cur(blk):
        return pl.BlockSpec((1, tm, ch), lambda bi, i: (bi, i, blk))

    def prev(blk):
        return pl.BlockSpec((1, CONV_HALO, ch), lambda bi, i: (bi, jnp.maximum(i * r - 1, 0), blk))

    def nxt(blk):
        return pl.BlockSpec((1, CONV_HALO, ch), lambda bi, i: (bi, jnp.minimum((i + 1) * r, n_halo - 1), blk))

    vec = pl.BlockSpec((1, ch), lambda bi, i: (0, 0))
    return pl.pallas_call(
        _conv_kernel,
        out_shape=jax.ShapeDtypeStruct((b, n, ch), BF16),
        grid=(b, n // tm),
        in_specs=[cur(a_blk), cur(g_blk), prev(a_blk), prev(g_blk), nxt(a_blk), nxt(g_blk),
                  pl.BlockSpec((width, ch), lambda bi, i: (0, 0)), vec, vec, vec],
        out_specs=pl.BlockSpec((1, tm, ch), lambda bi, i: (bi, i, 0)),
        scratch_shapes=[pltpu.VMEM((tm + 2 * CONV_HALO, ch), F32), pltpu.VMEM((tm, ch), F32),
                        pltpu.VMEM((SUBLANES, tm + 2 * CONV_HALO - SUBLANES, LANES), F32)],
        compiler_params=_params(("arbitrary", "arbitrary"),
                                (2 * tm + 4 * CONV_HALO) * ch * 4 + tm * ch * 2 + 40 * ch * 4,
                                (2 * tm + 2 * CONV_HALO) * ch * 4 + 4 * tm * ch * 4
                                + SUBLANES * (tm + 2 * CONV_HALO) * LANES * 4),
        name="conformer_conv",
    )(proj, proj, proj, proj, proj, proj, conv_w, conv_b.reshape(1, ch), ln_g.reshape(1, ch),
      ln_b.reshape(1, ch))


def _ret_kernel(qf_ref, kf_ref, vf_ref, gf_ref, qb_ref, kb_ref, vb_ref, gb_ref, s0_ref,
                inner_ref, xi_ref, zeta_ref, gc_ref, gng_ref, gnb_ref,
                of_ref, ob_ref, sfin_ref, s_ref, *, kdim):
    i = pl.program_id(1)
    c = qf_ref.shape[1]
    dv = gng_ref.shape[1]
    n_pair = qf_ref.shape[2] // LANES

    @pl.when(i == 0)
    def _():
        s_ref[...] = s0_ref[0]

    lo = _lane_iota((c, LANES)) < kdim
    dirs = ((qf_ref, kf_ref, vf_ref, gf_ref, of_ref), (qb_ref, kb_ref, vb_ref, gb_ref, ob_ref))
    for d, (q_ref, k_ref, v_ref, g_ref, o_ref) in enumerate(dirs):
        for p in range(n_pair):
            sl = slice(p * LANES, (p + 1) * LANES)
            q = q_ref[0, :, sl]
            k = k_ref[0, :, sl]
            kz = k * zeta_ref[d, p]
            kb16 = k.astype(BF16)
            s_pair = s_ref[d, p]
            s16 = s_pair.astype(BF16)
            upd = jnp.zeros((LANES, dv), F32)
            for hh in range(2):
                h = 2 * p + hh
                hsl = slice(h * dv, (h + 1) * dv)
                keep = lo if hh == 0 else jnp.logical_not(lo)
                qh = jnp.where(keep, q, 0.0).astype(BF16)
                v16 = v_ref[0, :, hsl].astype(BF16)
                att = lax.dot_general(qh, kb16, (((1,), (1,)), ((), ())), preferred_element_type=F32)
                att = (att * inner_ref[d, h]).astype(BF16)
                o = (jnp.dot(att, v16, preferred_element_type=F32)
                     + jnp.dot(qh, s16, preferred_element_type=F32) * xi_ref[d, h])
                kzh = jnp.where(keep, kz, 0.0).astype(BF16)
                upd = upd + lax.dot_general(kzh, v16, (((0,), (0,)), ((), ())), preferred_element_type=F32)
                mu = jnp.mean(o, axis=-1, keepdims=True)
                oc = o - mu
                var = jnp.mean(oc * oc, axis=-1, keepdims=True)
                y = oc * lax.rsqrt(var + EPS) * gng_ref[...] + gnb_ref[...]
                g = g_ref[0, :, hsl]
                o_ref[0, :, hsl] = (g * _sigmoid(g)) * y
            s_ref[d, p] = s_pair * gc_ref[p] + upd

    sfin_ref[0] = s_ref[...]


def _retention(rq, rk, proj, s0, tabs, gn_g, gn_b, cols):
    b, n, rw = rq.shape
    dv = gn_g.shape[0]
    d = cols["d"]
    c = RET_CHUNK
    nch = n // c
    kdim = dv // 2
    assert n % c == 0 and c == LANES and 2 * kdim == LANES
    n_pair = rw // LANES
    inner, xi, zeta, gc = tabs
    v_blk, gf_blk, gb_blk = cols["rv"] // d, cols["rgf"] // d, cols["rgb"] // d

    def fwd(width, blk):
        return pl.BlockSpec((1, c, width), lambda bi, i: (bi, i, blk))

    def bwd(width, blk):
        return pl.BlockSpec((1, c, width), lambda bi, i: (bi, nch - 1 - i, blk))

    def whole(a):
        return pl.BlockSpec(a.shape, lambda bi, i: (0,) * a.ndim)

    state = pl.BlockSpec((1, 2, n_pair, LANES, dv), lambda bi, i: (bi, 0, 0, 0, 0))
    vec = pl.BlockSpec((1, dv), lambda bi, i: (0, 0))
    kern = functools.partial(_ret_kernel, kdim=kdim)
    pipelined = 2 * c * (2 * rw + 2 * d) * 4 + 2 * c * d * 4 + 2 * 2 * n_pair * LANES * dv * 4
    resident = (inner.size + xi.size + zeta.size + gc.size) * 4 * 2 + 2 * n_pair * LANES * dv * 4 + 8 * MIB
    return pl.pallas_call(
        kern,
        out_shape=(jax.ShapeDtypeStruct((b, n, d), F32), jax.ShapeDtypeStruct((b, n, d), F32),
                   jax.ShapeDtypeStruct((b, 2, n_pair, LANES, dv), F32)),
        grid=(b, nch),
        in_specs=[fwd(rw, 0), fwd(rw, 0), fwd(d, v_blk), fwd(d, gf_blk),
                  bwd(rw, 0), bwd(rw, 0), bwd(d, v_blk), bwd(d, gb_blk),
                  state, whole(inner), whole(xi), whole(zeta), whole(gc), vec, vec],
        out_specs=(pl.BlockSpec((1, c, d), lambda bi, i: (bi, i, 0)),
                   pl.BlockSpec((1, c, d), lambda bi, i: (bi, nch - 1 - i, 0)),
                   state),
        scratch_shapes=[pltpu.VMEM((2, n_pair, LANES, dv), F32)],
        compiler_params=_params(("arbitrary", "arbitrary"), pipelined, resident),
        name="retention",
    )(rq, rk, proj, proj, rq, rk, proj, proj, s0, inner, xi, zeta, gc,
      gn_g.reshape(1, dv), gn_b.reshape(1, dv))


def _retention_tables(heads, kdim, dv):
    c = RET_CHUNK
    lg = jnp.log1p(-jnp.exp2(-5.0 - jnp.arange(heads, dtype=F32)))
    pos = jnp.arange(c, dtype=F32)
    diff = pos[:, None] - pos[None, :]
    inner_f = jnp.where(diff[None] >= 0, jnp.exp(jnp.maximum(diff, 0.0)[None] * lg[:, None, None]), 0.0)
    inner = jnp.stack([inner_f, jnp.swapaxes(inner_f, 1, 2)])
    xi_f = jnp.exp((pos + 1.0)[None, :] * lg[:, None])
    zeta_f = jnp.exp((c - 1.0 - pos)[None, :] * lg[:, None])
    xi = jnp.stack([xi_f, xi_f[:, ::-1]])
    zeta = jnp.stack([zeta_f, zeta_f[:, ::-1]])
    xi = jnp.broadcast_to(xi[..., None], (2, heads, c, dv))
    zeta = jnp.broadcast_to(zeta[..., None], (2, heads, c, kdim))
    zeta = zeta.reshape(2, heads // 2, 2, c, kdim).transpose(0, 1, 3, 2, 4).reshape(2, heads // 2, c, 2 * kdim)
    gc = jnp.exp(c * lg)
    gc = jnp.broadcast_to(gc[:, None, None], (heads, kdim, dv)).reshape(heads // 2, 2 * kdim, dv)
    return inner.astype(F32), xi.astype(F32), zeta.astype(F32), gc.astype(F32)


def _merge_kernel(h_ref, mod_ref, oa_ref, ob_ref, of_ref, obk_ref, gate_ref, bg_ref,
                  wa_ref, wb_ref, wc_ref, wo_ref, o_ref):
    d = h_ref.shape[2]
    g = _sigmoid(gate_ref[0] + bg_ref[...])
    oc = (of_ref[0] + obk_ref[0]).astype(BF16)
    y = g[:, 0:d] * jnp.dot(oa_ref[0], wa_ref[...], preferred_element_type=F32)
    y = y + g[:, d:2 * d] * jnp.dot(ob_ref[0], wb_ref[...], preferred_element_type=F32)
    y = y + g[:, 2 * d:3 * d] * jnp.dot(oc, wc_ref[...], preferred_element_type=F32)
    m = jnp.dot(y.astype(BF16), wo_ref[...], preferred_element_type=F32)
    o_ref[0] = h_ref[0] + mod_ref[0, 2:3, :] * m


def _merge(h, mod, oa, ob, of, obk, proj, b_gate, w_pa, w_pb, w_pc, w_out, layer, cols):
    b, n, d = h.shape
    tm = min(n, 512)
    assert n % tm == 0
    mb = mod.shape[0]
    gate_blk = cols["gate"] // (3 * d)
    row = lambda width: pl.BlockSpec((1, tm, width), lambda bi, i: (bi, i, 0))
    wspec = pl.BlockSpec((None, d, d), lambda bi, i: (layer, 0, 0))
    return pl.pallas_call(
        _merge_kernel,
        out_shape=jax.ShapeDtypeStruct((b, n, d), F32),
        grid=(b, n // tm),
        in_specs=[row(d), pl.BlockSpec((1, N_MOD, d), lambda bi, i: (bi % mb, 0, 0)),
                  row(d), row(d), row(d), row(d),
                  pl.BlockSpec((1, tm, 3 * d), lambda bi, i: (bi, i, gate_blk)),
                  pl.BlockSpec((1, 3 * d), lambda bi, i: (0, 0)),
                  wspec, wspec, wspec, wspec],
        out_specs=row(d),
        compiler_params=_params(("arbitrary", "arbitrary"),
                                tm * d * (4 + 2 + 2 + 4 + 4 + 12 + 4) + 4 * d * d * 2, 10 * tm * d * 4),
        name="merge",
    )(h, mod, oa, ob, of, obk, proj, b_gate.reshape(1, 3 * d), w_pa, w_pb, w_pc, w_out)


ROUTE_E, ROUTE_G, ROUTE_R = 0, 4, 8


def _route_kernel(h_ref, mod_ref, g_ref, wr_ref, br_ref, tri_ref, base_ref,
                  hx_ref, route_ref, cnt_ref, run_ref, *, n_exp):
    first = jnp.logical_and(pl.program_id(0) == 0, pl.program_id(1) == 0)

    @pl.when(first)
    def _():
        run_ref[...] = base_ref[...]

    h = h_ref[0]
    tm = h.shape[0]
    y = h * lax.rsqrt(jnp.mean(h * h, axis=-1, keepdims=True) + EPS) * g_ref[...]
    hx = y * (1.0 + mod_ref[0, 4:5, :]) + mod_ref[0, 3:4, :]
    hx_ref[0] = hx

    lane = _lane_iota((tm, LANES))
    lane_f = lane.astype(F32)
    hx_hi = hx.astype(BF16)
    hx_lo = (hx - hx_hi.astype(F32)).astype(BF16)
    logits = (jnp.dot(hx_hi, wr_ref[0], preferred_element_type=F32)
              + jnp.dot(hx_hi, wr_ref[1], preferred_element_type=F32)
              + jnp.dot(hx_lo, wr_ref[0], preferred_element_type=F32)) + br_ref[...]
    logits = jnp.where(lane < n_exp, logits, NEG_INF)

    sels, vals = [], []
    for _ in range(TOP_K):
        mx = jnp.max(logits, axis=-1, keepdims=True)
        idx = jnp.min(jnp.where(logits == mx, lane_f, float(LANES)), axis=-1, keepdims=True)
        sel = lane_f == idx
        sels.append((sel, idx))
        vals.append(mx)
        logits = jnp.where(sel, NEG_INF, logits)

    es = [jnp.exp(v - vals[0]) for v in vals]
    denom = es[0]
    for e in es[1:]:
        denom = denom + e

    onehot = jnp.zeros((tm, LANES), F32)
    for sel, _ in sels:
        onehot = onehot + jnp.where(sel, 1.0, 0.0)
    before = run_ref[...] + jnp.dot(tri_ref[...], onehot.astype(BF16), preferred_element_type=F32)

    route = jnp.zeros((tm, LANES), F32)
    for k, (sel, idx) in enumerate(sels):
        rank = jnp.sum(jnp.where(sel, before, 0.0), axis=-1, keepdims=True)
        route = jnp.where(lane == ROUTE_E + k, idx, route)
        route = jnp.where(lane == ROUTE_G + k, es[k] / denom, route)
        route = jnp.where(lane == ROUTE_R + k, rank, route)
    route_ref[0] = route

    run_ref[...] = run_ref[...] + jnp.sum(onehot, axis=0, keepdims=True)
    cnt_ref[...] = run_ref[...]


def _route(h, mod, g, w_router, b_router, base):
    b, n, d = h.shape
    n_exp = w_router.shape[-1]
    tm = min(n, 256)
    assert n % tm == 0 and n_exp <= LANES
    mb = mod.shape[0]
    wr = jnp.zeros((d, LANES), F32).at[:, :n_exp].set(w_router)
    wr_hi = wr.astype(BF16)
    wr = jnp.stack([wr_hi, (wr - wr_hi.astype(F32)).astype(BF16)])
    br = jnp.zeros((1, LANES), F32).at[0, :n_exp].set(b_router)
    tri = (jnp.arange(tm)[:, None] > jnp.arange(tm)[None, :]).astype(BF16)
    row = lambda width: pl.BlockSpec((1, tm, width), lambda bi, i: (bi, i, 0))
    const = lambda shape: pl.BlockSpec(shape, lambda bi, i: (0,) * len(shape))
    kern = functools.partial(_route_kernel, n_exp=n_exp)
    return pl.pallas_call(
        kern,
        out_shape=(jax.ShapeDtypeStruct((b, n, d), F32), jax.ShapeDtypeStruct((b, n, LANES), F32),
                   jax.ShapeDtypeStruct((1, LANES), F32)),
        grid=(b, n // tm),
        in_specs=[row(d), pl.BlockSpec((1, N_MOD, d), lambda bi, i: (bi % mb, 0, 0)), const((1, d)),
                  const((2, d, LANES)), const((1, LANES)), const((tm, tm)), const((1, LANES))],
        out_specs=(row(d), row(LANES), const((1, LANES))),
        scratch_shapes=[pltpu.VMEM((1, LANES), F32)],
        compiler_params=_params(("arbitrary", "arbitrary"),
                                2 * tm * d * 4 + tm * LANES * 4 + d * LANES * 4 + tm * tm * 2, 8 * tm * d * 4),
        name="moe_route",
    )(h, mod, g.reshape(1, d), wr, br, tri, base)


DMA_ISSUE_UNROLL = 8


def _tile_dest(dest, tm):
    t = dest.shape[0]
    return dest.reshape(t // tm, tm, TOP_K).transpose(0, 2, 1).reshape(t // tm, 1, TOP_K * tm)


def _dispatch_kernel(dest_ref, hx_ref, xs_in_ref, xs_ref, sem):
    del xs_in_ref
    tm = hx_ref.shape[0]

    def copy(k, t):
        return pltpu.make_async_copy(hx_ref.at[pl.ds(t, 1)],
                                     xs_ref.at[pl.ds(dest_ref[0, 0, k * tm + t], 1)], sem)

    for t in range(tm):
        for k in range(TOP_K):
            copy(k, t).start(priority=(t * TOP_K + k) % 2)
    for k in range(TOP_K):
        def wait(t, c, k=k):
            copy(k, t).wait()
            return c

        lax.fori_loop(0, tm, wait, 0, unroll=DMA_ISSUE_UNROLL)


def _dispatch(hx_flat, dest, xs):
    t, d = hx_flat.shape
    tm = min(t, 256)
    assert t % tm == 0
    dest3 = _tile_dest(dest, tm)
    return pl.pallas_call(
        _dispatch_kernel,
        out_shape=jax.ShapeDtypeStruct(xs.shape, xs.dtype),
        grid=(t // tm,),
        in_specs=[pl.BlockSpec((1, 1, tm * TOP_K), lambda i: (i, 0, 0), memory_space=pltpu.SMEM),
                  pl.BlockSpec((tm, d), lambda i: (i, 0)),
                  pl.BlockSpec(memory_space=pl.ANY)],
        out_specs=pl.BlockSpec(memory_space=pl.ANY),
        scratch_shapes=[pltpu.SemaphoreType.DMA],
        input_output_aliases={2: 0},
        compiler_params=pltpu.CompilerParams(dimension_semantics=("arbitrary",),
                                             vmem_limit_bytes=_vmem_limit(tm * d * 4, 4 * MIB),
                                             has_side_effects=True),
        name="moe_dispatch",
    )(dest3, hx_flat, xs)


def _split_kernel(w_ref, sel_ref, g_ref, u_ref):
    blk = sel_ref.shape[0]
    half = blk // 2
    for i in range(w_ref.shape[1] // blk):
        r = jnp.dot(w_ref[:, i * blk:(i + 1) * blk].astype(BF16), sel_ref[...], preferred_element_type=F32)
        g_ref[:, i * half:(i + 1) * half] = r[:, :half].astype(BF16)
        u_ref[:, i * half:(i + 1) * half] = r[:, half:].astype(BF16)


def _split_gate_up(w_gate_up):
    depth, n_exp, d, de2 = w_gate_up.shape
    de = de2 // 2
    tr = min(d, 512)
    assert d % tr == 0
    blk = 2 * LANES
    assert de2 % blk == 0
    order = jnp.concatenate([jnp.arange(0, blk, 2), jnp.arange(1, blk, 2)])
    sel = (jnp.arange(blk)[:, None] == order[None, :]).astype(BF16)
    out = jax.ShapeDtypeStruct((depth, n_exp, d, de), BF16)
    ospec = pl.BlockSpec((None, None, tr, de), lambda l, e, r: (l, e, r, 0))
    return pl.pallas_call(
        _split_kernel,
        out_shape=(out, out),
        grid=(depth, n_exp, d // tr),
        in_specs=[pl.BlockSpec((None, None, tr, de2), lambda l, e, r: (l, e, r, 0)),
                  pl.BlockSpec((blk, blk), lambda l, e, r: (0, 0))],
        out_specs=(ospec, ospec),
        compiler_params=_params(("arbitrary", "arbitrary", "arbitrary"),
                                tr * de2 * 4 + blk * blk * 2 + 2 * tr * de * 2, tr * de2 * (2 + 4 + 4)),
        name="split_gate_up",
    )(w_gate_up, sel)


def _expert_kernel(blk_e_ref, n_used_ref, x_ref, wg_ref, wu_ref, wd_ref, bg_ref, bu_ref, bd_ref, y_ref):
    del blk_e_ref
    used = pl.program_id(0) < n_used_ref[0]

    @pl.when(used)
    def _():
        x = x_ref[...].astype(BF16)
        gl = jnp.dot(x, wg_ref[...], preferred_element_type=F32) + bg_ref[...]
        up = jnp.dot(x, wu_ref[...], preferred_element_type=F32) + bu_ref[...]
        gl = jnp.minimum(gl, SWIGLU_LIMIT)
        up = jnp.clip(up, -SWIGLU_LIMIT, SWIGLU_LIMIT)
        act = (up + 1.0) * (gl * _sigmoid(SWIGLU_ALPHA * gl))
        y_ref[...] = jnp.dot(act.astype(BF16), wd_ref[...], preferred_element_type=F32) + bd_ref[...]

    @pl.when(jnp.logical_not(used))
    def _():
        y_ref[...] = jnp.zeros(y_ref.shape, F32)


def _experts(blk_e, n_used, xs, wg, wu, wd, bg, bu, bd, layer, bm):
    n_rows, d = xs.shape
    de = wg.shape[-1]
    nb = n_rows // bm
    wspec = lambda a, c: pl.BlockSpec((None, None, a, c), lambda i, be, nu: (layer, be[i], 0, 0))
    return pl.pallas_call(
        _expert_kernel,
        out_shape=jax.ShapeDtypeStruct((n_rows, d), F32),
        grid_spec=pltpu.PrefetchScalarGridSpec(
            num_scalar_prefetch=2,
            grid=(nb,),
            in_specs=[pl.BlockSpec((bm, d), lambda i, be, nu: (i, 0)),
                      wspec(d, de), wspec(d, de), wspec(de, d),
                      wspec(1, de), wspec(1, de), wspec(1, d)],
            out_specs=pl.BlockSpec((bm, d), lambda i, be, nu: (i, 0))),
        compiler_params=_params(("arbitrary",), 2 * bm * d * 4 + 3 * d * de * 2, 8 * bm * de * 4),
        name="moe_experts",
    )(blk_e, n_used, xs, wg, wu, wd, bg, bu, bd)


def _combine_kernel(dest_ref, h_ref, mod_ref, route_ref, y_ref, o_ref, ybuf, sem):
    tm = h_ref.shape[1]

    def copy(r):
        return pltpu.make_async_copy(y_ref.at[pl.ds(dest_ref[0, 0, r], 1)], ybuf.at[pl.ds(r, 1)], sem)

    for r in range(TOP_K * tm):
        copy(r).start(priority=r % 2)

    def wait(r, c):
        copy(r).wait()
        return c

    lax.fori_loop(0, TOP_K * tm, wait, 0, unroll=DMA_ISSUE_UNROLL)

    route = route_ref[0]
    f = jnp.zeros((tm, h_ref.shape[2]), F32)
    for k in range(TOP_K):
        f = f + route[:, ROUTE_G + k:ROUTE_G + k + 1] * ybuf[k * tm:(k + 1) * tm, :]
    o_ref[0] = h_ref[0] + mod_ref[0, 5:6, :] * f


def _combine(h, mod, route, dest, y):
    b, n, d = h.shape
    tm = min(n, 256)
    assert n % tm == 0
    mb = mod.shape[0]
    nt = n // tm
    dest3 = _tile_dest(dest, tm)
    row = lambda width: pl.BlockSpec((1, tm, width), lambda bi, i: (bi, i, 0))
    return pl.pallas_call(
        _combine_kernel,
        out_shape=jax.ShapeDtypeStruct((b, n, d), F32),
        grid=(b, nt),
        in_specs=[pl.BlockSpec((1, 1, tm * TOP_K), lambda bi, i: (bi * nt + i, 0, 0), memory_space=pltpu.SMEM),
                  row(d), pl.BlockSpec((1, N_MOD, d), lambda bi, i: (bi % mb, 0, 0)), row(LANES),
                  pl.BlockSpec(memory_space=pl.ANY)],
        out_specs=row(d),
        scratch_shapes=[pltpu.VMEM((TOP_K * tm, d), F32), pltpu.SemaphoreType.DMA],
        compiler_params=_params(("arbitrary", "arbitrary"), 2 * tm * d * 4 + tm * LANES * 4,
                                TOP_K * tm * d * 4 + 4 * tm * d * 4),
        name="moe_combine",
    )(dest3, h, mod, route, y)


def _moe_rows(n_tok, n_exp, bm):
    return -(-(n_tok * TOP_K + n_exp * (bm - 1)) // bm) * bm


def _moe(parts, xs, g, w_router, b_router, wg, wu, wd, bg, bu, bd, layer, bm):
    n_exp = w_router.shape[-1]
    n_rows, d = xs.shape
    base = jnp.zeros((1, LANES), F32)
    hxs, routes = [], []
    for h, mod in parts:
        hx, route, base = _route(h, mod, g, w_router, b_router, base)
        hxs.append(hx)
        routes.append(route)
    counts = base[0, :n_exp].astype(jnp.int32)

    padded = (counts + bm - 1) // bm * bm
    pend = jnp.cumsum(padded)
    pstart = pend - padded
    n_tok = sum(h.shape[0] * h.shape[1] for h, _ in parts)
    assert n_rows >= _moe_rows(n_tok, n_exp, bm) and n_rows % bm == 0
    nb = n_rows // bm
    blk_start = jnp.arange(nb, dtype=jnp.int32) * bm
    blk_e = jnp.minimum(jnp.sum((pend[None, :] <= blk_start[:, None]).astype(jnp.int32), axis=1), n_exp - 1)
    n_used = (pend[-1] // bm).astype(jnp.int32).reshape(1)
    eids = jnp.arange(n_exp, dtype=jnp.int32)

    dests = []
    for hx, route in zip(hxs, routes):
        e = route[..., ROUTE_E:ROUTE_E + TOP_K].astype(jnp.int32)
        rank = route[..., ROUTE_R:ROUTE_R + TOP_K].astype(jnp.int32)
        start = jnp.sum(jnp.where(e[..., None] == eids, pstart, 0), axis=-1)
        dest = (start + rank).reshape(-1, TOP_K)
        dests.append(dest)
        xs = _dispatch(hx.reshape(-1, d), dest, xs)

    y = _experts(blk_e, n_used, xs, wg, wu, wd, bg, bu, bd, layer, bm)
    return [_combine(h, mod, route, dest, y)
            for (h, mod), route, dest in zip(parts, routes, dests)], xs


def kernel(x, c, ctx, c_ctx, w_mod, b_mod, g_norm1, g_norm2, w_in, b_gate, g_qa, g_ka, lam_q1, lam_k1, lam_q2, lam_k2, g_sub, conv_w, conv_b, ln_g, ln_b, gn_g, gn_b, w_pa, w_pb, w_pc, w_out, w_router, b_router, w_gate_up, b_gate_up, w_down, b_down):
    bsz, n_x, d = x.shape
    n_c = ctx.shape[1]
    depth = w_mod.shape[0]
    da_dim = g_qa.shape[-1]
    dv = g_sub.shape[-1]
    heads = d // dv
    ret_kdim = gn_g.shape[-1] // 2
    n_exp = w_router.shape[-1]
    de = w_down.shape[-2]
    moe_bm = 512

    cols = {"d": d, "ret_qk": heads * ret_kdim}
    off = 0
    for name, width in (("aq", d), ("ak", d), ("av", d), ("conv", 2 * d), ("rq", heads * ret_kdim),
                        ("rk", heads * ret_kdim), ("rv", d), ("rgf", d), ("rgb", d), ("gate", 3 * d)):
        cols[name] = off
        off += width
    assert off == w_in.shape[-1]

    w_in_bf = w_in.astype(BF16)
    w_pa_bf, w_pb_bf, w_pc_bf, w_out_bf = (w.astype(BF16) for w in (w_pa, w_pb, w_pc, w_out))
    wg, wu = _split_gate_up(w_gate_up)
    wd = w_down.astype(BF16)
    bg = b_gate_up[..., 0::2].reshape(depth, n_exp, 1, de)
    bu = b_gate_up[..., 1::2].reshape(depth, n_exp, 1, de)
    bd = b_down.reshape(depth, n_exp, 1, d)

    rows = -(-(bsz + 1) // SUBLANES) * SUBLANES
    cvecs = jnp.zeros((rows, d), F32).at[:bsz].set(c).at[bsz].set(c_ctx)
    mods = _modulation(cvecs, w_mod, b_mod).reshape(depth, rows, N_MOD, d)

    tabs_x = _rope_tables(n_x, da_dim, ret_kdim)
    tabs_c = tuple(jnp.zeros((n_c, LANES), F32) for _ in range(4))
    ret_tabs = _retention_tables(heads, ret_kdim, dv)
    s_zero = jnp.zeros((bsz, 2, heads // 2, LANES, dv), F32)
    xs_rows = jnp.zeros((_moe_rows(bsz * (n_c + n_x), n_exp, moe_bm), d), F32)

    h_ctx = ctx
    for l in range(depth):
        need_ctx = l < depth - 1
        lam_init = 0.8 - 0.6 * math.exp(-0.3 * l)
        lam = (jnp.exp(jnp.sum(lam_q1[l] * lam_k1[l])) - jnp.exp(jnp.sum(lam_q2[l] * lam_k2[l])) + lam_init)
        scal = jnp.stack([lam, jnp.asarray(1.0 - lam_init, F32)]).astype(F32)
        mod_x = mods[l, :bsz]
        mod_c = mods[l, bsz:bsz + 1]

        proj_x, aq_x, ak_x, rq_x, rk_x = _in_proj(x, mod_x, g_norm1[l], w_in_bf, l, g_qa[l], g_ka[l], tabs_x,
                                                  cols, rope=True, da_dim=da_dim, ret_kdim=ret_kdim)
        proj_c, aq_c, ak_c, rq_c, rk_c = _in_proj(h_ctx, mod_c, g_norm1[l], w_in_bf, l, g_qa[l], g_ka[l],
                                                  tabs_c, cols, rope=False, da_dim=da_dim, ret_kdim=ret_kdim)
        oa_x = _attention(scal, aq_x, ak_x, proj_x, ak_c, proj_c, g_sub[l], cols, with_x=True, da_dim=da_dim)
        ob_x = _conformer_conv(proj_x, conv_w[l], conv_b[l], ln_g[l], ln_b[l], cols)
        of_c, ob_c, s_ctx = _retention(rq_c, rk_c, proj_c, s_zero, ret_tabs, gn_g[l], gn_b[l], cols)
        of_x, obk_x, _ = _retention(rq_x, rk_x, proj_x, s_ctx, ret_tabs, gn_g[l], gn_b[l], cols)
        x = _merge(x, mod_x, oa_x, ob_x, of_x, obk_x, proj_x, b_gate[l],
                   w_pa_bf, w_pb_bf, w_pc_bf, w_out_bf, l, cols)
        if need_ctx:
            oa_c = _attention(scal, aq_c, ak_c, proj_c, ak_c, proj_c, g_sub[l], cols, with_x=False, da_dim=da_dim)
            cb_c = _conformer_conv(proj_c, conv_w[l], conv_b[l], ln_g[l], ln_b[l], cols)
            h_ctx = _merge(h_ctx, mod_c, oa_c, cb_c, of_c, ob_c, proj_c, b_gate[l],
                           w_pa_bf, w_pb_bf, w_pc_bf, w_out_bf, l, cols)

        parts = [(h_ctx, mod_c), (x, mod_x)] if need_ctx else [(x, mod_x)]
        outs, xs_rows = _moe(parts, xs_rows, g_norm2[l], w_router[l], b_router[l], wg, wu, wd, bg, bu, bd,
                             l, moe_bm)
        if need_ctx:
            h_ctx, x = outs
        else:
            (x,) = outs
    return x
```

```python
import functools
import math

import jax
import jax.numpy as jnp
from jax import lax
from jax.experimental import pallas as pl
from jax.experimental.pallas import tpu as pltpu

F32 = jnp.float32
BF16 = jnp.bfloat16

GRID_W = 64
N_MOD = 6
EPS = 1e-6
ROPE_BASE = 10000.0
RET_CHUNK = 128
TOP_K = 4
SWIGLU_ALPHA = 1.702
SWIGLU_LIMIT = 7.0

LANES = 128
SUBLANES = 8
V7X_VMEM_BYTES = 64 * 2**20
VMEM_LIMIT_CAP = 56 * 2**20
MIB = 2**20

NEG_INF = float("-inf")


def _sigmoid(x):
    return 1.0 / (1.0 + jnp.exp(-x))


def _vmem_limit(pipelined_bytes, resident_bytes):
    need = 2 * pipelined_bytes + resident_bytes
    return int(min(max(need, 16 * MIB), VMEM_LIMIT_CAP))


def _params(sem, pipelined_bytes, resident_bytes):
    return pltpu.CompilerParams(dimension_semantics=sem,
                                vmem_limit_bytes=_vmem_limit(pipelined_bytes, resident_bytes))


def _lane_iota(shape):
    return lax.broadcasted_iota(jnp.int32, shape, len(shape) - 1)


def _mod_kernel(c_ref, w_ref, b_ref, o_ref):
    c = c_ref[...]
    a = (c * _sigmoid(c)).astype(BF16)
    o_ref[...] = jnp.dot(a, w_ref[...].astype(BF16), preferred_element_type=F32) + b_ref[...]


def _modulation(cvecs, w_mod, b_mod):
    depth, d, cols = w_mod.shape
    rows = cvecs.shape[0]
    tn = 1536
    assert cols % tn == 0
    return pl.pallas_call(
        _mod_kernel,
        out_shape=jax.ShapeDtypeStruct((depth, rows, cols), F32),
        grid=(depth, cols // tn),
        in_specs=[pl.BlockSpec((rows, d), lambda l, j: (0, 0)),
                  pl.BlockSpec((None, d, tn), lambda l, j: (l, 0, j)),
                  pl.BlockSpec((None, 1, tn), lambda l, j: (l, 0, j))],
        out_specs=pl.BlockSpec((None, rows, tn), lambda l, j: (l, 0, j)),
        compiler_params=_params(("arbitrary", "arbitrary"), d * tn * 4 + rows * tn * 4, 4 * MIB),
        name="modulation",
    )(cvecs, w_mod, b_mod.reshape(depth, 1, cols))


def _inproj_kernel(h_ref, mod_ref, g_ref, w_ref, o_ref, xs_ref):
    @pl.when(pl.program_id(2) == 0)
    def _():
        h = h_ref[0]
        y = h * lax.rsqrt(jnp.mean(h * h, axis=-1, keepdims=True) + EPS) * g_ref[...]
        xs_ref[...] = (y * (1.0 + mod_ref[0, 1:2, :]) + mod_ref[0, 0:1, :]).astype(BF16)

    o_ref[0] = jnp.dot(xs_ref[...], w_ref[...], preferred_element_type=F32)


def _in_proj(h, mod, g, w_in_bf, layer):
    b, n, d = h.shape
    cols = w_in_bf.shape[-1]
    tm = min(n, 1024)
    tn = 2048
    assert n % tm == 0 and cols % tn == 0
    mb = mod.shape[0]
    return pl.pallas_call(
        _inproj_kernel,
        out_shape=jax.ShapeDtypeStruct((b, n, cols), F32),
        grid=(b, n // tm, cols // tn),
        in_specs=[pl.BlockSpec((1, tm, d), lambda bi, i, j: (bi, i, 0)),
                  pl.BlockSpec((1, N_MOD, d), lambda bi, i, j: (bi % mb, 0, 0)),
                  pl.BlockSpec((1, d), lambda bi, i, j: (0, 0)),
                  pl.BlockSpec((None, d, tn), lambda bi, i, j: (layer, 0, j))],
        out_specs=pl.BlockSpec((1, tm, tn), lambda bi, i, j: (bi, i, j)),
        scratch_shapes=[pltpu.VMEM((tm, d), BF16)],
        compiler_params=_params(("arbitrary", "arbitrary", "arbitrary"),
                                tm * d * 4 + d * tn * 2 + tm * tn * 4, tm * d * 2 + 3 * tm * d * 4),
        name="in_proj",
    )(h, mod, g.reshape(1, d), w_in_bf)


def _prep_kernel(aq_ref, ak_ref, rq_ref, rk_ref, gq_ref, gk_ref, ca_ref, sa_ref, cr_ref, sr_ref,
                 oq_ref, ok_ref, orq_ref, ork_ref, *, rope, da_dim, ret_kdim):
    tm = aq_ref.shape[1]
    lane = _lane_iota((tm, LANES))
    lo = lane < da_dim

    def rms(x, g):
        x2 = x * x
        s_lo = jnp.sum(jnp.where(lo, x2, 0.0), axis=-1, keepdims=True)
        s_hi = jnp.sum(jnp.where(lo, 0.0, x2), axis=-1, keepdims=True)
        ms = jnp.where(lo, s_lo, s_hi) * (1.0 / da_dim)
        return x * lax.rsqrt(ms + EPS) * g

    def rot(x, c, s, half):
        first = (lane % (2 * half)) < half
        partner = jnp.where(first, pltpu.roll(x, LANES - half, 1), pltpu.roll(x, half, 1))
        return x * c + partner * s

    q_scale = da_dim ** -0.5
    k_scale = ret_kdim ** -0.5
    for j in range(aq_ref.shape[2] // LANES):
        sl = slice(j * LANES, (j + 1) * LANES)
        q = rms(aq_ref[0, :, sl], gq_ref[...])
        k = rms(ak_ref[0, :, sl], gk_ref[...])
        if rope:
            q = rot(q, ca_ref[...], sa_ref[...], da_dim // 4)
            k = rot(k, ca_ref[...], sa_ref[...], da_dim // 4)
        oq_ref[0, :, sl] = (q * q_scale).astype(BF16)
        ok_ref[0, :, sl] = k.astype(BF16)
    for j in range(rq_ref.shape[2] // LANES):
        sl = slice(j * LANES, (j + 1) * LANES)
        q = rq_ref[0, :, sl]
        k = rk_ref[0, :, sl]
        if rope:
            q = rot(q, cr_ref[...], sr_ref[...], ret_kdim // 2)
            k = rot(k, cr_ref[...], sr_ref[...], ret_kdim // 2)
        orq_ref[0, :, sl] = q
        ork_ref[0, :, sl] = k * k_scale


def _qk_prep(proj, g_q, g_k, tabs, cols, *, rope, da_dim, ret_kdim):
    b, n, _ = proj.shape
    d = cols["d"]
    rw = cols["ret_qk"]
    tm = min(n, 1024)
    assert n % tm == 0
    gq = jnp.tile(g_q, LANES // da_dim).reshape(1, LANES)
    gk = jnp.tile(g_k, LANES // da_dim).reshape(1, LANES)
    tab_spec = pl.BlockSpec((tm, LANES), lambda bi, i: (i, 0))
    vec_spec = pl.BlockSpec((1, LANES), lambda bi, i: (0, 0))

    def col_spec(width, off):
        assert off % width == 0
        return pl.BlockSpec((1, tm, width), lambda bi, i: (bi, i, off // width))

    kern = functools.partial(_prep_kernel, rope=rope, da_dim=da_dim, ret_kdim=ret_kdim)
    return pl.pallas_call(
        kern,
        out_shape=(jax.ShapeDtypeStruct((b, n, d), BF16), jax.ShapeDtypeStruct((b, n, d), BF16),
                   jax.ShapeDtypeStruct((b, n, rw), F32), jax.ShapeDtypeStruct((b, n, rw), F32)),
        grid=(b, n // tm),
        in_specs=[col_spec(d, cols["aq"]), col_spec(d, cols["ak"]),
                  col_spec(rw, cols["rq"]), col_spec(rw, cols["rk"]),
                  vec_spec, vec_spec, tab_spec, tab_spec, tab_spec, tab_spec],
        out_specs=(pl.BlockSpec((1, tm, d), lambda bi, i: (bi, i, 0)),
                   pl.BlockSpec((1, tm, d), lambda bi, i: (bi, i, 0)),
                   pl.BlockSpec((1, tm, rw), lambda bi, i: (bi, i, 0)),
                   pl.BlockSpec((1, tm, rw), lambda bi, i: (bi, i, 0))),
        compiler_params=_params(("arbitrary", "arbitrary"),
                                tm * (2 * d + 2 * rw) * 4 + tm * (2 * d * 2 + 2 * rw * 4)
                                + 4 * tm * LANES * 4, 8 * MIB),
        name="qk_prep",
    )(proj, proj, proj, proj, gq, gk, *tabs)


def _rope_tables(n_tok, da_dim, ret_kdim):
    t = jnp.arange(n_tok)

    def angles(pos, dim):
        inv = ROPE_BASE ** (-jnp.arange(0, dim, 2, dtype=F32) / dim)
        return pos.astype(F32)[:, None] * inv[None, :]

    def cs(ang):
        c, s = jnp.cos(ang), jnp.sin(ang)
        return jnp.concatenate([c, c], axis=-1), jnp.concatenate([-s, s], axis=-1)

    half = da_dim // 2
    cr_, sr_ = cs(angles(t // GRID_W, half))
    cc_, sc_ = cs(angles(t % GRID_W, half))
    ca = jnp.tile(jnp.concatenate([cr_, cc_], axis=-1), (1, LANES // da_dim))
    sa = jnp.tile(jnp.concatenate([sr_, sc_], axis=-1), (1, LANES // da_dim))
    c1, s1 = cs(angles(t, ret_kdim))
    cr = jnp.tile(c1, (1, LANES // ret_kdim))
    sr = jnp.tile(s1, (1, LANES // ret_kdim))
    return ca, sa, cr, sr


ATTN_MAX_KEY_TILE = 768
ATTN_ONES_ROWS = 16


def _attn_kernel(scal_ref, q_ref, kx_ref, vx_ref, kc_ref, vc_ref, gsub_ref, o_ref,
                 k_all, v_all, s0_ref, s1_ref, s2_ref, p0_ref, p1_ref, p2_ref, a0_ref, a1_ref, a2_ref,
                 m_ref, acc_ref, *, tq, tk, with_x, da_dim):
    dv = vc_ref.shape[2]
    nc = kc_ref.shape[1]
    nx = kx_ref.shape[1] if with_x else 0
    n_steps = (nx + nc) // tk
    n_tiles = q_ref.shape[1] // tq
    rows = 2 * tq

    k_all[nx:nx + nc, :] = kc_ref[0]
    v_all[0:dv, nx:nx + nc] = vc_ref[0].T.astype(BF16)
    v_all[dv:, :] = jnp.ones((v_all.shape[0] - dv, nx + nc), BF16)
    if with_x:
        chunk = min(512, nx)

        def fill(j, c):
            off = pl.multiple_of(j * chunk, chunk)
            k_all[pl.ds(off, chunk), :] = kx_ref[0, pl.ds(off, chunk), :]
            v_all[0:dv, pl.ds(off, chunk)] = vx_ref[0, pl.ds(off, chunk), :].T.astype(BF16)
            return c

        lax.fori_loop(0, nx // chunk, fill, 0)

    s_bufs, p_bufs, a_bufs = (s0_ref, s1_ref, s2_ref), (p0_ref, p1_ref, p2_ref), (a0_ref, a1_ref, a2_ref)
    lo = _lane_iota((tq, LANES)) < da_dim

    def buf(j):
        return 2 if (n_steps % 2 == 1 and n_steps > 1 and j == n_steps - 1) else j % 2

    def q_start(t):
        return t * tq if isinstance(t, int) else pl.multiple_of(t * tq, tq)

    def qk(t, j):
        q = q_ref[0, pl.ds(q_start(t), tq), :]
        zero = jnp.zeros_like(q)
        q2 = jnp.concatenate([jnp.where(lo, q, zero), jnp.where(lo, zero, q)], axis=0)
        s_bufs[buf(j)][...] = lax.dot_general(k_all[j * tk:(j + 1) * tk, :], q2,
                                              (((1,), (1,)), ((), ())), preferred_element_type=F32)

    def softmax(j):
        b = buf(j)
        s = s_bufs[b][...]
        col_max = jnp.max(s, axis=0, keepdims=True)
        if j == 0:
            m_new = col_max
        else:
            m_prev = m_ref[...]
            m_new = jnp.maximum(m_prev, col_max)
            a_bufs[b][...] = jnp.exp(m_prev - m_new)
        p_bufs[b][...] = jnp.exp(s - m_new).astype(BF16)
        m_ref[...] = m_new

    def pv(j):
        b = buf(j)
        r = jnp.dot(v_all[:, j * tk:(j + 1) * tk], p_bufs[b][...], preferred_element_type=F32)
        if j == 0:
            acc_ref[...] = r
        else:
            acc_ref[...] = a_bufs[b][...] * acc_ref[...] + r

    def finalize(t):
        acc = acc_ref[...]
        o0 = acc[0:dv, 0:tq] / acc[dv:dv + 1, 0:tq]
        o1 = acc[0:dv, tq:2 * tq] / acc[dv:dv + 1, tq:2 * tq]
        o = o0 - scal_ref[0] * o1
        y = o * lax.rsqrt(jnp.mean(o * o, axis=0, keepdims=True) + EPS) * gsub_ref[...]
        o_ref[0, pl.ds(q_start(t), tq), :] = (y * scal_ref[1]).T.astype(BF16)

    if n_steps == 1:
        def lone(t, c):
            qk(t, 0)
            softmax(0)
            pv(0)
            finalize(t)
            return c

        lax.fori_loop(0, n_tiles, lone, 0)
    else:
        qk(0, 0)
        last = buf(n_steps - 1)
        p_bufs[last][...] = jnp.zeros(p_bufs[last].shape, BF16)
        a_bufs[last][...] = jnp.ones(a_bufs[last].shape, F32)
        acc_ref[...] = jnp.ones(acc_ref.shape, F32)

        def tile(t, c):
            qk(t, 1)
            softmax(0)
            pv(n_steps - 1)
            finalize(jnp.maximum(t - 1, 0))
            for j in range(1, n_steps - 1):
                qk(t, j + 1)
                softmax(j)
                pv(j - 1)
            qk(jnp.minimum(t + 1, n_tiles - 1), 0)
            softmax(n_steps - 1)
            pv(n_steps - 2)
            return c

        lax.fori_loop(0, n_tiles, tile, 0)
        pv(n_steps - 1)
        finalize(n_tiles - 1)


def _attention(scal, q, kx, proj_x, kc, proj_c, g_sub, cols, *, with_x, da_dim):
    b, nq, d = q.shape
    dv = g_sub.shape[0]
    heads = d // dv
    nc = kc.shape[1]
    nx = kx.shape[1]
    tq = min(nq, 256)
    n_keys = (nx if with_x else 0) + nc
    tk = max(t for t in range(LANES, ATTN_MAX_KEY_TILE + 1, LANES) if n_keys % t == 0)
    assert nq % tq == 0 and 2 * da_dim == LANES and dv == LANES
    v_blk = cols["av"] // dv
    kern = functools.partial(_attn_kernel, tq=tq, tk=tk, with_x=with_x, da_dim=da_dim)
    rows = 2 * tq
    pipelined = 2 * nq * dv * 2 + nx * dv * (2 + 4) + nc * dv * (2 + 4)
    resident = (n_keys * 3 * dv * 2 + 3 * rows * tk * (4 + 2) + 4 * rows * LANES * 4
                + rows * 2 * dv * 4 + 3 * rows * tk * 4)
    return pl.pallas_call(
        kern,
        out_shape=jax.ShapeDtypeStruct((b, nq, d), BF16),
        grid=(b, heads),
        in_specs=[pl.BlockSpec(memory_space=pltpu.SMEM),
                  pl.BlockSpec((1, nq, dv), lambda bi, h: (bi, 0, h)),
                  pl.BlockSpec((1, nx, dv), lambda bi, h: (bi, 0, h)),
                  pl.BlockSpec((1, nx, dv), lambda bi, h: (bi, 0, v_blk + h)),
                  pl.BlockSpec((1, nc, dv), lambda bi, h: (bi, 0, h)),
                  pl.BlockSpec((1, nc, dv), lambda bi, h: (bi, 0, v_blk + h)),
                  pl.BlockSpec((dv, 1), lambda bi, h: (0, 0))],
        out_specs=pl.BlockSpec((1, nq, dv), lambda bi, h: (bi, 0, h)),
        scratch_shapes=[pltpu.VMEM((n_keys, dv), BF16), pltpu.VMEM((dv + ATTN_ONES_ROWS, n_keys), BF16)]
        + [pltpu.VMEM((tk, rows), F32)] * 3 + [pltpu.VMEM((tk, rows), BF16)] * 3
        + [pltpu.VMEM((1, rows), F32)] * 4 + [pltpu.VMEM((dv + ATTN_ONES_ROWS, rows), F32)],
        compiler_params=_params(("arbitrary", "arbitrary"), pipelined, resident),
        name="diff_attention" if with_x else "diff_attention_ctx",
    )(scal, q, kx, proj_x, kc, proj_c, g_sub.reshape(dv, 1))


CONV_HALO = 16
CONV_ROW_CHUNK = 64


def _conv_kernel(a_ref, g_ref, ap_ref, gp_ref, an_ref, gn_ref, w_ref, cb_ref, lg_ref, lb_ref,
                 o_ref, ext_ref, cv_ref, sh_ref):
    i = pl.program_id(1)
    last = pl.num_programs(1) - 1
    tm = a_ref.shape[1]
    ch = a_ref.shape[2]
    width = w_ref.shape[0]
    pad = width // 2

    def glu(a, g):
        return a * _sigmoid(g)

    ext_ref[0:CONV_HALO, :] = jnp.where(i > 0, glu(ap_ref[0], gp_ref[0]), 0.0)
    ext_ref[CONV_HALO:CONV_HALO + tm, :] = glu(a_ref[0], g_ref[0])
    ext_ref[CONV_HALO + tm:CONV_HALO + tm + CONV_HALO, :] = jnp.where(i < last, glu(an_ref[0], gn_ref[0]), 0.0)

    span = sh_ref.shape[1]
    for c in range(ch // LANES):
        sl = slice(c * LANES, (c + 1) * LANES)
        for s in range(SUBLANES):
            sh_ref[s] = ext_ref[s:s + span, sl]
        def rows(i, carry, sl=sl):
            r0 = pl.multiple_of(i * CONV_ROW_CHUNK, CONV_ROW_CHUNK)
            acc = jnp.zeros((CONV_ROW_CHUNK, LANES), F32)
            for j in range(width):
                start = CONV_HALO - pad + j
                base = start - start % SUBLANES
                acc = acc + w_ref[j:j + 1, sl] * sh_ref[start % SUBLANES, pl.ds(base + r0, CONV_ROW_CHUNK), :]
            cv_ref[pl.ds(r0, CONV_ROW_CHUNK), sl] = acc + cb_ref[:, sl]
            return carry

        lax.fori_loop(0, tm // CONV_ROW_CHUNK, rows, 0)

    v = cv_ref[...]
    mu = jnp.mean(v, axis=-1, keepdims=True)
    vc = v - mu
    var = jnp.mean(vc * vc, axis=-1, keepdims=True)
    y = vc * lax.rsqrt(var + EPS) * lg_ref[...] + lb_ref[...]
    o_ref[0] = (y * _sigmoid(y)).astype(BF16)


def _conformer_conv(proj, conv_w, conv_b, ln_g, ln_b, cols):
    b, n, _ = proj.shape
    width, ch = conv_w.shape
    tm = min(n, 256)
    assert n % tm == 0 and tm % CONV_HALO == 0 and width // 2 < CONV_HALO
    a_blk = cols["conv"] // ch
    g_blk = a_blk + 1
    r = tm // CONV_HALO
    n_halo = n // CONV_HALO

    def cur(blk):
        return pl.BlockSpec((1, tm, ch), lambda bi, i: (bi, i, blk))

    def prev(blk):
        return pl.BlockSpec((1, CONV_HALO, ch), lambda bi, i: (bi, jnp.maximum(i * r - 1, 0), blk))

    def nxt(blk):
        return pl.BlockSpec((1, CONV_HALO, ch), lambda bi, i: (bi, jnp.minimum((i + 1) * r, n_halo - 1), blk))

    vec = pl.BlockSpec((1, ch), lambda bi, i: (0, 0))
    return pl.pallas_call(
        _conv_kernel,
        out_shape=jax.ShapeDtypeStruct((b, n, ch), BF16),
        grid=(b, n // tm),
        in_specs=[cur(a_blk), cur(g_blk), prev(a_blk), prev(g_blk), nxt(a_blk), nxt(g_blk),
                  pl.BlockSpec((width, ch), lambda bi, i: (0, 0)), vec, vec, vec],
        out_specs=pl.BlockSpec((1, tm, ch), lambda bi, i: (bi, i, 0)),
        scratch_shapes=[pltpu.VMEM((tm + 2 * CONV_HALO, ch), F32), pltpu.VMEM((tm, ch), F32),
                        pltpu.VMEM((SUBLANES, tm + 2 * CONV_HALO - SUBLANES, LANES), F32)],
        compiler_params=_params(("arbitrary", "arbitrary"),
                                (2 * tm + 4 * CONV_HALO) * ch * 4 + tm * ch * 2 + 40 * ch * 4,
                                (2 * tm + 2 * CONV_HALO) * ch * 4 + 4 * tm * ch * 4
                                + SUBLANES * (tm + 2 * CONV_HALO) * LANES * 4),
        name="conformer_conv",
    )(proj, proj, proj, proj, proj, proj, conv_w, conv_b.reshape(1, ch), ln_g.reshape(1, ch),
      ln_b.reshape(1, ch))


def _ret_kernel(qf_ref, kf_ref, vf_ref, gf_ref, qb_ref, kb_ref, vb_ref, gb_ref, s0_ref,
                inner_ref, xi_ref, zeta_ref, gc_ref, gng_ref, gnb_ref,
                of_ref, ob_ref, sfin_ref, s_ref, *, kdim):
    i = pl.program_id(1)
    c = qf_ref.shape[1]
    dv = gng_ref.shape[1]
    n_pair = qf_ref.shape[2] // LANES

    @pl.when(i == 0)
    def _():
        s_ref[...] = s0_ref[0]

    lo = _lane_iota((c, LANES)) < kdim
    dirs = ((qf_ref, kf_ref, vf_ref, gf_ref, of_ref), (qb_ref, kb_ref, vb_ref, gb_ref, ob_ref))
    for d, (q_ref, k_ref, v_ref, g_ref, o_ref) in enumerate(dirs):
        for p in range(n_pair):
            sl = slice(p * LANES, (p + 1) * LANES)
            q = q_ref[0, :, sl]
            k = k_ref[0, :, sl]
            kz = k * zeta_ref[d, p]
            kb16 = k.astype(BF16)
            s_pair = s_ref[d, p]
            s16 = s_pair.astype(BF16)
            upd = jnp.zeros((LANES, dv), F32)
            for hh in range(2):
                h = 2 * p + hh
                hsl = slice(h * dv, (h + 1) * dv)
                keep = lo if hh == 0 else jnp.logical_not(lo)
                qh = jnp.where(keep, q, 0.0).astype(BF16)
                v16 = v_ref[0, :, hsl].astype(BF16)
                att = lax.dot_general(qh, kb16, (((1,), (1,)), ((), ())), preferred_element_type=F32)
                att = (att * inner_ref[d, h]).astype(BF16)
                o = (jnp.dot(att, v16, preferred_element_type=F32)
                     + jnp.dot(qh, s16, preferred_element_type=F32) * xi_ref[d, h])
                kzh = jnp.where(keep, kz, 0.0).astype(BF16)
                upd = upd + lax.dot_general(kzh, v16, (((0,), (0,)), ((), ())), preferred_element_type=F32)
                mu = jnp.mean(o, axis=-1, keepdims=True)
                oc = o - mu
                var = jnp.mean(oc * oc, axis=-1, keepdims=True)
                y = oc * lax.rsqrt(var + EPS) * gng_ref[...] + gnb_ref[...]
                g = g_ref[0, :, hsl]
                o_ref[0, :, hsl] = (g * _sigmoid(g)) * y
            s_ref[d, p] = s_pair * gc_ref[p] + upd

    sfin_ref[0] = s_ref[...]


def _retention(rq, rk, proj, s0, tabs, gn_g, gn_b, cols):
    b, n, rw = rq.shape
    dv = gn_g.shape[0]
    d = cols["d"]
    c = RET_CHUNK
    nch = n // c
    kdim = dv // 2
    assert n % c == 0 and c == LANES and 2 * kdim == LANES
    n_pair = rw // LANES
    inner, xi, zeta, gc = tabs
    v_blk, gf_blk, gb_blk = cols["rv"] // d, cols["rgf"] // d, cols["rgb"] // d

    def fwd(width, blk):
        return pl.BlockSpec((1, c, width), lambda bi, i: (bi, i, blk))

    def bwd(width, blk):
        return pl.BlockSpec((1, c, width), lambda bi, i: (bi, nch - 1 - i, blk))

    def whole(a):
        return pl.BlockSpec(a.shape, lambda bi, i: (0,) * a.ndim)

    state = pl.BlockSpec((1, 2, n_pair, LANES, dv), lambda bi, i: (bi, 0, 0, 0, 0))
    vec = pl.BlockSpec((1, dv), lambda bi, i: (0, 0))
    kern = functools.partial(_ret_kernel, kdim=kdim)
    pipelined = 2 * c * (2 * rw + 2 * d) * 4 + 2 * c * d * 4 + 2 * 2 * n_pair * LANES * dv * 4
    resident = (inner.size + xi.size + zeta.size + gc.size) * 4 * 2 + 2 * n_pair * LANES * dv * 4 + 8 * MIB
    return pl.pallas_call(
        kern,
        out_shape=(jax.ShapeDtypeStruct((b, n, d), F32), jax.ShapeDtypeStruct((b, n, d), F32),
                   jax.ShapeDtypeStruct((b, 2, n_pair, LANES, dv), F32)),
        grid=(b, nch),
        in_specs=[fwd(rw, 0), fwd(rw, 0), fwd(d, v_blk), fwd(d, gf_blk),
                  bwd(rw, 0), bwd(rw, 0), bwd(d, v_blk), bwd(d, gb_blk),
                  state, whole(inner), whole(xi), whole(zeta), whole(gc), vec, vec],
        out_specs=(pl.BlockSpec((1, c, d), lambda bi, i: (bi, i, 0)),
                   pl.BlockSpec((1, c, d), lambda bi, i: (bi, nch - 1 - i, 0)),
                   state),
        scratch_shapes=[pltpu.VMEM((2, n_pair, LANES, dv), F32)],
        compiler_params=_params(("arbitrary", "arbitrary"), pipelined, resident),
        name="retention",
    )(rq, rk, proj, proj, rq, rk, proj, proj, s0, inner, xi, zeta, gc,
      gn_g.reshape(1, dv), gn_b.reshape(1, dv))


def _retention_tables(heads, kdim, dv):
    c = RET_CHUNK
    lg = jnp.log1p(-jnp.exp2(-5.0 - jnp.arange(heads, dtype=F32)))
    pos = jnp.arange(c, dtype=F32)
    diff = pos[:, None] - pos[None, :]
    inner_f = jnp.where(diff[None] >= 0, jnp.exp(jnp.maximum(diff, 0.0)[None] * lg[:, None, None]), 0.0)
    inner = jnp.stack([inner_f, jnp.swapaxes(inner_f, 1, 2)])
    xi_f = jnp.exp((pos + 1.0)[None, :] * lg[:, None])
    zeta_f = jnp.exp((c - 1.0 - pos)[None, :] * lg[:, None])
    xi = jnp.stack([xi_f, xi_f[:, ::-1]])
    zeta = jnp.stack([zeta_f, zeta_f[:, ::-1]])
    xi = jnp.broadcast_to(xi[..., None], (2, heads, c, dv))
    zeta = jnp.broadcast_to(zeta[..., None], (2, heads, c, kdim))
    zeta = zeta.reshape(2, heads // 2, 2, c, kdim).transpose(0, 1, 3, 2, 4).reshape(2, heads // 2, c, 2 * kdim)
    gc = jnp.exp(c * lg)
    gc = jnp.broadcast_to(gc[:, None, None], (heads, kdim, dv)).reshape(heads // 2, 2 * kdim, dv)
    return inner.astype(F32), xi.astype(F32), zeta.astype(F32), gc.astype(F32)


def _merge_kernel(h_ref, mod_ref, oa_ref, ob_ref, of_ref, obk_ref, gate_ref, bg_ref,
                  wa_ref, wb_ref, wc_ref, wo_ref, o_ref):
    d = h_ref.shape[2]
    g = _sigmoid(gate_ref[0] + bg_ref[...])
    oc = (of_ref[0] + obk_ref[0]).astype(BF16)
    y = g[:, 0:d] * jnp.dot(oa_ref[0], wa_ref[...], preferred_element_type=F32)
    y = y + g[:, d:2 * d] * jnp.dot(ob_ref[0], wb_ref[...], preferred_element_type=F32)
    y = y + g[:, 2 * d:3 * d] * jnp.dot(oc, wc_ref[...], preferred_element_type=F32)
    m = jnp.dot(y.astype(BF16), wo_ref[...], preferred_element_type=F32)
    o_ref[0] = h_ref[0] + mod_ref[0, 2:3, :] * m


def _merge(h, mod, oa, ob, of, obk, proj, b_gate, w_pa, w_pb, w_pc, w_out, layer, cols):
    b, n, d = h.shape
    tm = min(n, 512)
    assert n % tm == 0
    mb = mod.shape[0]
    gate_blk = cols["gate"] // (3 * d)
    row = lambda width: pl.BlockSpec((1, tm, width), lambda bi, i: (bi, i, 0))
    wspec = pl.BlockSpec((None, d, d), lambda bi, i: (layer, 0, 0))
    return pl.pallas_call(
        _merge_kernel,
        out_shape=jax.ShapeDtypeStruct((b, n, d), F32),
        grid=(b, n // tm),
        in_specs=[row(d), pl.BlockSpec((1, N_MOD, d), lambda bi, i: (bi % mb, 0, 0)),
                  row(d), row(d), row(d), row(d),
                  pl.BlockSpec((1, tm, 3 * d), lambda bi, i: (bi, i, gate_blk)),
                  pl.BlockSpec((1, 3 * d), lambda bi, i: (0, 0)),
                  wspec, wspec, wspec, wspec],
        out_specs=row(d),
        compiler_params=_params(("arbitrary", "arbitrary"),
                                tm * d * (4 + 2 + 2 + 4 + 4 + 12 + 4) + 4 * d * d * 2, 10 * tm * d * 4),
        name="merge",
    )(h, mod, oa, ob, of, obk, proj, b_gate.reshape(1, 3 * d), w_pa, w_pb, w_pc, w_out)


ROUTE_E, ROUTE_G, ROUTE_R = 0, 4, 8


def _route_kernel(h_ref, mod_ref, g_ref, wr_ref, br_ref, tri_ref, base_ref,
                  hx_ref, route_ref, cnt_ref, run_ref, *, n_exp):
    first = jnp.logical_and(pl.program_id(0) == 0, pl.program_id(1) == 0)

    @pl.when(first)
    def _():
        run_ref[...] = base_ref[...]

    h = h_ref[0]
    tm = h.shape[0]
    y = h * lax.rsqrt(jnp.mean(h * h, axis=-1, keepdims=True) + EPS) * g_ref[...]
    hx = y * (1.0 + mod_ref[0, 4:5, :]) + mod_ref[0, 3:4, :]
    hx_ref[0] = hx

    lane = _lane_iota((tm, LANES))
    lane_f = lane.astype(F32)
    hx_hi = hx.astype(BF16)
    hx_lo = (hx - hx_hi.astype(F32)).astype(BF16)
    logits = (jnp.dot(hx_hi, wr_ref[0], preferred_element_type=F32)
              + jnp.dot(hx_hi, wr_ref[1], preferred_element_type=F32)
              + jnp.dot(hx_lo, wr_ref[0], preferred_element_type=F32)) + br_ref[...]
    logits = jnp.where(lane < n_exp, logits, NEG_INF)

    sels, vals = [], []
    for _ in range(TOP_K):
        mx = jnp.max(logits, axis=-1, keepdims=True)
        idx = jnp.min(jnp.where(logits == mx, lane_f, float(LANES)), axis=-1, keepdims=True)
        sel = lane_f == idx
        sels.append((sel, idx))
        vals.append(mx)
        logits = jnp.where(sel, NEG_INF, logits)

    es = [jnp.exp(v - vals[0]) for v in vals]
    denom = es[0]
    for e in es[1:]:
        denom = denom + e

    onehot = jnp.zeros((tm, LANES), F32)
    for sel, _ in sels:
        onehot = onehot + jnp.where(sel, 1.0, 0.0)
    before = run_ref[...] + jnp.dot(tri_ref[...], onehot.astype(BF16), preferred_element_type=F32)

    route = jnp.zeros((tm, LANES), F32)
    for k, (sel, idx) in enumerate(sels):
        rank = jnp.sum(jnp.where(sel, before, 0.0), axis=-1, keepdims=True)
        route = jnp.where(lane == ROUTE_E + k, idx, route)
        route = jnp.where(lane == ROUTE_G + k, es[k] / denom, route)
        route = jnp.where(lane == ROUTE_R + k, rank, route)
    route_ref[0] = route

    run_ref[...] = run_ref[...] + jnp.sum(onehot, axis=0, keepdims=True)
    cnt_ref[...] = run_ref[...]


def _route(h, mod, g, w_router, b_router, base):
    b, n, d = h.shape
    n_exp = w_router.shape[-1]
    tm = min(n, 256)
    assert n % tm == 0 and n_exp <= LANES
    mb = mod.shape[0]
    wr = jnp.zeros((d, LANES), F32).at[:, :n_exp].set(w_router)
    wr_hi = wr.astype(BF16)
    wr = jnp.stack([wr_hi, (wr - wr_hi.astype(F32)).astype(BF16)])
    br = jnp.zeros((1, LANES), F32).at[0, :n_exp].set(b_router)
    tri = (jnp.arange(tm)[:, None] > jnp.arange(tm)[None, :]).astype(BF16)
    row = lambda width: pl.BlockSpec((1, tm, width), lambda bi, i: (bi, i, 0))
    const = lambda shape: pl.BlockSpec(shape, lambda bi, i: (0,) * len(shape))
    kern = functools.partial(_route_kernel, n_exp=n_exp)
    return pl.pallas_call(
        kern,
        out_shape=(jax.ShapeDtypeStruct((b, n, d), F32), jax.ShapeDtypeStruct((b, n, LANES), F32),
                   jax.ShapeDtypeStruct((1, LANES), F32)),
        grid=(b, n // tm),
        in_specs=[row(d), pl.BlockSpec((1, N_MOD, d), lambda bi, i: (bi % mb, 0, 0)), const((1, d)),
                  const((2, d, LANES)), const((1, LANES)), const((tm, tm)), const((1, LANES))],
        out_specs=(row(d), row(LANES), const((1, LANES))),
        scratch_shapes=[pltpu.VMEM((1, LANES), F32)],
        compiler_params=_params(("arbitrary", "arbitrary"),
                                2 * tm * d * 4 + tm * LANES * 4 + d * LANES * 4 + tm * tm * 2, 8 * tm * d * 4),
        name="moe_route",
    )(h, mod, g.reshape(1, d), wr, br, tri, base)


DMA_ISSUE_UNROLL = 8


def _tile_dest(dest, tm):
    t = dest.shape[0]
    return dest.reshape(t // tm, tm, TOP_K).transpose(0, 2, 1).reshape(t // tm, 1, TOP_K * tm)


def _dispatch_kernel(dest_ref, hx_ref, xs_in_ref, xs_ref, sem):
    del xs_in_ref
    tm = hx_ref.shape[0]

    def copy(k, t):
        return pltpu.make_async_copy(hx_ref.at[pl.ds(t, 1)],
                                     xs_ref.at[pl.ds(dest_ref[0, 0, k * tm + t], 1)], sem)

    for t in range(tm):
        for k in range(TOP_K):
            copy(k, t).start(priority=(t * TOP_K + k) % 2)
    for k in range(TOP_K):
        def wait(t, c, k=k):
            copy(k, t).wait()
            return c

        lax.fori_loop(0, tm, wait, 0, unroll=DMA_ISSUE_UNROLL)


def _dispatch(hx_flat, dest, xs):
    t, d = hx_flat.shape
    tm = min(t, 256)
    assert t % tm == 0
    dest3 = _tile_dest(dest, tm)
    return pl.pallas_call(
        _dispatch_kernel,
        out_shape=jax.ShapeDtypeStruct(xs.shape, xs.dtype),
        grid=(t // tm,),
        in_specs=[pl.BlockSpec((1, 1, tm * TOP_K), lambda i: (i, 0, 0), memory_space=pltpu.SMEM),
                  pl.BlockSpec((tm, d), lambda i: (i, 0)),
                  pl.BlockSpec(memory_space=pl.ANY)],
        out_specs=pl.BlockSpec(memory_space=pl.ANY),
        scratch_shapes=[pltpu.SemaphoreType.DMA],
        input_output_aliases={2: 0},
        compiler_params=pltpu.CompilerParams(dimension_semantics=("arbitrary",),
                                             vmem_limit_bytes=_vmem_limit(tm * d * 4, 4 * MIB),
                                             has_side_effects=True),
        name="moe_dispatch",
    )(dest3, hx_flat, xs)


def _split_kernel(w_ref, sel_ref, g_ref, u_ref):
    blk = sel_ref.shape[0]
    half = blk // 2
    for i in range(w_ref.shape[1] // blk):
        r = jnp.dot(w_ref[:, i * blk:(i + 1) * blk].astype(BF16), sel_ref[...], preferred_element_type=F32)
        g_ref[:, i * half:(i + 1) * half] = r[:, :half].astype(BF16)
        u_ref[:, i * half:(i + 1) * half] = r[:, half:].astype(BF16)


def _split_gate_up(w_gate_up):
    depth, n_exp, d, de2 = w_gate_up.shape
    de = de2 // 2
    tr = min(d, 512)
    assert d % tr == 0
    blk = 2 * LANES
    assert de2 % blk == 0
    order = jnp.concatenate([jnp.arange(0, blk, 2), jnp.arange(1, blk, 2)])
    sel = (jnp.arange(blk)[:, None] == order[None, :]).astype(BF16)
    out = jax.ShapeDtypeStruct((depth, n_exp, d, de), BF16)
    ospec = pl.BlockSpec((None, None, tr, de), lambda l, e, r: (l, e, r, 0))
    return pl.pallas_call(
        _split_kernel,
        out_shape=(out, out),
        grid=(depth, n_exp, d // tr),
        in_specs=[pl.BlockSpec((None, None, tr, de2), lambda l, e, r: (l, e, r, 0)),
                  pl.BlockSpec((blk, blk), lambda l, e, r: (0, 0))],
        out_specs=(ospec, ospec),
        compiler_params=_params(("arbitrary", "arbitrary", "arbitrary"),
                                tr * de2 * 4 + blk * blk * 2 + 2 * tr * de * 2, tr * de2 * (2 + 4 + 4)),
        name="split_gate_up",
    )(w_gate_up, sel)


def _expert_kernel(blk_e_ref, n_used_ref, x_ref, wg_ref, wu_ref, wd_ref, bg_ref, bu_ref, bd_ref, y_ref):
    del blk_e_ref
    used = pl.program_id(0) < n_used_ref[0]

    @pl.when(used)
    def _():
        x = x_ref[...].astype(BF16)
        gl = jnp.dot(x, wg_ref[...], preferred_element_type=F32) + bg_ref[...]
        up = jnp.dot(x, wu_ref[...], preferred_element_type=F32) + bu_ref[...]
        gl = jnp.minimum(gl, SWIGLU_LIMIT)
        up = jnp.clip(up, -SWIGLU_LIMIT, SWIGLU_LIMIT)
        act = (up + 1.0) * (gl * _sigmoid(SWIGLU_ALPHA * gl))
        y_ref[...] = jnp.dot(act.astype(BF16), wd_ref[...], preferred_element_type=F32) + bd_ref[...]

    @pl.when(jnp.logical_not(used))
    def _():
        y_ref[...] = jnp.zeros(y_ref.shape, F32)


def _experts(blk_e, n_used, xs, wg, wu, wd, bg, bu, bd, layer, bm):
    n_rows, d = xs.shape
    de = wg.shape[-1]
    nb = n_rows // bm
    wspec = lambda a, c: pl.BlockSpec((None, None, a, c), lambda i, be, nu: (layer, be[i], 0, 0))
    return pl.pallas_call(
        _expert_kernel,
        out_shape=jax.ShapeDtypeStruct((n_rows, d), F32),
        grid_spec=pltpu.PrefetchScalarGridSpec(
            num_scalar_prefetch=2,
            grid=(nb,),
            in_specs=[pl.BlockSpec((bm, d), lambda i, be, nu: (i, 0)),
                      wspec(d, de), wspec(d, de), wspec(de, d),
                      wspec(1, de), wspec(1, de), wspec(1, d)],
            out_specs=pl.BlockSpec((bm, d), lambda i, be, nu: (i, 0))),
        compiler_params=_params(("arbitrary",), 2 * bm * d * 4 + 3 * d * de * 2, 8 * bm * de * 4),
        name="moe_experts",
    )(blk_e, n_used, xs, wg, wu, wd, bg, bu, bd)


def _combine_kernel(dest_ref, dnext_ref, h_ref, mod_ref, route_ref, y_ref, o_ref, ybuf0, ybuf1, sem):
    tm = h_ref.shape[1]
    n_copy = TOP_K * tm
    step = pl.program_id(0) * pl.num_programs(1) + pl.program_id(1)
    n_step = pl.num_programs(0) * pl.num_programs(1)
    bufs = (ybuf0, ybuf1)

    def copy(idx_ref, r, slot):
        return pltpu.make_async_copy(y_ref.at[pl.ds(idx_ref[0, 0, r], 1)], bufs[slot].at[pl.ds(r, 1)],
                                     sem.at[slot])

    def fetch(idx_ref, slot):
        for r in range(n_copy):
            copy(idx_ref, r, slot).start(priority=r % 2)

    def finish(slot):
        def wait(r, c):
            copy(dest_ref, r, slot).wait()
            return c

        lax.fori_loop(0, n_copy, wait, 0, unroll=DMA_ISSUE_UNROLL)
        route = route_ref[0]
        f = jnp.zeros((tm, h_ref.shape[2]), F32)
        for k in range(TOP_K):
            f = f + route[:, ROUTE_G + k:ROUTE_G + k + 1] * bufs[slot][k * tm:(k + 1) * tm, :]
        o_ref[0] = h_ref[0] + mod_ref[0, 5:6, :] * f

    @pl.when(step == 0)
    def _():
        fetch(dest_ref, 0)

    for slot in range(2):
        @pl.when(step % 2 == slot)
        def _(slot=slot):
            @pl.when(step + 1 < n_step)
            def _():
                fetch(dnext_ref, 1 - slot)

            finish(slot)


def _combine(h, mod, route, dest, y):
    b, n, d = h.shape
    tm = min(n, 256)
    assert n % tm == 0
    mb = mod.shape[0]
    nt = n // tm
    dest3 = _tile_dest(dest, tm)
    row = lambda width: pl.BlockSpec((1, tm, width), lambda bi, i: (bi, i, 0))
    return pl.pallas_call(
        _combine_kernel,
        out_shape=jax.ShapeDtypeStruct((b, n, d), F32),
        grid=(b, nt),
        in_specs=[pl.BlockSpec((1, 1, tm * TOP_K), lambda bi, i: (bi * nt + i, 0, 0), memory_space=pltpu.SMEM),
                  pl.BlockSpec((1, 1, tm * TOP_K), lambda bi, i: (jnp.minimum(bi * nt + i + 1, b * nt - 1), 0, 0),
                               memory_space=pltpu.SMEM),
                  row(d), pl.BlockSpec((1, N_MOD, d), lambda bi, i: (bi % mb, 0, 0)), row(LANES),
                  pl.BlockSpec(memory_space=pl.ANY)],
        out_specs=row(d),
        scratch_shapes=[pltpu.VMEM((TOP_K * tm, d), F32), pltpu.VMEM((TOP_K * tm, d), F32),
                        pltpu.SemaphoreType.DMA((2,))],
        compiler_params=_params(("arbitrary", "arbitrary"), 2 * tm * d * 4 + tm * LANES * 4,
                                2 * TOP_K * tm * d * 4 + 4 * tm * d * 4),
        name="moe_combine",
    )(dest3, dest3, h, mod, route, y)


def _moe_rows(n_tok, n_exp, bm):
    return -(-(n_tok * TOP_K + n_exp * (bm - 1)) // bm) * bm


def _moe(parts, xs, g, w_router, b_router, wg, wu, wd, bg, bu, bd, layer, bm):
    n_exp = w_router.shape[-1]
    n_rows, d = xs.shape
    base = jnp.zeros((1, LANES), F32)
    hxs, routes = [], []
    for h, mod in parts:
        hx, route, base = _route(h, mod, g, w_router, b_router, base)
        hxs.append(hx)
        routes.append(route)
    counts = base[0, :n_exp].astype(jnp.int32)

    padded = (counts + bm - 1) // bm * bm
    pend = jnp.cumsum(padded)
    pstart = pend - padded
    n_tok = sum(h.shape[0] * h.shape[1] for h, _ in parts)
    assert n_rows >= _moe_rows(n_tok, n_exp, bm) and n_rows % bm == 0
    nb = n_rows // bm
    blk_start = jnp.arange(nb, dtype=jnp.int32) * bm
    blk_e = jnp.minimum(jnp.sum((pend[None, :] <= blk_start[:, None]).astype(jnp.int32), axis=1), n_exp - 1)
    n_used = (pend[-1] // bm).astype(jnp.int32).reshape(1)
    eids = jnp.arange(n_exp, dtype=jnp.int32)

    dests = []
    for hx, route in zip(hxs, routes):
        e = route[..., ROUTE_E:ROUTE_E + TOP_K].astype(jnp.int32)
        rank = route[..., ROUTE_R:ROUTE_R + TOP_K].astype(jnp.int32)
        start = jnp.sum(jnp.where(e[..., None] == eids, pstart, 0), axis=-1)
        dest = (start + rank).reshape(-1, TOP_K)
        dests.append(dest)
        xs = _dispatch(hx.reshape(-1, d), dest, xs)

    y = _experts(blk_e, n_used, xs, wg, wu, wd, bg, bu, bd, layer, bm)
    return [_combine(h, mod, route, dest, y)
            for (h, mod), route, dest in zip(parts, routes, dests)], xs


def kernel(x, c, ctx, c_ctx, w_mod, b_mod, g_norm1, g_norm2, w_in, b_gate, g_qa, g_ka, lam_q1, lam_k1, lam_q2, lam_k2, g_sub, conv_w, conv_b, ln_g, ln_b, gn_g, gn_b, w_pa, w_pb, w_pc, w_out, w_router, b_router, w_gate_up, b_gate_up, w_down, b_down):
    bsz, n_x, d = x.shape
    n_c = ctx.shape[1]
    depth = w_mod.shape[0]
    da_dim = g_qa.shape[-1]
    dv = g_sub.shape[-1]
    heads = d // dv
    ret_kdim = gn_g.shape[-1] // 2
    n_exp = w_router.shape[-1]
    de = w_down.shape[-2]
    moe_bm = 512

    cols = {"d": d, "ret_qk": heads * ret_kdim}
    off = 0
    for name, width in (("aq", d), ("ak", d), ("av", d), ("conv", 2 * d), ("rq", heads * ret_kdim),
                        ("rk", heads * ret_kdim), ("rv", d), ("rgf", d), ("rgb", d), ("gate", 3 * d)):
        cols[name] = off
        off += width
    assert off == w_in.shape[-1]

    w_in_bf = w_in.astype(BF16)
    w_pa_bf, w_pb_bf, w_pc_bf, w_out_bf = (w.astype(BF16) for w in (w_pa, w_pb, w_pc, w_out))
    wg, wu = _split_gate_up(w_gate_up)
    wd = w_down.astype(BF16)
    bg = b_gate_up[..., 0::2].reshape(depth, n_exp, 1, de)
    bu = b_gate_up[..., 1::2].reshape(depth, n_exp, 1, de)
    bd = b_down.reshape(depth, n_exp, 1, d)

    rows = -(-(bsz + 1) // SUBLANES) * SUBLANES
    cvecs = jnp.zeros((rows, d), F32).at[:bsz].set(c).at[bsz].set(c_ctx)
    mods = _modulation(cvecs, w_mod, b_mod).reshape(depth, rows, N_MOD, d)

    tabs_x = _rope_tables(n_x, da_dim, ret_kdim)
    tabs_c = tuple(jnp.zeros((n_c, LANES), F32) for _ in range(4))
    ret_tabs = _retention_tables(heads, ret_kdim, dv)
    s_zero = jnp.zeros((bsz, 2, heads // 2, LANES, dv), F32)
    xs_rows = jnp.zeros((_moe_rows(bsz * (n_c + n_x), n_exp, moe_bm), d), F32)

    h_ctx = ctx
    for l in range(depth):
        need_ctx = l < depth - 1
        lam_init = 0.8 - 0.6 * math.exp(-0.3 * l)
        lam = (jnp.exp(jnp.sum(lam_q1[l] * lam_k1[l])) - jnp.exp(jnp.sum(lam_q2[l] * lam_k2[l])) + lam_init)
        scal = jnp.stack([lam, jnp.asarray(1.0 - lam_init, F32)]).astype(F32)
        mod_x = mods[l, :bsz]
        mod_c = mods[l, bsz:bsz + 1]

        proj_x = _in_proj(x, mod_x, g_norm1[l], w_in_bf, l)
        proj_c = _in_proj(h_ctx, mod_c, g_norm1[l], w_in_bf, l)
        aq_x, ak_x, rq_x, rk_x = _qk_prep(proj_x, g_qa[l], g_ka[l], tabs_x, cols, rope=True,
                                          da_dim=da_dim, ret_kdim=ret_kdim)
        aq_c, ak_c, rq_c, rk_c = _qk_prep(proj_c, g_qa[l], g_ka[l], tabs_c, cols, rope=False,
                                          da_dim=da_dim, ret_kdim=ret_kdim)
        oa_x = _attention(scal, aq_x, ak_x, proj_x, ak_c, proj_c, g_sub[l], cols, with_x=True, da_dim=da_dim)
        ob_x = _conformer_conv(proj_x, conv_w[l], conv_b[l], ln_g[l], ln_b[l], cols)
        of_c, ob_c, s_ctx = _retention(rq_c, rk_c, proj_c, s_zero, ret_tabs, gn_g[l], gn_b[l], cols)
        of_x, obk_x, _ = _retention(rq_x, rk_x, proj_x, s_ctx, ret_tabs, gn_g[l], gn_b[l], cols)
        x = _merge(x, mod_x, oa_x, ob_x, of_x, obk_x, proj_x, b_gate[l],
                   w_pa_bf, w_pb_bf, w_pc_bf, w_out_bf, l, cols)
        if need_ctx:
            oa_c = _attention(scal, aq_c, ak_c, proj_c, ak_c, proj_c, g_sub[l], cols, with_x=False, da_dim=da_dim)
            cb_c = _conformer_conv(proj_c, conv_w[l], conv_b[l], ln_g[l], ln_b[l], cols)
            h_ctx = _merge(h_ctx, mod_c, oa_c, cb_c, of_c, ob_c, proj_c, b_gate[l],
                           w_pa_bf, w_pb_bf, w_pc_bf, w_out_bf, l, cols)

        parts = [(h_ctx, mod_c), (x, mod_x)] if need_ctx else [(x, mod_x)]
        outs, xs_rows = _moe(parts, xs_rows, g_norm2[l], w_router[l], b_router[l], wg, wu, wd, bg, bu, bd,
                             l, moe_bm)
        if need_ctx:
            h_ctx, x = outs
        else:
            (x,) = outs
    return x
```

```python
import functools
import math

import jax
import jax.numpy as jnp
from jax import lax
from jax.experimental import pallas as pl
from jax.experimental.pallas import tpu as pltpu

F32 = jnp.float32
BF16 = jnp.bfloat16

GRID_W = 64
N_MOD = 6
EPS = 1e-6
ROPE_BASE = 10000.0
RET_CHUNK = 128
TOP_K = 4
SWIGLU_ALPHA = 1.702
SWIGLU_LIMIT = 7.0

LANES = 128
SUBLANES = 8
V7X_VMEM_BYTES = 64 * 2**20
VMEM_LIMIT_CAP = 56 * 2**20
MIB = 2**20

NEG_INF = float("-inf")
LOG2_E = math.log2(math.e)


def _sigmoid(x):
    return 1.0 / (1.0 + jnp.exp(-x))


def _vmem_limit(pipelined_bytes, resident_bytes):
    need = 2 * pipelined_bytes + resident_bytes
    return int(min(max(need, 16 * MIB), VMEM_LIMIT_CAP))


def _params(sem, pipelined_bytes, resident_bytes):
    return pltpu.CompilerParams(dimension_semantics=sem,
                                vmem_limit_bytes=_vmem_limit(pipelined_bytes, resident_bytes))


def _lane_iota(shape):
    return lax.broadcasted_iota(jnp.int32, shape, len(shape) - 1)


def _mod_kernel(c_ref, w_ref, b_ref, o_ref):
    c = c_ref[...]
    a = (c * _sigmoid(c)).astype(BF16)
    o_ref[...] = jnp.dot(a, w_ref[...].astype(BF16), preferred_element_type=F32) + b_ref[...]


def _modulation(cvecs, w_mod, b_mod):
    depth, d, cols = w_mod.shape
    rows = cvecs.shape[0]
    tn = 1536
    assert cols % tn == 0
    return pl.pallas_call(
        _mod_kernel,
        out_shape=jax.ShapeDtypeStruct((depth, rows, cols), F32),
        grid=(depth, cols // tn),
        in_specs=[pl.BlockSpec((rows, d), lambda l, j: (0, 0)),
                  pl.BlockSpec((None, d, tn), lambda l, j: (l, 0, j)),
                  pl.BlockSpec((None, 1, tn), lambda l, j: (l, 0, j))],
        out_specs=pl.BlockSpec((None, rows, tn), lambda l, j: (l, 0, j)),
        compiler_params=_params(("arbitrary", "arbitrary"), d * tn * 4 + rows * tn * 4, 4 * MIB),
        name="modulation",
    )(cvecs, w_mod, b_mod.reshape(depth, 1, cols))


def _inproj_kernel(h_ref, mod_ref, g_ref, w_ref, o_ref, xs_ref):
    @pl.when(pl.program_id(2) == 0)
    def _():
        h = h_ref[0]
        y = h * lax.rsqrt(jnp.mean(h * h, axis=-1, keepdims=True) + EPS) * g_ref[...]
        xs_ref[...] = (y * (1.0 + mod_ref[0, 1:2, :]) + mod_ref[0, 0:1, :]).astype(BF16)

    o_ref[0] = jnp.dot(xs_ref[...], w_ref[...], preferred_element_type=F32)


def _in_proj(h, mod, g, w_in_bf, layer):
    b, n, d = h.shape
    cols = w_in_bf.shape[-1]
    tm = min(n, 1024)
    tn = 2048
    assert n % tm == 0 and cols % tn == 0
    mb = mod.shape[0]
    return pl.pallas_call(
        _inproj_kernel,
        out_shape=jax.ShapeDtypeStruct((b, n, cols), F32),
        grid=(b, n // tm, cols // tn),
        in_specs=[pl.BlockSpec((1, tm, d), lambda bi, i, j: (bi, i, 0)),
                  pl.BlockSpec((1, N_MOD, d), lambda bi, i, j: (bi % mb, 0, 0)),
                  pl.BlockSpec((1, d), lambda bi, i, j: (0, 0)),
                  pl.BlockSpec((None, d, tn), lambda bi, i, j: (layer, 0, j))],
        out_specs=pl.BlockSpec((1, tm, tn), lambda bi, i, j: (bi, i, j)),
        scratch_shapes=[pltpu.VMEM((tm, d), BF16)],
        compiler_params=_params(("arbitrary", "arbitrary", "arbitrary"),
                                tm * d * 4 + d * tn * 2 + tm * tn * 4, tm * d * 2 + 3 * tm * d * 4),
        name="in_proj",
    )(h, mod, g.reshape(1, d), w_in_bf)


def _prep_kernel(aq_ref, ak_ref, rq_ref, rk_ref, gq_ref, gk_ref, ca_ref, sa_ref, cr_ref, sr_ref,
                 oq_ref, ok_ref, orq_ref, ork_ref, *, rope, da_dim, ret_kdim):
    tm = aq_ref.shape[1]
    lane = _lane_iota((tm, LANES))
    lo = lane < da_dim

    def rms(x, g):
        x2 = x * x
        s_lo = jnp.sum(jnp.where(lo, x2, 0.0), axis=-1, keepdims=True)
        s_hi = jnp.sum(jnp.where(lo, 0.0, x2), axis=-1, keepdims=True)
        ms = jnp.where(lo, s_lo, s_hi) * (1.0 / da_dim)
        return x * lax.rsqrt(ms + EPS) * g

    def rot(x, c, s, half):
        first = (lane % (2 * half)) < half
        partner = jnp.where(first, pltpu.roll(x, LANES - half, 1), pltpu.roll(x, half, 1))
        return x * c + partner * s

    q_scale = da_dim ** -0.5 * LOG2_E
    k_scale = ret_kdim ** -0.5
    for j in range(aq_ref.shape[2] // LANES):
        sl = slice(j * LANES, (j + 1) * LANES)
        q = rms(aq_ref[0, :, sl], gq_ref[...])
        k = rms(ak_ref[0, :, sl], gk_ref[...])
        if rope:
            q = rot(q, ca_ref[...], sa_ref[...], da_dim // 4)
            k = rot(k, ca_ref[...], sa_ref[...], da_dim // 4)
        oq_ref[0, :, sl] = (q * q_scale).astype(BF16)
        ok_ref[0, :, sl] = k.astype(BF16)
    for j in range(rq_ref.shape[2] // LANES):
        sl = slice(j * LANES, (j + 1) * LANES)
        q = rq_ref[0, :, sl]
        k = rk_ref[0, :, sl]
        if rope:
            q = rot(q, cr_ref[...], sr_ref[...], ret_kdim // 2)
            k = rot(k, cr_ref[...], sr_ref[...], ret_kdim // 2)
        orq_ref[0, :, sl] = q
        ork_ref[0, :, sl] = k * k_scale


def _qk_prep(proj, g_q, g_k, tabs, cols, *, rope, da_dim, ret_kdim):
    b, n, _ = proj.shape
    d = cols["d"]
    rw = cols["ret_qk"]
    tm = min(n, 1024)
    assert n % tm == 0
    gq = jnp.tile(g_q, LANES // da_dim).reshape(1, LANES)
    gk = jnp.tile(g_k, LANES // da_dim).reshape(1, LANES)
    tab_spec = pl.BlockSpec((tm, LANES), lambda bi, i: (i, 0))
    vec_spec = pl.BlockSpec((1, LANES), lambda bi, i: (0, 0))

    def col_spec(width, off):
        assert off % width == 0
        return pl.BlockSpec((1, tm, width), lambda bi, i: (bi, i, off // width))

    kern = functools.partial(_prep_kernel, rope=rope, da_dim=da_dim, ret_kdim=ret_kdim)
    return pl.pallas_call(
        kern,
        out_shape=(jax.ShapeDtypeStruct((b, n, d), BF16), jax.ShapeDtypeStruct((b, n, d), BF16),
                   jax.ShapeDtypeStruct((b, n, rw), F32), jax.ShapeDtypeStruct((b, n, rw), F32)),
        grid=(b, n // tm),
        in_specs=[col_spec(d, cols["aq"]), col_spec(d, cols["ak"]),
                  col_spec(rw, cols["rq"]), col_spec(rw, cols["rk"]),
                  vec_spec, vec_spec, tab_spec, tab_spec, tab_spec, tab_spec],
        out_specs=(pl.BlockSpec((1, tm, d), lambda bi, i: (bi, i, 0)),
                   pl.BlockSpec((1, tm, d), lambda bi, i: (bi, i, 0)),
                   pl.BlockSpec((1, tm, rw), lambda bi, i: (bi, i, 0)),
                   pl.BlockSpec((1, tm, rw), lambda bi, i: (bi, i, 0))),
        compiler_params=_params(("arbitrary", "arbitrary"),
                                tm * (2 * d + 2 * rw) * 4 + tm * (2 * d * 2 + 2 * rw * 4)
                                + 4 * tm * LANES * 4, 8 * MIB),
        name="qk_prep",
    )(proj, proj, proj, proj, gq, gk, *tabs)


def _rope_tables(n_tok, da_dim, ret_kdim):
    t = jnp.arange(n_tok)

    def angles(pos, dim):
        inv = ROPE_BASE ** (-jnp.arange(0, dim, 2, dtype=F32) / dim)
        return pos.astype(F32)[:, None] * inv[None, :]

    def cs(ang):
        c, s = jnp.cos(ang), jnp.sin(ang)
        return jnp.concatenate([c, c], axis=-1), jnp.concatenate([-s, s], axis=-1)

    half = da_dim // 2
    cr_, sr_ = cs(angles(t // GRID_W, half))
    cc_, sc_ = cs(angles(t % GRID_W, half))
    ca = jnp.tile(jnp.concatenate([cr_, cc_], axis=-1), (1, LANES // da_dim))
    sa = jnp.tile(jnp.concatenate([sr_, sc_], axis=-1), (1, LANES // da_dim))
    c1, s1 = cs(angles(t, ret_kdim))
    cr = jnp.tile(c1, (1, LANES // ret_kdim))
    sr = jnp.tile(s1, (1, LANES // ret_kdim))
    return ca, sa, cr, sr


ATTN_MAX_KEY_TILE = 768
ATTN_ONES_ROWS = 16


def _attn_kernel(scal_ref, q_ref, kx_ref, vx_ref, kc_ref, vc_ref, gsub_ref, o_ref,
                 k_all, v_all, s0_ref, s1_ref, s2_ref, p0_ref, p1_ref, p2_ref, a0_ref, a1_ref, a2_ref,
                 m_ref, acc_ref, *, tq, tk, with_x, da_dim):
    dv = vc_ref.shape[2]
    nc = kc_ref.shape[1]
    nx = kx_ref.shape[1] if with_x else 0
    n_steps = (nx + nc) // tk
    n_tiles = q_ref.shape[1] // tq
    rows = 2 * tq

    k_all[nx:nx + nc, :] = kc_ref[0]
    v_all[0:dv, nx:nx + nc] = vc_ref[0].T.astype(BF16)
    v_all[dv:, :] = jnp.ones((v_all.shape[0] - dv, nx + nc), BF16)
    if with_x:
        chunk = min(512, nx)

        def fill(j, c):
            off = pl.multiple_of(j * chunk, chunk)
            k_all[pl.ds(off, chunk), :] = kx_ref[0, pl.ds(off, chunk), :]
            v_all[0:dv, pl.ds(off, chunk)] = vx_ref[0, pl.ds(off, chunk), :].T.astype(BF16)
            return c

        lax.fori_loop(0, nx // chunk, fill, 0)

    s_bufs, p_bufs, a_bufs = (s0_ref, s1_ref, s2_ref), (p0_ref, p1_ref, p2_ref), (a0_ref, a1_ref, a2_ref)
    lo = _lane_iota((tq, LANES)) < da_dim

    def buf(j):
        return 2 if (n_steps % 2 == 1 and n_steps > 1 and j == n_steps - 1) else j % 2

    def q_start(t):
        return t * tq if isinstance(t, int) else pl.multiple_of(t * tq, tq)

    def qk(t, j):
        q = q_ref[0, pl.ds(q_start(t), tq), :]
        zero = jnp.zeros_like(q)
        q2 = jnp.concatenate([jnp.where(lo, q, zero), jnp.where(lo, zero, q)], axis=0)
        s_bufs[buf(j)][...] = lax.dot_general(k_all[j * tk:(j + 1) * tk, :], q2,
                                              (((1,), (1,)), ((), ())), preferred_element_type=F32)

    def softmax(j):
        b = buf(j)
        s = s_bufs[b][...]
        col_max = jnp.max(s, axis=0, keepdims=True)
        if j == 0:
            m_new = col_max
        else:
            m_prev = m_ref[...]
            m_new = jnp.maximum(m_prev, col_max)
            a_bufs[b][...] = jnp.exp2(m_prev - m_new)
        p_bufs[b][...] = jnp.exp2(s - m_new).astype(BF16)
        m_ref[...] = m_new

    def pv(j):
        b = buf(j)
        r = jnp.dot(v_all[:, j * tk:(j + 1) * tk], p_bufs[b][...], preferred_element_type=F32)
        if j == 0:
            acc_ref[...] = r
        else:
            acc_ref[...] = a_bufs[b][...] * acc_ref[...] + r

    def finalize(t):
        acc = acc_ref[...]
        o0 = acc[0:dv, 0:tq] / acc[dv:dv + 1, 0:tq]
        o1 = acc[0:dv, tq:2 * tq] / acc[dv:dv + 1, tq:2 * tq]
        o = o0 - scal_ref[0] * o1
        y = o * lax.rsqrt(jnp.mean(o * o, axis=0, keepdims=True) + EPS) * gsub_ref[...]
        o_ref[0, pl.ds(q_start(t), tq), :] = (y * scal_ref[1]).T.astype(BF16)

    if n_steps == 1:
        def lone(t, c):
            qk(t, 0)
            softmax(0)
            pv(0)
            finalize(t)
            return c

        lax.fori_loop(0, n_tiles, lone, 0)
    else:
        qk(0, 0)
        last = buf(n_steps - 1)
        p_bufs[last][...] = jnp.zeros(p_bufs[last].shape, BF16)
        a_bufs[last][...] = jnp.ones(a_bufs[last].shape, F32)
        acc_ref[...] = jnp.ones(acc_ref.shape, F32)

        def tile(t, c):
            qk(t, 1)
            softmax(0)
            pv(n_steps - 1)
            finalize(jnp.maximum(t - 1, 0))
            for j in range(1, n_steps - 1):
                qk(t, j + 1)
                softmax(j)
                pv(j - 1)
            qk(jnp.minimum(t + 1, n_tiles - 1), 0)
            softmax(n_steps - 1)
            pv(n_steps - 2)
            return c

        lax.fori_loop(0, n_tiles, tile, 0)
        pv(n_steps - 1)
        finalize(n_tiles - 1)


def _attention(scal, q, kx, proj_x, kc, proj_c, g_sub, cols, *, with_x, da_dim):
    b, nq, d = q.shape
    dv = g_sub.shape[0]
    heads = d // dv
    nc = kc.shape[1]
    nx = kx.shape[1]
    tq = min(nq, 256)
    n_keys = (nx if with_x else 0) + nc
    tk = max(t for t in range(LANES, ATTN_MAX_KEY_TILE + 1, LANES) if n_keys % t == 0)
    assert nq % tq == 0 and 2 * da_dim == LANES and dv == LANES
    v_blk = cols["av"] // dv
    kern = functools.partial(_attn_kernel, tq=tq, tk=tk, with_x=with_x, da_dim=da_dim)
    rows = 2 * tq
    pipelined = 2 * nq * dv * 2 + nx * dv * (2 + 4) + nc * dv * (2 + 4)
    resident = (n_keys * 3 * dv * 2 + 3 * rows * tk * (4 + 2) + 4 * rows * LANES * 4
                + rows * 2 * dv * 4 + 3 * rows * tk * 4)
    return pl.pallas_call(
        kern,
        out_shape=jax.ShapeDtypeStruct((b, nq, d), BF16),
        grid=(b, heads),
        in_specs=[pl.BlockSpec(memory_space=pltpu.SMEM),
                  pl.BlockSpec((1, nq, dv), lambda bi, h: (bi, 0, h)),
                  pl.BlockSpec((1, nx, dv), lambda bi, h: (bi, 0, h)),
                  pl.BlockSpec((1, nx, dv), lambda bi, h: (bi, 0, v_blk + h)),
                  pl.BlockSpec((1, nc, dv), lambda bi, h: (bi, 0, h)),
                  pl.BlockSpec((1, nc, dv), lambda bi, h: (bi, 0, v_blk + h)),
                  pl.BlockSpec((dv, 1), lambda bi, h: (0, 0))],
        out_specs=pl.BlockSpec((1, nq, dv), lambda bi, h: (bi, 0, h)),
        scratch_shapes=[pltpu.VMEM((n_keys, dv), BF16), pltpu.VMEM((dv + ATTN_ONES_ROWS, n_keys), BF16)]
        + [pltpu.VMEM((tk, rows), F32)] * 3 + [pltpu.VMEM((tk, rows), BF16)] * 3
        + [pltpu.VMEM((1, rows), F32)] * 4 + [pltpu.VMEM((dv + ATTN_ONES_ROWS, rows), F32)],
        compiler_params=_params(("arbitrary", "arbitrary"), pipelined, resident),
        name="diff_attention" if with_x else "diff_attention_ctx",
    )(scal, q, kx, proj_x, kc, proj_c, g_sub.reshape(dv, 1))


CONV_HALO = 16
CONV_ROW_CHUNK = 64


def _conv_kernel(a_ref, g_ref, ap_ref, gp_ref, an_ref, gn_ref, w_ref, cb_ref, lg_ref, lb_ref,
                 o_ref, ext_ref, cv_ref, sh_ref):
    i = pl.program_id(1)
    last = pl.num_programs(1) - 1
    tm = a_ref.shape[1]
    ch = a_ref.shape[2]
    width = w_ref.shape[0]
    pad = width // 2

    def glu(a, g):
        return a * _sigmoid(g)

    ext_ref[0:CONV_HALO, :] = jnp.where(i > 0, glu(ap_ref[0], gp_ref[0]), 0.0)
    ext_ref[CONV_HALO:CONV_HALO + tm, :] = glu(a_ref[0], g_ref[0])
    ext_ref[CONV_HALO + tm:CONV_HALO + tm + CONV_HALO, :] = jnp.where(i < last, glu(an_ref[0], gn_ref[0]), 0.0)

    span = sh_ref.shape[1]
    for c in range(ch // LANES):
        sl = slice(c * LANES, (c + 1) * LANES)
        for s in range(SUBLANES):
            sh_ref[s] = ext_ref[s:s + span, sl]
        def rows(i, carry, sl=sl):
            r0 = pl.multiple_of(i * CONV_ROW_CHUNK, CONV_ROW_CHUNK)
            acc = jnp.zeros((CONV_ROW_CHUNK, LANES), F32)
            for j in range(width):
                start = CONV_HALO - pad + j
                base = start - start % SUBLANES
                acc = acc + w_ref[j:j + 1, sl] * sh_ref[start % SUBLANES, pl.ds(base + r0, CONV_ROW_CHUNK), :]
            cv_ref[pl.ds(r0, CONV_ROW_CHUNK), sl] = acc + cb_ref[:, sl]
            return carry

        lax.fori_loop(0, tm // CONV_ROW_CHUNK, rows, 0)

    v = cv_ref[...]
    mu = jnp.mean(v, axis=-1, keepdims=True)
    vc = v - mu
    var = jnp.mean(vc * vc, axis=-1, keepdims=True)
    y = vc * lax.rsqrt(var + EPS) * lg_ref[...] + lb_ref[...]
    o_ref[0] = (y * _sigmoid(y)).astype(BF16)


def _conformer_conv(proj, conv_w, conv_b, ln_g, ln_b, cols):
    b, n, _ = proj.shape
    width, ch = conv_w.shape
    tm = min(n, 256)
    assert n % tm == 0 and tm % CONV_HALO == 0 and width // 2 < CONV_HALO
    a_blk = cols["conv"] // ch
    g_blk = a_blk + 1
    r = tm // CONV_HALO
    n_halo = n // CONV_HALO

    def cur(blk):
        return pl.BlockSpec((1, tm, ch), lambda bi, i: (bi, i, blk))

    def prev(blk):
        return pl.BlockSpec((1, CONV_HALO, ch), lambda bi, i: (bi, jnp.maximum(i * r - 1, 0), blk))

    def nxt(blk):
        return pl.BlockSpec((1, CONV_HALO, ch), lambda bi, i: (bi, jnp.minimum((i + 1) * r, n_halo - 1), blk))

    vec = pl.BlockSpec((1, ch), lambda bi, i: (0, 0))
    return pl.pallas_call(
        _conv_kernel,
        out_shape=jax.ShapeDtypeStruct((b, n, ch), BF16),
        grid=(b, n // tm),
        in_specs=[cur(a_blk), cur(g_blk), prev(a_blk), prev(g_blk), nxt(a_blk), nxt(g_blk),
                  pl.BlockSpec((width, ch), lambda bi, i: (0, 0)), vec, vec, vec],
        out_specs=pl.BlockSpec((1, tm, ch), lambda bi, i: (bi, i, 0)),
        scratch_shapes=[pltpu.VMEM((tm + 2 * CONV_HALO, ch), F32), pltpu.VMEM((tm, ch), F32),
                        pltpu.VMEM((SUBLANES, tm + 2 * CONV_HALO - SUBLANES, LANES), F32)],
        compiler_params=_params(("arbitrary", "arbitrary"),
                                (2 * tm + 4 * CONV_HALO) * ch * 4 + tm * ch * 2 + 40 * ch * 4,
                                (2 * tm + 2 * CONV_HALO) * ch * 4 + 4 * tm * ch * 4
                                + SUBLANES * (tm + 2 * CONV_HALO) * LANES * 4),
        name="conformer_conv",
    )(proj, proj, proj, proj, proj, proj, conv_w, conv_b.reshape(1, ch), ln_g.reshape(1, ch),
      ln_b.reshape(1, ch))


def _ret_kernel(qf_ref, kf_ref, vf_ref, gf_ref, qb_ref, kb_ref, vb_ref, gb_ref, s0_ref,
                inner_ref, xi_ref, zeta_ref, gc_ref, gng_ref, gnb_ref,
                of_ref, ob_ref, sfin_ref, s_ref, *, kdim):
    i = pl.program_id(1)
    c = qf_ref.shape[1]
    dv = gng_ref.shape[1]
    n_pair = qf_ref.shape[2] // LANES

    @pl.when(i == 0)
    def _():
        s_ref[...] = s0_ref[0]

    lo = _lane_iota((c, LANES)) < kdim
    dirs = ((qf_ref, kf_ref, vf_ref, gf_ref, of_ref), (qb_ref, kb_ref, vb_ref, gb_ref, ob_ref))
    for d, (q_ref, k_ref, v_ref, g_ref, o_ref) in enumerate(dirs):
        for p in range(n_pair):
            sl = slice(p * LANES, (p + 1) * LANES)
            q = q_ref[0, :, sl]
            k = k_ref[0, :, sl]
            kz = k * zeta_ref[d, p]
            kb16 = k.astype(BF16)
            s_pair = s_ref[d, p]
            s16 = s_pair.astype(BF16)
            upd = jnp.zeros((LANES, dv), F32)
            for hh in range(2):
                h = 2 * p + hh
                hsl = slice(h * dv, (h + 1) * dv)
                keep = lo if hh == 0 else jnp.logical_not(lo)
                qh = jnp.where(keep, q, 0.0).astype(BF16)
                v16 = v_ref[0, :, hsl].astype(BF16)
                att = lax.dot_general(qh, kb16, (((1,), (1,)), ((), ())), preferred_element_type=F32)
                att = (att * inner_ref[d, h]).astype(BF16)
                o = (jnp.dot(att, v16, preferred_element_type=F32)
                     + jnp.dot(qh, s16, preferred_element_type=F32) * xi_ref[d, h])
                kzh = jnp.where(keep, kz, 0.0).astype(BF16)
                upd = upd + lax.dot_general(kzh, v16, (((0,), (0,)), ((), ())), preferred_element_type=F32)
                mu = jnp.mean(o, axis=-1, keepdims=True)
                oc = o - mu
                var = jnp.mean(oc * oc, axis=-1, keepdims=True)
                y = oc * lax.rsqrt(var + EPS) * gng_ref[...] + gnb_ref[...]
                g = g_ref[0, :, hsl]
                o_ref[0, :, hsl] = (g * _sigmoid(g)) * y
            s_ref[d, p] = s_pair * gc_ref[p] + upd

    sfin_ref[0] = s_ref[...]


def _retention(rq, rk, proj, s0, tabs, gn_g, gn_b, cols):
    b, n, rw = rq.shape
    dv = gn_g.shape[0]
    d = cols["d"]
    c = RET_CHUNK
    nch = n // c
    kdim = dv // 2
    assert n % c == 0 and c == LANES and 2 * kdim == LANES
    n_pair = rw // LANES
    inner, xi, zeta, gc = tabs
    v_blk, gf_blk, gb_blk = cols["rv"] // d, cols["rgf"] // d, cols["rgb"] // d

    def fwd(width, blk):
        return pl.BlockSpec((1, c, width), lambda bi, i: (bi, i, blk))

    def bwd(width, blk):
        return pl.BlockSpec((1, c, width), lambda bi, i: (bi, nch - 1 - i, blk))

    def whole(a):
        return pl.BlockSpec(a.shape, lambda bi, i: (0,) * a.ndim)

    state = pl.BlockSpec((1, 2, n_pair, LANES, dv), lambda bi, i: (bi, 0, 0, 0, 0))
    vec = pl.BlockSpec((1, dv), lambda bi, i: (0, 0))
    kern = functools.partial(_ret_kernel, kdim=kdim)
    pipelined = 2 * c * (2 * rw + 2 * d) * 4 + 2 * c * d * 4 + 2 * 2 * n_pair * LANES * dv * 4
    resident = (inner.size + xi.size + zeta.size + gc.size) * 4 * 2 + 2 * n_pair * LANES * dv * 4 + 8 * MIB
    return pl.pallas_call(
        kern,
        out_shape=(jax.ShapeDtypeStruct((b, n, d), F32), jax.ShapeDtypeStruct((b, n, d), F32),
                   jax.ShapeDtypeStruct((b, 2, n_pair, LANES, dv), F32)),
        grid=(b, nch),
        in_specs=[fwd(rw, 0), fwd(rw, 0), fwd(d, v_blk), fwd(d, gf_blk),
                  bwd(rw, 0), bwd(rw, 0), bwd(d, v_blk), bwd(d, gb_blk),
                  state, whole(inner), whole(xi), whole(zeta), whole(gc), vec, vec],
        out_specs=(pl.BlockSpec((1, c, d), lambda bi, i: (bi, i, 0)),
                   pl.BlockSpec((1, c, d), lambda bi, i: (bi, nch - 1 - i, 0)),
                   state),
        scratch_shapes=[pltpu.VMEM((2, n_pair, LANES, dv), F32)],
        compiler_params=_params(("arbitrary", "arbitrary"), pipelined, resident),
        name="retention",
    )(rq, rk, proj, proj, rq, rk, proj, proj, s0, inner, xi, zeta, gc,
      gn_g.reshape(1, dv), gn_b.reshape(1, dv))


def _retention_tables(heads, kdim, dv):
    c = RET_CHUNK
    lg = jnp.log1p(-jnp.exp2(-5.0 - jnp.arange(heads, dtype=F32)))
    pos = jnp.arange(c, dtype=F32)
    diff = pos[:, None] - pos[None, :]
    inner_f = jnp.where(diff[None] >= 0, jnp.exp(jnp.maximum(diff, 0.0)[None] * lg[:, None, None]), 0.0)
    inner = jnp.stack([inner_f, jnp.swapaxes(inner_f, 1, 2)])
    xi_f = jnp.exp((pos + 1.0)[None, :] * lg[:, None])
    zeta_f = jnp.exp((c - 1.0 - pos)[None, :] * lg[:, None])
    xi = jnp.stack([xi_f, xi_f[:, ::-1]])
    zeta = jnp.stack([zeta_f, zeta_f[:, ::-1]])
    xi = jnp.broadcast_to(xi[..., None], (2, heads, c, dv))
    zeta = jnp.broadcast_to(zeta[..., None], (2, heads, c, kdim))
    zeta = zeta.reshape(2, heads // 2, 2, c, kdim).transpose(0, 1, 3, 2, 4).reshape(2, heads // 2, c, 2 * kdim)
    gc = jnp.exp(c * lg)
    gc = jnp.broadcast_to(gc[:, None, None], (heads, kdim, dv)).reshape(heads // 2, 2 * kdim, dv)
    return inner.astype(F32), xi.astype(F32), zeta.astype(F32), gc.astype(F32)


def _merge_kernel(h_ref, mod_ref, oa_ref, ob_ref, of_ref, obk_ref, gate_ref, bg_ref,
                  wa_ref, wb_ref, wc_ref, wo_ref, o_ref):
    d = h_ref.shape[2]
    g = _sigmoid(gate_ref[0] + bg_ref[...])
    oc = (of_ref[0] + obk_ref[0]).astype(BF16)
    y = g[:, 0:d] * jnp.dot(oa_ref[0], wa_ref[...], preferred_element_type=F32)
    y = y + g[:, d:2 * d] * jnp.dot(ob_ref[0], wb_ref[...], preferred_element_type=F32)
    y = y + g[:, 2 * d:3 * d] * jnp.dot(oc, wc_ref[...], preferred_element_type=F32)
    m = jnp.dot(y.astype(BF16), wo_ref[...], preferred_element_type=F32)
    o_ref[0] = h_ref[0] + mod_ref[0, 2:3, :] * m


def _merge(h, mod, oa, ob, of, obk, proj, b_gate, w_pa, w_pb, w_pc, w_out, layer, cols):
    b, n, d = h.shape
    tm = min(n, 512)
    assert n % tm == 0
    mb = mod.shape[0]
    gate_blk = cols["gate"] // (3 * d)
    row = lambda width: pl.BlockSpec((1, tm, width), lambda bi, i: (bi, i, 0))
    wspec = pl.BlockSpec((None, d, d), lambda bi, i: (layer, 0, 0))
    return pl.pallas_call(
        _merge_kernel,
        out_shape=jax.ShapeDtypeStruct((b, n, d), F32),
        grid=(b, n // tm),
        in_specs=[row(d), pl.BlockSpec((1, N_MOD, d), lambda bi, i: (bi % mb, 0, 0)),
                  row(d), row(d), row(d), row(d),
                  pl.BlockSpec((1, tm, 3 * d), lambda bi, i: (bi, i, gate_blk)),
                  pl.BlockSpec((1, 3 * d), lambda bi, i: (0, 0)),
                  wspec, wspec, wspec, wspec],
        out_specs=row(d),
        compiler_params=_params(("arbitrary", "arbitrary"),
                                tm * d * (4 + 2 + 2 + 4 + 4 + 12 + 4) + 4 * d * d * 2, 10 * tm * d * 4),
        name="merge",
    )(h, mod, oa, ob, of, obk, proj, b_gate.reshape(1, 3 * d), w_pa, w_pb, w_pc, w_out)


ROUTE_E, ROUTE_G, ROUTE_R = 0, 4, 8


def _route_kernel(h_ref, mod_ref, g_ref, wr_ref, br_ref, tri_ref, base_ref,
                  hx_ref, route_ref, cnt_ref, run_ref, *, n_exp):
    first = jnp.logical_and(pl.program_id(0) == 0, pl.program_id(1) == 0)

    @pl.when(first)
    def _():
        run_ref[...] = base_ref[...]

    h = h_ref[0]
    tm = h.shape[0]
    y = h * lax.rsqrt(jnp.mean(h * h, axis=-1, keepdims=True) + EPS) * g_ref[...]
    hx = y * (1.0 + mod_ref[0, 4:5, :]) + mod_ref[0, 3:4, :]
    hx_ref[0] = hx

    lane = _lane_iota((tm, LANES))
    lane_f = lane.astype(F32)
    hx_hi = hx.astype(BF16)
    hx_lo = (hx - hx_hi.astype(F32)).astype(BF16)
    logits = (jnp.dot(hx_hi, wr_ref[0], preferred_element_type=F32)
              + jnp.dot(hx_hi, wr_ref[1], preferred_element_type=F32)
              + jnp.dot(hx_lo, wr_ref[0], preferred_element_type=F32)) + br_ref[...]
    logits = jnp.where(lane < n_exp, logits, NEG_INF)

    sels, vals = [], []
    for _ in range(TOP_K):
        mx = jnp.max(logits, axis=-1, keepdims=True)
        idx = jnp.min(jnp.where(logits == mx, lane_f, float(LANES)), axis=-1, keepdims=True)
        sel = lane_f == idx
        sels.append((sel, idx))
        vals.append(mx)
        logits = jnp.where(sel, NEG_INF, logits)

    es = [jnp.exp(v - vals[0]) for v in vals]
    denom = es[0]
    for e in es[1:]:
        denom = denom + e

    onehot = jnp.zeros((tm, LANES), F32)
    for sel, _ in sels:
        onehot = onehot + jnp.where(sel, 1.0, 0.0)
    before = run_ref[...] + jnp.dot(tri_ref[...], onehot.astype(BF16), preferred_element_type=F32)

    route = jnp.zeros((tm, LANES), F32)
    for k, (sel, idx) in enumerate(sels):
        rank = jnp.sum(jnp.where(sel, before, 0.0), axis=-1, keepdims=True)
        route = jnp.where(lane == ROUTE_E + k, idx, route)
        route = jnp.where(lane == ROUTE_G + k, es[k] / denom, route)
        route = jnp.where(lane == ROUTE_R + k, rank, route)
    route_ref[0] = route

    run_ref[...] = run_ref[...] + jnp.sum(onehot, axis=0, keepdims=True)
    cnt_ref[...] = run_ref[...]


def _route(h, mod, g, w_router, b_router, base):
    b, n, d = h.shape
    n_exp = w_router.shape[-1]
    tm = min(n, 256)
    assert n % tm == 0 and n_exp <= LANES
    mb = mod.shape[0]
    wr = jnp.zeros((d, LANES), F32).at[:, :n_exp].set(w_router)
    wr_hi = wr.astype(BF16)
    wr = jnp.stack([wr_hi, (wr - wr_hi.astype(F32)).astype(BF16)])
    br = jnp.zeros((1, LANES), F32).at[0, :n_exp].set(b_router)
    tri = (jnp.arange(tm)[:, None] > jnp.arange(tm)[None, :]).astype(BF16)
    row = lambda width: pl.BlockSpec((1, tm, width), lambda bi, i: (bi, i, 0))
    const = lambda shape: pl.BlockSpec(shape, lambda bi, i: (0,) * len(shape))
    kern = functools.partial(_route_kernel, n_exp=n_exp)
    return pl.pallas_call(
        kern,
        out_shape=(jax.ShapeDtypeStruct((b, n, d), F32), jax.ShapeDtypeStruct((b, n, LANES), F32),
                   jax.ShapeDtypeStruct((1, LANES), F32)),
        grid=(b, n // tm),
        in_specs=[row(d), pl.BlockSpec((1, N_MOD, d), lambda bi, i: (bi % mb, 0, 0)), const((1, d)),
                  const((2, d, LANES)), const((1, LANES)), const((tm, tm)), const((1, LANES))],
        out_specs=(row(d), row(LANES), const((1, LANES))),
        scratch_shapes=[pltpu.VMEM((1, LANES), F32)],
        compiler_params=_params(("arbitrary", "arbitrary"),
                                2 * tm * d * 4 + tm * LANES * 4 + d * LANES * 4 + tm * tm * 2, 8 * tm * d * 4),
        name="moe_route",
    )(h, mod, g.reshape(1, d), wr, br, tri, base)


DMA_ISSUE_UNROLL = 8


def _tile_dest(dest, tm):
    t = dest.shape[0]
    return dest.reshape(t // tm, tm, TOP_K).transpose(0, 2, 1).reshape(t // tm, 1, TOP_K * tm)


def _dispatch_kernel(dest_ref, hx_ref, xs_in_ref, xs_ref, sem):
    del xs_in_ref
    tm = hx_ref.shape[0]

    def copy(k, t):
        return pltpu.make_async_copy(hx_ref.at[pl.ds(t, 1)],
                                     xs_ref.at[pl.ds(dest_ref[0, 0, k * tm + t], 1)], sem)

    for t in range(tm):
        for k in range(TOP_K):
            copy(k, t).start(priority=(t * TOP_K + k) % 2)
    for k in range(TOP_K):
        def wait(t, c, k=k):
            copy(k, t).wait()
            return c

        lax.fori_loop(0, tm, wait, 0, unroll=DMA_ISSUE_UNROLL)


def _dispatch(hx_flat, dest, xs):
    t, d = hx_flat.shape
    tm = min(t, 256)
    assert t % tm == 0
    dest3 = _tile_dest(dest, tm)
    return pl.pallas_call(
        _dispatch_kernel,
        out_shape=jax.ShapeDtypeStruct(xs.shape, xs.dtype),
        grid=(t // tm,),
        in_specs=[pl.BlockSpec((1, 1, tm * TOP_K), lambda i: (i, 0, 0), memory_space=pltpu.SMEM),
                  pl.BlockSpec((tm, d), lambda i: (i, 0)),
                  pl.BlockSpec(memory_space=pl.ANY)],
        out_specs=pl.BlockSpec(memory_space=pl.ANY),
        scratch_shapes=[pltpu.SemaphoreType.DMA],
        input_output_aliases={2: 0},
        compiler_params=pltpu.CompilerParams(dimension_semantics=("arbitrary",),
                                             vmem_limit_bytes=_vmem_limit(tm * d * 4, 4 * MIB),
                                             has_side_effects=True),
        name="moe_dispatch",
    )(dest3, hx_flat, xs)


def _split_kernel(w_ref, sel_ref, g_ref, u_ref):
    blk = sel_ref.shape[0]
    half = blk // 2
    for i in range(w_ref.shape[1] // blk):
        r = jnp.dot(w_ref[:, i * blk:(i + 1) * blk].astype(BF16), sel_ref[...], preferred_element_type=F32)
        g_ref[:, i * half:(i + 1) * half] = r[:, :half].astype(BF16)
        u_ref[:, i * half:(i + 1) * half] = r[:, half:].astype(BF16)


def _split_gate_up(w_gate_up):
    depth, n_exp, d, de2 = w_gate_up.shape
    de = de2 // 2
    tr = min(d, 512)
    assert d % tr == 0
    blk = 2 * LANES
    assert de2 % blk == 0
    order = jnp.concatenate([jnp.arange(0, blk, 2), jnp.arange(1, blk, 2)])
    sel = (jnp.arange(blk)[:, None] == order[None, :]).astype(BF16)
    out = jax.ShapeDtypeStruct((depth, n_exp, d, de), BF16)
    ospec = pl.BlockSpec((None, None, tr, de), lambda l, e, r: (l, e, r, 0))
    return pl.pallas_call(
        _split_kernel,
        out_shape=(out, out),
        grid=(depth, n_exp, d // tr),
        in_specs=[pl.BlockSpec((None, None, tr, de2), lambda l, e, r: (l, e, r, 0)),
                  pl.BlockSpec((blk, blk), lambda l, e, r: (0, 0))],
        out_specs=(ospec, ospec),
        compiler_params=_params(("arbitrary", "arbitrary", "arbitrary"),
                                tr * de2 * 4 + blk * blk * 2 + 2 * tr * de * 2, tr * de2 * (2 + 4 + 4)),
        name="split_gate_up",
    )(w_gate_up, sel)


def _expert_kernel(blk_e_ref, n_used_ref, x_ref, wg_ref, wu_ref, wd_ref, bg_ref, bu_ref, bd_ref, y_ref):
    del blk_e_ref
    used = pl.program_id(0) < n_used_ref[0]

    @pl.when(used)
    def _():
        x = x_ref[...].astype(BF16)
        gl = jnp.dot(x, wg_ref[...], preferred_element_type=F32) + bg_ref[...]
        up = jnp.dot(x, wu_ref[...], preferred_element_type=F32) + bu_ref[...]
        gl = jnp.minimum(gl, SWIGLU_LIMIT)
        up = jnp.clip(up, -SWIGLU_LIMIT, SWIGLU_LIMIT)
        act = (up + 1.0) * (gl * _sigmoid(SWIGLU_ALPHA * gl))
        y_ref[...] = jnp.dot(act.astype(BF16), wd_ref[...], preferred_element_type=F32) + bd_ref[...]

    @pl.when(jnp.logical_not(used))
    def _():
        y_ref[...] = jnp.zeros(y_ref.shape, F32)


def _experts(blk_e, n_used, xs, wg, wu, wd, bg, bu, bd, layer, bm):
    n_rows, d = xs.shape
    de = wg.shape[-1]
    nb = n_rows // bm
    wspec = lambda a, c: pl.BlockSpec((None, None, a, c), lambda i, be, nu: (layer, be[i], 0, 0))
    return pl.pallas_call(
        _expert_kernel,
        out_shape=jax.ShapeDtypeStruct((n_rows, d), F32),
        grid_spec=pltpu.PrefetchScalarGridSpec(
            num_scalar_prefetch=2,
            grid=(nb,),
            in_specs=[pl.BlockSpec((bm, d), lambda i, be, nu: (i, 0)),
                      wspec(d, de), wspec(d, de), wspec(de, d),
                      wspec(1, de), wspec(1, de), wspec(1, d)],
            out_specs=pl.BlockSpec((bm, d), lambda i, be, nu: (i, 0))),
        compiler_params=_params(("arbitrary",), 2 * bm * d * 4 + 3 * d * de * 2, 8 * bm * de * 4),
        name="moe_experts",
    )(blk_e, n_used, xs, wg, wu, wd, bg, bu, bd)


def _combine_kernel(dest_ref, dnext_ref, h_ref, mod_ref, route_ref, y_ref, o_ref, ybuf0, ybuf1, sem):
    tm = h_ref.shape[1]
    n_copy = TOP_K * tm
    step = pl.program_id(0) * pl.num_programs(1) + pl.program_id(1)
    n_step = pl.num_programs(0) * pl.num_programs(1)
    bufs = (ybuf0, ybuf1)

    def copy(idx_ref, r, slot):
        return pltpu.make_async_copy(y_ref.at[pl.ds(idx_ref[0, 0, r], 1)], bufs[slot].at[pl.ds(r, 1)],
                                     sem.at[slot])

    def fetch(idx_ref, slot):
        for r in range(n_copy):
            copy(idx_ref, r, slot).start(priority=r % 2)

    def finish(slot):
        def wait(r, c):
            copy(dest_ref, r, slot).wait()
            return c

        lax.fori_loop(0, n_copy, wait, 0, unroll=DMA_ISSUE_UNROLL)
        route = route_ref[0]
        f = jnp.zeros((tm, h_ref.shape[2]), F32)
        for k in range(TOP_K):
            f = f + route[:, ROUTE_G + k:ROUTE_G + k + 1] * bufs[slot][k * tm:(k + 1) * tm, :]
        o_ref[0] = h_ref[0] + mod_ref[0, 5:6, :] * f

    @pl.when(step == 0)
    def _():
        fetch(dest_ref, 0)

    for slot in range(2):
        @pl.when(step % 2 == slot)
        def _(slot=slot):
            @pl.when(step + 1 < n_step)
            def _():
                fetch(dnext_ref, 1 - slot)

            finish(slot)


def _combine(h, mod, route, dest, y):
    b, n, d = h.shape
    tm = min(n, 256)
    assert n % tm == 0
    mb = mod.shape[0]
    nt = n // tm
    dest3 = _tile_dest(dest, tm)
    row = lambda width: pl.BlockSpec((1, tm, width), lambda bi, i: (bi, i, 0))
    return pl.pallas_call(
        _combine_kernel,
        out_shape=jax.ShapeDtypeStruct((b, n, d), F32),
        grid=(b, nt),
        in_specs=[pl.BlockSpec((1, 1, tm * TOP_K), lambda bi, i: (bi * nt + i, 0, 0), memory_space=pltpu.SMEM),
                  pl.BlockSpec((1, 1, tm * TOP_K), lambda bi, i: (jnp.minimum(bi * nt + i + 1, b * nt - 1), 0, 0),
                               memory_space=pltpu.SMEM),
                  row(d), pl.BlockSpec((1, N_MOD, d), lambda bi, i: (bi % mb, 0, 0)), row(LANES),
                  pl.BlockSpec(memory_space=pl.ANY)],
        out_specs=row(d),
        scratch_shapes=[pltpu.VMEM((TOP_K * tm, d), F32), pltpu.VMEM((TOP_K * tm, d), F32),
                        pltpu.SemaphoreType.DMA((2,))],
        compiler_params=_params(("arbitrary", "arbitrary"), 2 * tm * d * 4 + tm * LANES * 4,
                                2 * TOP_K * tm * d * 4 + 4 * tm * d * 4),
        name="moe_combine",
    )(dest3, dest3, h, mod, route, y)


def _moe_rows(n_tok, n_exp, bm):
    return -(-(n_tok * TOP_K + n_exp * (bm - 1)) // bm) * bm


def _moe(parts, xs, g, w_router, b_router, wg, wu, wd, bg, bu, bd, layer, bm):
    n_exp = w_router.shape[-1]
    n_rows, d = xs.shape
    base = jnp.zeros((1, LANES), F32)
    hxs, routes = [], []
    for h, mod in parts:
        hx, route, base = _route(h, mod, g, w_router, b_router, base)
        hxs.append(hx)
        routes.append(route)
    counts = base[0, :n_exp].astype(jnp.int32)

    padded = (counts + bm - 1) // bm * bm
    pend = jnp.cumsum(padded)
    pstart = pend - padded
    n_tok = sum(h.shape[0] * h.shape[1] for h, _ in parts)
    assert n_rows >= _moe_rows(n_tok, n_exp, bm) and n_rows % bm == 0
    nb = n_rows // bm
    blk_start = jnp.arange(nb, dtype=jnp.int32) * bm
    blk_e = jnp.minimum(jnp.sum((pend[None, :] <= blk_start[:, None]).astype(jnp.int32), axis=1), n_exp - 1)
    n_used = (pend[-1] // bm).astype(jnp.int32).reshape(1)
    eids = jnp.arange(n_exp, dtype=jnp.int32)

    dests = []
    for hx, route in zip(hxs, routes):
        e = route[..., ROUTE_E:ROUTE_E + TOP_K].astype(jnp.int32)
        rank = route[..., ROUTE_R:ROUTE_R + TOP_K].astype(jnp.int32)
        start = jnp.sum(jnp.where(e[..., None] == eids, pstart, 0), axis=-1)
        dest = (start + rank).reshape(-1, TOP_K)
        dests.append(dest)
        xs = _dispatch(hx.reshape(-1, d), dest, xs)

    y = _experts(blk_e, n_used, xs, wg, wu, wd, bg, bu, bd, layer, bm)
    return [_combine(h, mod, route, dest, y)
            for (h, mod), route, dest in zip(parts, routes, dests)], xs


def kernel(x, c, ctx, c_ctx, w_mod, b_mod, g_norm1, g_norm2, w_in, b_gate, g_qa, g_ka, lam_q1, lam_k1, lam_q2, lam_k2, g_sub, conv_w, conv_b, ln_g, ln_b, gn_g, gn_b, w_pa, w_pb, w_pc, w_out, w_router, b_router, w_gate_up, b_gate_up, w_down, b_down):
    bsz, n_x, d = x.shape
    n_c = ctx.shape[1]
    depth = w_mod.shape[0]
    da_dim = g_qa.shape[-1]
    dv = g_sub.shape[-1]
    heads = d // dv
    ret_kdim = gn_g.shape[-1] // 2
    n_exp = w_router.shape[-1]
    de = w_down.shape[-2]
    moe_bm = 512

    cols = {"d": d, "ret_qk": heads * ret_kdim}
    off = 0
    for name, width in (("aq", d), ("ak", d), ("av", d), ("conv", 2 * d), ("rq", heads * ret_kdim),
                        ("rk", heads * ret_kdim), ("rv", d), ("rgf", d), ("rgb", d), ("gate", 3 * d)):
        cols[name] = off
        off += width
    assert off == w_in.shape[-1]

    w_in_bf = w_in.astype(BF16)
    w_pa_bf, w_pb_bf, w_pc_bf, w_out_bf = (w.astype(BF16) for w in (w_pa, w_pb, w_pc, w_out))
    wg, wu = _split_gate_up(w_gate_up)
    wd = w_down.astype(BF16)
    bg = b_gate_up[..., 0::2].reshape(depth, n_exp, 1, de)
    bu = b_gate_up[..., 1::2].reshape(depth, n_exp, 1, de)
    bd = b_down.reshape(depth, n_exp, 1, d)

    rows = -(-(bsz + 1) // SUBLANES) * SUBLANES
    cvecs = jnp.zeros((rows, d), F32).at[:bsz].set(c).at[bsz].set(c_ctx)
    mods = _modulation(cvecs, w_mod, b_mod).reshape(depth, rows, N_MOD, d)

    tabs_x = _rope_tables(n_x, da_dim, ret_kdim)
    tabs_c = tuple(jnp.zeros((n_c, LANES), F32) for _ in range(4))
    ret_tabs = _retention_tables(heads, ret_kdim, dv)
    s_zero = jnp.zeros((bsz, 2, heads // 2, LANES, dv), F32)
    xs_rows = jnp.zeros((_moe_rows(bsz * (n_c + n_x), n_exp, moe_bm), d), F32)

    h_ctx = ctx
    for l in range(depth):
        need_ctx = l < depth - 1
        lam_init = 0.8 - 0.6 * math.exp(-0.3 * l)
        lam = (jnp.exp(jnp.sum(lam_q1[l] * lam_k1[l])) - jnp.exp(jnp.sum(lam_q2[l] * lam_k2[l])) + lam_init)
        scal = jnp.stack([lam, jnp.asarray(1.0 - lam_init, F32)]).astype(F32)
        mod_x = mods[l, :bsz]
        mod_c = mods[l, bsz:bsz + 1]

        proj_x = _in_proj(x, mod_x, g_norm1[l], w_in_bf, l)
        proj_c = _in_proj(h_ctx, mod_c, g_norm1[l], w_in_bf, l)
        aq_x, ak_x, rq_x, rk_x = _qk_prep(proj_x, g_qa[l], g_ka[l], tabs_x, cols, rope=True,
                                          da_dim=da_dim, ret_kdim=ret_kdim)
        aq_c, ak_c, rq_c, rk_c = _qk_prep(proj_c, g_qa[l], g_ka[l], tabs_c, cols, rope=False,
                                          da_dim=da_dim, ret_kdim=ret_kdim)
        oa_x = _attention(scal, aq_x, ak_x, proj_x, ak_c, proj_c, g_sub[l], cols, with_x=True, da_dim=da_dim)
        ob_x = _conformer_conv(proj_x, conv_w[l], conv_b[l], ln_g[l], ln_b[l], cols)
        of_c, ob_c, s_ctx = _retention(rq_c, rk_c, proj_c, s_zero, ret_tabs, gn_g[l], gn_b[l], cols)
        of_x, obk_x, _ = _retention(rq_x, rk_x, proj_x, s_ctx, ret_tabs, gn_g[l], gn_b[l], cols)
        x = _merge(x, mod_x, oa_x, ob_x, of_x, obk_x, proj_x, b_gate[l],
                   w_pa_bf, w_pb_bf, w_pc_bf, w_out_bf, l, cols)
        if need_ctx:
            oa_c = _attention(scal, aq_c, ak_c, proj_c, ak_c, proj_c, g_sub[l], cols, with_x=False, da_dim=da_dim)
            cb_c = _conformer_conv(proj_c, conv_w[l], conv_b[l], ln_g[l], ln_b[l], cols)
            h_ctx = _merge(h_ctx, mod_c, oa_c, cb_c, of_c, ob_c, proj_c, b_gate[l],
                           w_pa_bf, w_pb_bf, w_pc_bf, w_out_bf, l, cols)

        parts = [(h_ctx, mod_c), (x, mod_x)] if need_ctx else [(x, mod_x)]
        outs, xs_rows = _moe(parts, xs_rows, g_norm2[l], w_router[l], b_router[l], wg, wu, wd, bg, bu, bd,
                             l, moe_bm)
        if need_ctx:
            h_ctx, x = outs
        else:
            (x,) = outs
    return x
```

```python
import functools
import math

import jax
import jax.numpy as jnp
from jax import lax
from jax.experimental import pallas as pl
from jax.experimental.pallas import tpu as pltpu

F32 = jnp.float32
BF16 = jnp.bfloat16

GRID_W = 64
N_MOD = 6
EPS = 1e-6
ROPE_BASE = 10000.0
RET_CHUNK = 128
TOP_K = 4
SWIGLU_ALPHA = 1.702
SWIGLU_LIMIT = 7.0

LANES = 128
SUBLANES = 8
MIB = 2**20
V7X_VMEM_BYTES = 64 * MIB
V7X_VMEM_COMPILER_RESERVE = 8 * MIB
VMEM_LIMIT_CAP = V7X_VMEM_BYTES - V7X_VMEM_COMPILER_RESERVE
VMEM_LIMIT_FLOOR = 16 * MIB

NEG_INF = float("-inf")
LOG2_E = math.log2(math.e)


def _sigmoid(x):
    return 1.0 / (1.0 + jnp.exp(-x))


def _vmem_limit(pipelined_bytes, resident_bytes):
    need = 2 * pipelined_bytes + resident_bytes
    return int(min(max(need, VMEM_LIMIT_FLOOR), VMEM_LIMIT_CAP))


def _params(sem, pipelined_bytes, resident_bytes):
    return pltpu.CompilerParams(dimension_semantics=sem,
                                vmem_limit_bytes=_vmem_limit(pipelined_bytes, resident_bytes))


def _lane_iota(shape):
    return lax.broadcasted_iota(jnp.int32, shape, len(shape) - 1)


def _mod_kernel(c_ref, w_ref, b_ref, o_ref):
    c = c_ref[...]
    a = (c * _sigmoid(c)).astype(BF16)
    o_ref[...] = jnp.dot(a, w_ref[...].astype(BF16), preferred_element_type=F32) + b_ref[...]


def _modulation(cvecs, w_mod, b_mod):
    depth, d, cols = w_mod.shape
    rows = cvecs.shape[0]
    tn = 1536
    assert cols % tn == 0
    return pl.pallas_call(
        _mod_kernel,
        out_shape=jax.ShapeDtypeStruct((depth, rows, cols), F32),
        grid=(depth, cols // tn),
        in_specs=[pl.BlockSpec((rows, d), lambda l, j: (0, 0)),
                  pl.BlockSpec((None, d, tn), lambda l, j: (l, 0, j)),
                  pl.BlockSpec((None, 1, tn), lambda l, j: (l, 0, j))],
        out_specs=pl.BlockSpec((None, rows, tn), lambda l, j: (l, 0, j)),
        compiler_params=_params(("arbitrary", "arbitrary"), d * tn * 4 + rows * tn * 4, 4 * MIB),
        name="modulation",
    )(cvecs, w_mod, b_mod.reshape(depth, 1, cols))


def _inproj_kernel(h_ref, mod_ref, g_ref, w_ref, o_ref, xs_ref):
    @pl.when(pl.program_id(2) == 0)
    def _():
        h = h_ref[0]
        y = h * lax.rsqrt(jnp.mean(h * h, axis=-1, keepdims=True) + EPS) * g_ref[...]
        xs_ref[...] = (y * (1.0 + mod_ref[0, 1:2, :]) + mod_ref[0, 0:1, :]).astype(BF16)

    o_ref[0] = jnp.dot(xs_ref[...], w_ref[...], preferred_element_type=F32)


def _in_proj(h, mod, g, w_in_bf, layer):
    b, n, d = h.shape
    cols = w_in_bf.shape[-1]
    tm = min(n, 1024)
    tn = 3072
    assert n % tm == 0 and cols % tn == 0
    mb = mod.shape[0]
    return pl.pallas_call(
        _inproj_kernel,
        out_shape=jax.ShapeDtypeStruct((b, n, cols), F32),
        grid=(b, n // tm, cols // tn),
        in_specs=[pl.BlockSpec((1, tm, d), lambda bi, i, j: (bi, i, 0)),
                  pl.BlockSpec((1, N_MOD, d), lambda bi, i, j: (bi % mb, 0, 0)),
                  pl.BlockSpec((1, d), lambda bi, i, j: (0, 0)),
                  pl.BlockSpec((None, d, tn), lambda bi, i, j: (layer, 0, j))],
        out_specs=pl.BlockSpec((1, tm, tn), lambda bi, i, j: (bi, i, j)),
        scratch_shapes=[pltpu.VMEM((tm, d), BF16)],
        compiler_params=_params(("arbitrary", "arbitrary", "arbitrary"),
                                tm * d * 4 + d * tn * 2 + tm * tn * 4, tm * d * 2 + 3 * tm * d * 4),
        name="in_proj",
    )(h, mod, g.reshape(1, d), w_in_bf)


def _prep_kernel(aq_ref, ak_ref, rq_ref, rk_ref, gq_ref, gk_ref, ca_ref, sa_ref, cr_ref, sr_ref,
                 oq_ref, ok_ref, orq_ref, ork_ref, *, rope, da_dim, ret_kdim):
    tm = aq_ref.shape[1]
    lane = _lane_iota((tm, LANES))
    lo = lane < da_dim

    def rms(x, g):
        x2 = x * x
        s_lo = jnp.sum(jnp.where(lo, x2, 0.0), axis=-1, keepdims=True)
        s_hi = jnp.sum(jnp.where(lo, 0.0, x2), axis=-1, keepdims=True)
        ms = jnp.where(lo, s_lo, s_hi) * (1.0 / da_dim)
        return x * lax.rsqrt(ms + EPS) * g

    def rot(x, c, s, half):
        first = (lane % (2 * half)) < half
        partner = jnp.where(first, pltpu.roll(x, LANES - half, 1), pltpu.roll(x, half, 1))
        return x * c + partner * s

    q_scale = da_dim ** -0.5 * LOG2_E
    k_scale = ret_kdim ** -0.5
    for j in range(aq_ref.shape[2] // LANES):
        sl = slice(j * LANES, (j + 1) * LANES)
        q = rms(aq_ref[0, :, sl], gq_ref[...])
        k = rms(ak_ref[0, :, sl], gk_ref[...])
        if rope:
            q = rot(q, ca_ref[...], sa_ref[...], da_dim // 4)
            k = rot(k, ca_ref[...], sa_ref[...], da_dim // 4)
        oq_ref[0, :, sl] = (q * q_scale).astype(BF16)
        ok_ref[0, :, sl] = k.astype(BF16)
    for j in range(rq_ref.shape[2] // LANES):
        sl = slice(j * LANES, (j + 1) * LANES)
        q = rq_ref[0, :, sl]
        k = rk_ref[0, :, sl]
        if rope:
            q = rot(q, cr_ref[...], sr_ref[...], ret_kdim // 2)
            k = rot(k, cr_ref[...], sr_ref[...], ret_kdim // 2)
        orq_ref[0, :, sl] = q
        ork_ref[0, :, sl] = k * k_scale


def _qk_prep(proj, g_q, g_k, tabs, cols, *, rope, da_dim, ret_kdim):
    b, n, _ = proj.shape
    d = cols["d"]
    rw = cols["ret_qk"]
    tm = min(n, 1024)
    assert n % tm == 0
    gq = jnp.tile(g_q, LANES // da_dim).reshape(1, LANES)
    gk = jnp.tile(g_k, LANES // da_dim).reshape(1, LANES)
    tab_spec = pl.BlockSpec((tm, LANES), lambda bi, i: (i, 0))
    vec_spec = pl.BlockSpec((1, LANES), lambda bi, i: (0, 0))

    def col_spec(width, off):
        assert off % width == 0
        return pl.BlockSpec((1, tm, width), lambda bi, i: (bi, i, off // width))

    kern = functools.partial(_prep_kernel, rope=rope, da_dim=da_dim, ret_kdim=ret_kdim)
    return pl.pallas_call(
        kern,
        out_shape=(jax.ShapeDtypeStruct((b, n, d), BF16), jax.ShapeDtypeStruct((b, n, d), BF16),
                   jax.ShapeDtypeStruct((b, n, rw), F32), jax.ShapeDtypeStruct((b, n, rw), F32)),
        grid=(b, n // tm),
        in_specs=[col_spec(d, cols["aq"]), col_spec(d, cols["ak"]),
                  col_spec(rw, cols["rq"]), col_spec(rw, cols["rk"]),
                  vec_spec, vec_spec, tab_spec, tab_spec, tab_spec, tab_spec],
        out_specs=(pl.BlockSpec((1, tm, d), lambda bi, i: (bi, i, 0)),
                   pl.BlockSpec((1, tm, d), lambda bi, i: (bi, i, 0)),
                   pl.BlockSpec((1, tm, rw), lambda bi, i: (bi, i, 0)),
                   pl.BlockSpec((1, tm, rw), lambda bi, i: (bi, i, 0))),
        compiler_params=_params(("arbitrary", "arbitrary"),
                                tm * (2 * d + 2 * rw) * 4 + tm * (2 * d * 2 + 2 * rw * 4)
                                + 4 * tm * LANES * 4, 8 * MIB),
        name="qk_prep",
    )(proj, proj, proj, proj, gq, gk, *tabs)


def _rope_tables(n_tok, da_dim, ret_kdim):
    t = jnp.arange(n_tok)

    def angles(pos, dim):
        inv = ROPE_BASE ** (-jnp.arange(0, dim, 2, dtype=F32) / dim)
        return pos.astype(F32)[:, None] * inv[None, :]

    def cs(ang):
        c, s = jnp.cos(ang), jnp.sin(ang)
        return jnp.concatenate([c, c], axis=-1), jnp.concatenate([-s, s], axis=-1)

    half = da_dim // 2
    cr_, sr_ = cs(angles(t // GRID_W, half))
    cc_, sc_ = cs(angles(t % GRID_W, half))
    ca = jnp.tile(jnp.concatenate([cr_, cc_], axis=-1), (1, LANES // da_dim))
    sa = jnp.tile(jnp.concatenate([sr_, sc_], axis=-1), (1, LANES // da_dim))
    c1, s1 = cs(angles(t, ret_kdim))
    cr = jnp.tile(c1, (1, LANES // ret_kdim))
    sr = jnp.tile(s1, (1, LANES // ret_kdim))
    return ca, sa, cr, sr


ATTN_MAX_KEY_TILE = 768
ATTN_ONES_ROWS = 16


def _attn_kernel(scal_ref, q_ref, kx_ref, vx_ref, kc_ref, vc_ref, gsub_ref, o_ref,
                 k_all, v_all, s0_ref, s1_ref, s2_ref, p0_ref, p1_ref, p2_ref, a0_ref, a1_ref, a2_ref,
                 m_ref, acc_ref, *, tq, tk, with_x, da_dim):
    dv = vc_ref.shape[2]
    nc = kc_ref.shape[1]
    nx = kx_ref.shape[1] if with_x else 0
    n_steps = (nx + nc) // tk
    n_tiles = q_ref.shape[1] // tq

    k_all[nx:nx + nc, :] = kc_ref[0]
    v_all[0:dv, nx:nx + nc] = vc_ref[0].T.astype(BF16)
    v_all[dv:, :] = jnp.ones((v_all.shape[0] - dv, nx + nc), BF16)
    if with_x:
        chunk = min(512, nx)

        def fill(j, c):
            off = pl.multiple_of(j * chunk, chunk)
            k_all[pl.ds(off, chunk), :] = kx_ref[0, pl.ds(off, chunk), :]
            v_all[0:dv, pl.ds(off, chunk)] = vx_ref[0, pl.ds(off, chunk), :].T.astype(BF16)
            return c

        lax.fori_loop(0, nx // chunk, fill, 0)

    s_bufs, p_bufs, a_bufs = (s0_ref, s1_ref, s2_ref), (p0_ref, p1_ref, p2_ref), (a0_ref, a1_ref, a2_ref)
    lo = _lane_iota((tq, LANES)) < da_dim

    def buf(j):
        return 2 if (n_steps % 2 == 1 and n_steps > 1 and j == n_steps - 1) else j % 2

    def q_start(t):
        return t * tq if isinstance(t, int) else pl.multiple_of(t * tq, tq)

    def qk(t, j):
        q = q_ref[0, pl.ds(q_start(t), tq), :]
        zero = jnp.zeros_like(q)
        q2 = jnp.concatenate([jnp.where(lo, q, zero), jnp.where(lo, zero, q)], axis=0)
        s_bufs[buf(j)][...] = lax.dot_general(k_all[j * tk:(j + 1) * tk, :], q2,
                                              (((1,), (1,)), ((), ())), preferred_element_type=F32)

    def softmax(j):
        b = buf(j)
        s = s_bufs[b][...]
        col_max = jnp.max(s, axis=0, keepdims=True)
        if j == 0:
            m_new = col_max
        else:
            m_prev = m_ref[...]
            m_new = jnp.maximum(m_prev, col_max)
            a_bufs[b][...] = jnp.exp2(m_prev - m_new)
        p_bufs[b][...] = jnp.exp2(s - m_new).astype(BF16)
        m_ref[...] = m_new

    def pv(j):
        b = buf(j)
        r = jnp.dot(v_all[:, j * tk:(j + 1) * tk], p_bufs[b][...], preferred_element_type=F32)
        if j == 0:
            acc_ref[...] = r
        else:
            acc_ref[...] = a_bufs[b][...] * acc_ref[...] + r

    def finalize(t):
        acc = acc_ref[...]
        o0 = acc[0:dv, 0:tq] / acc[dv:dv + 1, 0:tq]
        o1 = acc[0:dv, tq:2 * tq] / acc[dv:dv + 1, tq:2 * tq]
        o = o0 - scal_ref[0] * o1
        y = o * lax.rsqrt(jnp.mean(o * o, axis=0, keepdims=True) + EPS) * gsub_ref[...]
        o_ref[0, pl.ds(q_start(t), tq), :] = (y * scal_ref[1]).T.astype(BF16)

    if n_steps == 1:
        def lone(t, c):
            qk(t, 0)
            softmax(0)
            pv(0)
            finalize(t)
            return c

        lax.fori_loop(0, n_tiles, lone, 0)
    else:
        qk(0, 0)
        last = buf(n_steps - 1)
        p_bufs[last][...] = jnp.zeros(p_bufs[last].shape, BF16)
        a_bufs[last][...] = jnp.ones(a_bufs[last].shape, F32)
        acc_ref[...] = jnp.ones(acc_ref.shape, F32)

        def tile(t, c):
            qk(t, 1)
            softmax(0)
            pv(n_steps - 1)
            finalize(jnp.maximum(t - 1, 0))
            for j in range(1, n_steps - 1):
                qk(t, j + 1)
                softmax(j)
                pv(j - 1)
            qk(jnp.minimum(t + 1, n_tiles - 1), 0)
            softmax(n_steps - 1)
            pv(n_steps - 2)
            return c

        lax.fori_loop(0, n_tiles, tile, 0)
        pv(n_steps - 1)
        finalize(n_tiles - 1)


def _attention(scal, q, kx, proj_x, kc, proj_c, g_sub, cols, *, with_x, da_dim):
    b, nq, d = q.shape
    dv = g_sub.shape[0]
    heads = d // dv
    nc = kc.shape[1]
    nx = kx.shape[1]
    tq = min(nq, 256)
    n_keys = (nx if with_x else 0) + nc
    tk = max(t for t in range(LANES, ATTN_MAX_KEY_TILE + 1, LANES) if n_keys % t == 0)
    assert nq % tq == 0 and 2 * da_dim == LANES and dv == LANES
    v_blk = cols["av"] // dv
    kern = functools.partial(_attn_kernel, tq=tq, tk=tk, with_x=with_x, da_dim=da_dim)
    rows = 2 * tq
    pipelined = 2 * nq * dv * 2 + nx * dv * (2 + 4) + nc * dv * (2 + 4)
    resident = (n_keys * 3 * dv * 2 + 3 * rows * tk * (4 + 2) + 4 * rows * LANES * 4
                + rows * 2 * dv * 4 + 3 * rows * tk * 4)
    return pl.pallas_call(
        kern,
        out_shape=jax.ShapeDtypeStruct((b, nq, d), BF16),
        grid=(b, heads),
        in_specs=[pl.BlockSpec(memory_space=pltpu.SMEM),
                  pl.BlockSpec((1, nq, dv), lambda bi, h: (bi, 0, h)),
                  pl.BlockSpec((1, nx, dv), lambda bi, h: (bi, 0, h)),
                  pl.BlockSpec((1, nx, dv), lambda bi, h: (bi, 0, v_blk + h)),
                  pl.BlockSpec((1, nc, dv), lambda bi, h: (bi, 0, h)),
                  pl.BlockSpec((1, nc, dv), lambda bi, h: (bi, 0, v_blk + h)),
                  pl.BlockSpec((dv, 1), lambda bi, h: (0, 0))],
        out_specs=pl.BlockSpec((1, nq, dv), lambda bi, h: (bi, 0, h)),
        scratch_shapes=[pltpu.VMEM((n_keys, dv), BF16), pltpu.VMEM((dv + ATTN_ONES_ROWS, n_keys), BF16)]
        + [pltpu.VMEM((tk, rows), F32)] * 3 + [pltpu.VMEM((tk, rows), BF16)] * 3
        + [pltpu.VMEM((1, rows), F32)] * 4 + [pltpu.VMEM((dv + ATTN_ONES_ROWS, rows), F32)],
        compiler_params=_params(("arbitrary", "arbitrary"), pipelined, resident),
        name="diff_attention" if with_x else "diff_attention_ctx",
    )(scal, q, kx, proj_x, kc, proj_c, g_sub.reshape(dv, 1))


CONV_HALO = 16
CONV_ROW_CHUNK = 64


def _conv_kernel(a_ref, g_ref, ap_ref, gp_ref, an_ref, gn_ref, w_ref, cb_ref, lg_ref, lb_ref,
                 o_ref, ext_ref, cv_ref, sh_ref):
    i = pl.program_id(1)
    last = pl.num_programs(1) - 1
    tm = a_ref.shape[1]
    ch = a_ref.shape[2]
    width = w_ref.shape[0]
    pad = width // 2

    def glu(a, g):
        return a * _sigmoid(g)

    ext_ref[0:CONV_HALO, :] = jnp.where(i > 0, glu(ap_ref[0], gp_ref[0]), 0.0)
    ext_ref[CONV_HALO:CONV_HALO + tm, :] = glu(a_ref[0], g_ref[0])
    ext_ref[CONV_HALO + tm:CONV_HALO + tm + CONV_HALO, :] = jnp.where(i < last, glu(an_ref[0], gn_ref[0]), 0.0)

    span = sh_ref.shape[1]
    for c in range(ch // LANES):
        sl = slice(c * LANES, (c + 1) * LANES)
        for s in range(SUBLANES):
            sh_ref[s] = ext_ref[s:s + span, sl]
        def rows(i, carry, sl=sl):
            r0 = pl.multiple_of(i * CONV_ROW_CHUNK, CONV_ROW_CHUNK)
            acc = jnp.zeros((CONV_ROW_CHUNK, LANES), F32)
            for j in range(width):
                start = CONV_HALO - pad + j
                base = start - start % SUBLANES
                acc = acc + w_ref[j:j + 1, sl] * sh_ref[start % SUBLANES, pl.ds(base + r0, CONV_ROW_CHUNK), :]
            cv_ref[pl.ds(r0, CONV_ROW_CHUNK), sl] = acc + cb_ref[:, sl]
            return carry

        lax.fori_loop(0, tm // CONV_ROW_CHUNK, rows, 0)

    v = cv_ref[...]
    mu = jnp.mean(v, axis=-1, keepdims=True)
    vc = v - mu
    var = jnp.mean(vc * vc, axis=-1, keepdims=True)
    y = vc * lax.rsqrt(var + EPS) * lg_ref[...] + lb_ref[...]
    o_ref[0] = (y * _sigmoid(y)).astype(BF16)


def _conformer_conv(proj, conv_w, conv_b, ln_g, ln_b, cols):
    b, n, _ = proj.shape
    width, ch = conv_w.shape
    tm = min(n, 512)
    assert n % tm == 0 and tm % CONV_HALO == 0 and width // 2 < CONV_HALO
    a_blk = cols["conv"] // ch
    g_blk = a_blk + 1
    r = tm // CONV_HALO
    n_halo = n // CONV_HALO

    def cur(blk):
        return pl.BlockSpec((1, tm, ch), lambda bi, i: (bi, i, blk))

    def prev(blk):
        return pl.BlockSpec((1, CONV_HALO, ch), lambda bi, i: (bi, jnp.maximum(i * r - 1, 0), blk))

    def nxt(blk):
        return pl.BlockSpec((1, CONV_HALO, ch), lambda bi, i: (bi, jnp.minimum((i + 1) * r, n_halo - 1), blk))

    vec = pl.BlockSpec((1, ch), lambda bi, i: (0, 0))
    return pl.pallas_call(
        _conv_kernel,
        out_shape=jax.ShapeDtypeStruct((b, n, ch), BF16),
        grid=(b, n // tm),
        in_specs=[cur(a_blk), cur(g_blk), prev(a_blk), prev(g_blk), nxt(a_blk), nxt(g_blk),
                  pl.BlockSpec((width, ch), lambda bi, i: (0, 0)), vec, vec, vec],
        out_specs=pl.BlockSpec((1, tm, ch), lambda bi, i: (bi, i, 0)),
        scratch_shapes=[pltpu.VMEM((tm + 2 * CONV_HALO, ch), F32), pltpu.VMEM((tm, ch), F32),
                        pltpu.VMEM((SUBLANES, tm + 2 * CONV_HALO - SUBLANES, LANES), F32)],
        compiler_params=_params(("arbitrary", "arbitrary"),
                                (2 * tm + 4 * CONV_HALO) * ch * 4 + tm * ch * 2 + 40 * ch * 4,
                                (2 * tm + 2 * CONV_HALO) * ch * 4 + 4 * tm * ch * 4
                                + SUBLANES * (tm + 2 * CONV_HALO) * LANES * 4),
        name="conformer_conv",
    )(proj, proj, proj, proj, proj, proj, conv_w, conv_b.reshape(1, ch), ln_g.reshape(1, ch),
      ln_b.reshape(1, ch))


def _ret_kernel(qf_ref, kf_ref, vf_ref, gf_ref, qb_ref, kb_ref, vb_ref, gb_ref, s0_ref,
                inner_ref, xi_ref, zeta_ref, gc_ref, gng_ref, gnb_ref,
                of_ref, ob_ref, sfin_ref, s_ref, *, kdim):
    i = pl.program_id(1)
    c = qf_ref.shape[1]
    dv = gng_ref.shape[1]
    n_pair = qf_ref.shape[2] // LANES

    @pl.when(i == 0)
    def _():
        s_ref[...] = s0_ref[0]

    lo = _lane_iota((c, LANES)) < kdim
    dirs = ((qf_ref, kf_ref, vf_ref, gf_ref, of_ref), (qb_ref, kb_ref, vb_ref, gb_ref, ob_ref))
    for d, (q_ref, k_ref, v_ref, g_ref, o_ref) in enumerate(dirs):
        for p in range(n_pair):
            sl = slice(p * LANES, (p + 1) * LANES)
            q = q_ref[0, :, sl]
            k = k_ref[0, :, sl]
            kz = k * zeta_ref[d, p]
            kb16 = k.astype(BF16)
            s_pair = s_ref[d, p]
            s16 = s_pair.astype(BF16)
            upd = jnp.zeros((LANES, dv), F32)
            for hh in range(2):
                h = 2 * p + hh
                hsl = slice(h * dv, (h + 1) * dv)
                keep = lo if hh == 0 else jnp.logical_not(lo)
                qh = jnp.where(keep, q, 0.0).astype(BF16)
                v16 = v_ref[0, :, hsl].astype(BF16)
                att = lax.dot_general(qh, kb16, (((1,), (1,)), ((), ())), preferred_element_type=F32)
                att = (att * inner_ref[d, h]).astype(BF16)
                o = (jnp.dot(att, v16, preferred_element_type=F32)
                     + jnp.dot(qh, s16, preferred_element_type=F32) * xi_ref[d, h])
                kzh = jnp.where(keep, kz, 0.0).astype(BF16)
                upd = upd + lax.dot_general(kzh, v16, (((0,), (0,)), ((), ())), preferred_element_type=F32)
                mu = jnp.mean(o, axis=-1, keepdims=True)
                oc = o - mu
                var = jnp.mean(oc * oc, axis=-1, keepdims=True)
                y = oc * lax.rsqrt(var + EPS) * gng_ref[...] + gnb_ref[...]
                g = g_ref[0, :, hsl]
                o_ref[0, :, hsl] = (g * _sigmoid(g)) * y
            s_ref[d, p] = s_pair * gc_ref[p] + upd

    sfin_ref[0] = s_ref[...]


def _retention(rq, rk, proj, s0, tabs, gn_g, gn_b, cols):
    b, n, rw = rq.shape
    dv = gn_g.shape[0]
    d = cols["d"]
    c = RET_CHUNK
    nch = n // c
    kdim = dv // 2
    assert n % c == 0 and c == LANES and 2 * kdim == LANES
    n_pair = rw // LANES
    inner, xi, zeta, gc = tabs
    v_blk, gf_blk, gb_blk = cols["rv"] // d, cols["rgf"] // d, cols["rgb"] // d

    def fwd(width, blk):
        return pl.BlockSpec((1, c, width), lambda bi, i: (bi, i, blk))

    def bwd(width, blk):
        return pl.BlockSpec((1, c, width), lambda bi, i: (bi, nch - 1 - i, blk))

    def whole(a):
        return pl.BlockSpec(a.shape, lambda bi, i: (0,) * a.ndim)

    state = pl.BlockSpec((1, 2, n_pair, LANES, dv), lambda bi, i: (bi, 0, 0, 0, 0))
    vec = pl.BlockSpec((1, dv), lambda bi, i: (0, 0))
    kern = functools.partial(_ret_kernel, kdim=kdim)
    pipelined = 2 * c * (2 * rw + 2 * d) * 4 + 2 * c * d * 4 + 2 * 2 * n_pair * LANES * dv * 4
    resident = (inner.size + xi.size + zeta.size + gc.size) * 4 * 2 + 2 * n_pair * LANES * dv * 4 + 8 * MIB
    return pl.pallas_call(
        kern,
        out_shape=(jax.ShapeDtypeStruct((b, n, d), F32), jax.ShapeDtypeStruct((b, n, d), F32),
                   jax.ShapeDtypeStruct((b, 2, n_pair, LANES, dv), F32)),
        grid=(b, nch),
        in_specs=[fwd(rw, 0), fwd(rw, 0), fwd(d, v_blk), fwd(d, gf_blk),
                  bwd(rw, 0), bwd(rw, 0), bwd(d, v_blk), bwd(d, gb_blk),
                  state, whole(inner), whole(xi), whole(zeta), whole(gc), vec, vec],
        out_specs=(pl.BlockSpec((1, c, d), lambda bi, i: (bi, i, 0)),
                   pl.BlockSpec((1, c, d), lambda bi, i: (bi, nch - 1 - i, 0)),
                   state),
        scratch_shapes=[pltpu.VMEM((2, n_pair, LANES, dv), F32)],
        compiler_params=_params(("arbitrary", "arbitrary"), pipelined, resident),
        name="retention",
    )(rq, rk, proj, proj, rq, rk, proj, proj, s0, inner, xi, zeta, gc,
      gn_g.reshape(1, dv), gn_b.reshape(1, dv))


def _retention_tables(heads, kdim, dv):
    c = RET_CHUNK
    lg = jnp.log1p(-jnp.exp2(-5.0 - jnp.arange(heads, dtype=F32)))
    pos = jnp.arange(c, dtype=F32)
    diff = pos[:, None] - pos[None, :]
    inner_f = jnp.where(diff[None] >= 0, jnp.exp(jnp.maximum(diff, 0.0)[None] * lg[:, None, None]), 0.0)
    inner = jnp.stack([inner_f, jnp.swapaxes(inner_f, 1, 2)])
    xi_f = jnp.exp((pos + 1.0)[None, :] * lg[:, None])
    zeta_f = jnp.exp((c - 1.0 - pos)[None, :] * lg[:, None])
    xi = jnp.stack([xi_f, xi_f[:, ::-1]])
    zeta = jnp.stack([zeta_f, zeta_f[:, ::-1]])
    xi = jnp.broadcast_to(xi[..., None], (2, heads, c, dv))
    zeta = jnp.broadcast_to(zeta[..., None], (2, heads, c, kdim))
    zeta = zeta.reshape(2, heads // 2, 2, c, kdim).transpose(0, 1, 3, 2, 4).reshape(2, heads // 2, c, 2 * kdim)
    gc = jnp.exp(c * lg)
    gc = jnp.broadcast_to(gc[:, None, None], (heads, kdim, dv)).reshape(heads // 2, 2 * kdim, dv)
    return inner.astype(F32), xi.astype(F32), zeta.astype(F32), gc.astype(F32)


def _merge_kernel(h_ref, mod_ref, oa_ref, ob_ref, of_ref, obk_ref, gate_ref, bg_ref,
                  wa_ref, wb_ref, wc_ref, wo_ref, o_ref):
    d = h_ref.shape[2]
    g = _sigmoid(gate_ref[0] + bg_ref[...])
    oc = (of_ref[0] + obk_ref[0]).astype(BF16)
    y = g[:, 0:d] * jnp.dot(oa_ref[0], wa_ref[...], preferred_element_type=F32)
    y = y + g[:, d:2 * d] * jnp.dot(ob_ref[0], wb_ref[...], preferred_element_type=F32)
    y = y + g[:, 2 * d:3 * d] * jnp.dot(oc, wc_ref[...], preferred_element_type=F32)
    m = jnp.dot(y.astype(BF16), wo_ref[...], preferred_element_type=F32)
    o_ref[0] = h_ref[0] + mod_ref[0, 2:3, :] * m


def _merge(h, mod, oa, ob, of, obk, proj, b_gate, w_pa, w_pb, w_pc, w_out, layer, cols):
    b, n, d = h.shape
    tm = min(n, 512)
    assert n % tm == 0
    mb = mod.shape[0]
    gate_blk = cols["gate"] // (3 * d)
    row = lambda width: pl.BlockSpec((1, tm, width), lambda bi, i: (bi, i, 0))
    wspec = pl.BlockSpec((None, d, d), lambda bi, i: (layer, 0, 0))
    return pl.pallas_call(
        _merge_kernel,
        out_shape=jax.ShapeDtypeStruct((b, n, d), F32),
        grid=(b, n // tm),
        in_specs=[row(d), pl.BlockSpec((1, N_MOD, d), lambda bi, i: (bi % mb, 0, 0)),
                  row(d), row(d), row(d), row(d),
                  pl.BlockSpec((1, tm, 3 * d), lambda bi, i: (bi, i, gate_blk)),
                  pl.BlockSpec((1, 3 * d), lambda bi, i: (0, 0)),
                  wspec, wspec, wspec, wspec],
        out_specs=row(d),
        compiler_params=_params(("arbitrary", "arbitrary"),
                                tm * d * (4 + 2 + 2 + 4 + 4 + 12 + 4) + 4 * d * d * 2, 10 * tm * d * 4),
        name="merge",
    )(h, mod, oa, ob, of, obk, proj, b_gate.reshape(1, 3 * d), w_pa, w_pb, w_pc, w_out)


ROUTE_E, ROUTE_G, ROUTE_R = 0, 4, 8


def _route_kernel(h_ref, mod_ref, g_ref, wr_ref, br_ref, tri_ref, base_ref,
                  hx_ref, route_ref, cnt_ref, run_ref, *, n_exp):
    first = jnp.logical_and(pl.program_id(0) == 0, pl.program_id(1) == 0)

    @pl.when(first)
    def _():
        run_ref[...] = base_ref[...]

    h = h_ref[0]
    tm = h.shape[0]
    y = h * lax.rsqrt(jnp.mean(h * h, axis=-1, keepdims=True) + EPS) * g_ref[...]
    hx = y * (1.0 + mod_ref[0, 4:5, :]) + mod_ref[0, 3:4, :]
    hx_ref[0] = hx

    lane = _lane_iota((tm, LANES))
    lane_f = lane.astype(F32)
    hx_hi = hx.astype(BF16)
    hx_lo = (hx - hx_hi.astype(F32)).astype(BF16)
    logits = (jnp.dot(hx_hi, wr_ref[0], preferred_element_type=F32)
              + jnp.dot(hx_hi, wr_ref[1], preferred_element_type=F32)
              + jnp.dot(hx_lo, wr_ref[0], preferred_element_type=F32)) + br_ref[...]
    logits = jnp.where(lane < n_exp, logits, NEG_INF)

    sels, vals = [], []
    for _ in range(TOP_K):
        mx = jnp.max(logits, axis=-1, keepdims=True)
        idx = jnp.min(jnp.where(logits == mx, lane_f, float(LANES)), axis=-1, keepdims=True)
        sel = lane_f == idx
        sels.append((sel, idx))
        vals.append(mx)
        logits = jnp.where(sel, NEG_INF, logits)

    es = [jnp.exp(v - vals[0]) for v in vals]
    denom = es[0]
    for e in es[1:]:
        denom = denom + e

    onehot = jnp.zeros((tm, LANES), F32)
    for sel, _ in sels:
        onehot = onehot + jnp.where(sel, 1.0, 0.0)
    before = run_ref[...] + jnp.dot(tri_ref[...], onehot.astype(BF16), preferred_element_type=F32)

    route = jnp.zeros((tm, LANES), F32)
    for k, (sel, idx) in enumerate(sels):
        rank = jnp.sum(jnp.where(sel, before, 0.0), axis=-1, keepdims=True)
        route = jnp.where(lane == ROUTE_E + k, idx, route)
        route = jnp.where(lane == ROUTE_G + k, es[k] / denom, route)
        route = jnp.where(lane == ROUTE_R + k, rank, route)
    route_ref[0] = route

    run_ref[...] = run_ref[...] + jnp.sum(onehot, axis=0, keepdims=True)
    cnt_ref[...] = run_ref[...]


def _route(h, mod, g, w_router, b_router, base):
    b, n, d = h.shape
    n_exp = w_router.shape[-1]
    tm = min(n, 256)
    assert n % tm == 0 and n_exp <= LANES
    mb = mod.shape[0]
    wr = jnp.zeros((d, LANES), F32).at[:, :n_exp].set(w_router)
    wr_hi = wr.astype(BF16)
    wr = jnp.stack([wr_hi, (wr - wr_hi.astype(F32)).astype(BF16)])
    br = jnp.zeros((1, LANES), F32).at[0, :n_exp].set(b_router)
    tri = (jnp.arange(tm)[:, None] > jnp.arange(tm)[None, :]).astype(BF16)
    row = lambda width: pl.BlockSpec((1, tm, width), lambda bi, i: (bi, i, 0))
    const = lambda shape: pl.BlockSpec(shape, lambda bi, i: (0,) * len(shape))
    kern = functools.partial(_route_kernel, n_exp=n_exp)
    return pl.pallas_call(
        kern,
        out_shape=(jax.ShapeDtypeStruct((b, n, d), F32), jax.ShapeDtypeStruct((b, n, LANES), F32),
                   jax.ShapeDtypeStruct((1, LANES), F32)),
        grid=(b, n // tm),
        in_specs=[row(d), pl.BlockSpec((1, N_MOD, d), lambda bi, i: (bi % mb, 0, 0)), const((1, d)),
                  const((2, d, LANES)), const((1, LANES)), const((tm, tm)), const((1, LANES))],
        out_specs=(row(d), row(LANES), const((1, LANES))),
        scratch_shapes=[pltpu.VMEM((1, LANES), F32)],
        compiler_params=_params(("arbitrary", "arbitrary"),
                                2 * tm * d * 4 + tm * LANES * 4 + d * LANES * 4 + tm * tm * 2, 8 * tm * d * 4),
        name="moe_route",
    )(h, mod, g.reshape(1, d), wr, br, tri, base)


DMA_ISSUE_UNROLL = 8


def _tile_dest(dest, tm):
    t = dest.shape[0]
    return dest.reshape(t // tm, tm, TOP_K).transpose(0, 2, 1).reshape(t // tm, 1, TOP_K * tm)


def _dispatch_kernel(dest_ref, hx_ref, xs_in_ref, xs_ref, sem):
    del xs_in_ref
    tm = hx_ref.shape[0]

    def copy(k, t):
        return pltpu.make_async_copy(hx_ref.at[pl.ds(t, 1)],
                                     xs_ref.at[pl.ds(dest_ref[0, 0, k * tm + t], 1)], sem)

    for t in range(tm):
        for k in range(TOP_K):
            copy(k, t).start(priority=(t * TOP_K + k) % 2)
    for k in range(TOP_K):
        def wait(t, c, k=k):
            copy(k, t).wait()
            return c

        lax.fori_loop(0, tm, wait, 0, unroll=DMA_ISSUE_UNROLL)


def _dispatch(hx_flat, dest, xs):
    t, d = hx_flat.shape
    tm = min(t, 256)
    assert t % tm == 0
    dest3 = _tile_dest(dest, tm)
    return pl.pallas_call(
        _dispatch_kernel,
        out_shape=jax.ShapeDtypeStruct(xs.shape, xs.dtype),
        grid=(t // tm,),
        in_specs=[pl.BlockSpec((1, 1, tm * TOP_K), lambda i: (i, 0, 0), memory_space=pltpu.SMEM),
                  pl.BlockSpec((tm, d), lambda i: (i, 0)),
                  pl.BlockSpec(memory_space=pl.ANY)],
        out_specs=pl.BlockSpec(memory_space=pl.ANY),
        scratch_shapes=[pltpu.SemaphoreType.DMA],
        input_output_aliases={2: 0},
        compiler_params=pltpu.CompilerParams(dimension_semantics=("arbitrary",),
                                             vmem_limit_bytes=_vmem_limit(tm * d * 4, 4 * MIB),
                                             has_side_effects=True),
        name="moe_dispatch",
    )(dest3, hx_flat, xs)


def _split_kernel(w_ref, sel_ref, g_ref, u_ref):
    blk = sel_ref.shape[0]
    half = blk // 2
    for i in range(w_ref.shape[1] // blk):
        r = jnp.dot(w_ref[:, i * blk:(i + 1) * blk].astype(BF16), sel_ref[...], preferred_element_type=F32)
        g_ref[:, i * half:(i + 1) * half] = r[:, :half].astype(BF16)
        u_ref[:, i * half:(i + 1) * half] = r[:, half:].astype(BF16)


def _split_gate_up(w_gate_up):
    depth, n_exp, d, de2 = w_gate_up.shape
    de = de2 // 2
    tr = min(d, 512)
    assert d % tr == 0
    blk = 2 * LANES
    assert de2 % blk == 0
    order = jnp.concatenate([jnp.arange(0, blk, 2), jnp.arange(1, blk, 2)])
    sel = (jnp.arange(blk)[:, None] == order[None, :]).astype(BF16)
    out = jax.ShapeDtypeStruct((depth, n_exp, d, de), BF16)
    ospec = pl.BlockSpec((None, None, tr, de), lambda l, e, r: (l, e, r, 0))
    return pl.pallas_call(
        _split_kernel,
        out_shape=(out, out),
        grid=(depth, n_exp, d // tr),
        in_specs=[pl.BlockSpec((None, None, tr, de2), lambda l, e, r: (l, e, r, 0)),
                  pl.BlockSpec((blk, blk), lambda l, e, r: (0, 0))],
        out_specs=(ospec, ospec),
        compiler_params=_params(("arbitrary", "arbitrary", "arbitrary"),
                                tr * de2 * 4 + blk * blk * 2 + 2 * tr * de * 2, tr * de2 * (2 + 4 + 4)),
        name="split_gate_up",
    )(w_gate_up, sel)


def _expert_kernel(blk_e_ref, n_used_ref, x_ref, wg_ref, wu_ref, wd_ref, bg_ref, bu_ref, bd_ref, y_ref):
    del blk_e_ref
    used = pl.program_id(0) < n_used_ref[0]

    @pl.when(used)
    def _():
        x = x_ref[...].astype(BF16)
        gl = jnp.dot(x, wg_ref[...], preferred_element_type=F32) + bg_ref[...]
        up = jnp.dot(x, wu_ref[...], preferred_element_type=F32) + bu_ref[...]
        gl = jnp.minimum(gl, SWIGLU_LIMIT)
        up = jnp.clip(up, -SWIGLU_LIMIT, SWIGLU_LIMIT)
        act = (up + 1.0) * (gl * _sigmoid(SWIGLU_ALPHA * gl))
        y_ref[...] = jnp.dot(act.astype(BF16), wd_ref[...], preferred_element_type=F32) + bd_ref[...]

    @pl.when(jnp.logical_not(used))
    def _():
        y_ref[...] = jnp.zeros(y_ref.shape, F32)


def _experts(blk_e, n_used, xs, wg, wu, wd, bg, bu, bd, layer, bm):
    n_rows, d = xs.shape
    de = wg.shape[-1]
    nb = n_rows // bm
    wspec = lambda a, c: pl.BlockSpec((None, None, a, c), lambda i, be, nu: (layer, be[i], 0, 0))
    return pl.pallas_call(
        _expert_kernel,
        out_shape=jax.ShapeDtypeStruct((n_rows, d), F32),
        grid_spec=pltpu.PrefetchScalarGridSpec(
            num_scalar_prefetch=2,
            grid=(nb,),
            in_specs=[pl.BlockSpec((bm, d), lambda i, be, nu: (i, 0)),
                      wspec(d, de), wspec(d, de), wspec(de, d),
                      wspec(1, de), wspec(1, de), wspec(1, d)],
            out_specs=pl.BlockSpec((bm, d), lambda i, be, nu: (i, 0))),
        compiler_params=_params(("arbitrary",), 2 * bm * d * 4 + 3 * d * de * 2, 8 * bm * de * 4),
        name="moe_experts",
    )(blk_e, n_used, xs, wg, wu, wd, bg, bu, bd)


def _combine_kernel(dest_ref, dnext_ref, h_ref, mod_ref, route_ref, y_ref, o_ref, ybuf0, ybuf1, sem):
    tm = h_ref.shape[1]
    n_copy = TOP_K * tm
    step = pl.program_id(0) * pl.num_programs(1) + pl.program_id(1)
    n_step = pl.num_programs(0) * pl.num_programs(1)
    bufs = (ybuf0, ybuf1)

    def copy(idx_ref, r, slot):
        return pltpu.make_async_copy(y_ref.at[pl.ds(idx_ref[0, 0, r], 1)], bufs[slot].at[pl.ds(r, 1)],
                                     sem.at[slot])

    def fetch(idx_ref, slot):
        for r in range(n_copy):
            copy(idx_ref, r, slot).start(priority=r % 2)

    def finish(slot):
        def wait(r, c):
            copy(dest_ref, r, slot).wait()
            return c

        lax.fori_loop(0, n_copy, wait, 0, unroll=DMA_ISSUE_UNROLL)
        route = route_ref[0]
        f = jnp.zeros((tm, h_ref.shape[2]), F32)
        for k in range(TOP_K):
            f = f + route[:, ROUTE_G + k:ROUTE_G + k + 1] * bufs[slot][k * tm:(k + 1) * tm, :]
        o_ref[0] = h_ref[0] + mod_ref[0, 5:6, :] * f

    @pl.when(step == 0)
    def _():
        fetch(dest_ref, 0)

    for slot in range(2):
        @pl.when(step % 2 == slot)
        def _(slot=slot):
            @pl.when(step + 1 < n_step)
            def _():
                fetch(dnext_ref, 1 - slot)

            finish(slot)


def _combine(h, mod, route, dest, y):
    b, n, d = h.shape
    tm = min(n, 256)
    assert n % tm == 0
    mb = mod.shape[0]
    nt = n // tm
    dest3 = _tile_dest(dest, tm)
    row = lambda width: pl.BlockSpec((1, tm, width), lambda bi, i: (bi, i, 0))
    return pl.pallas_call(
        _combine_kernel,
        out_shape=jax.ShapeDtypeStruct((b, n, d), F32),
        grid=(b, nt),
        in_specs=[pl.BlockSpec((1, 1, tm * TOP_K), lambda bi, i: (bi * nt + i, 0, 0), memory_space=pltpu.SMEM),
                  pl.BlockSpec((1, 1, tm * TOP_K), lambda bi, i: (jnp.minimum(bi * nt + i + 1, b * nt - 1), 0, 0),
                               memory_space=pltpu.SMEM),
                  row(d), pl.BlockSpec((1, N_MOD, d), lambda bi, i: (bi % mb, 0, 0)), row(LANES),
                  pl.BlockSpec(memory_space=pl.ANY)],
        out_specs=row(d),
        scratch_shapes=[pltpu.VMEM((TOP_K * tm, d), F32), pltpu.VMEM((TOP_K * tm, d), F32),
                        pltpu.SemaphoreType.DMA((2,))],
        compiler_params=_params(("arbitrary", "arbitrary"), 2 * tm * d * 4 + tm * LANES * 4,
                                2 * TOP_K * tm * d * 4 + 4 * tm * d * 4),
        name="moe_combine",
    )(dest3, dest3, h, mod, route, y)


def _moe_rows(n_tok, n_exp, bm):
    return -(-(n_tok * TOP_K + n_exp * (bm - 1)) // bm) * bm


def _moe(parts, xs, g, w_router, b_router, wg, wu, wd, bg, bu, bd, layer, bm):
    n_exp = w_router.shape[-1]
    n_rows, d = xs.shape
    base = jnp.zeros((1, LANES), F32)
    hxs, routes = [], []
    for h, mod in parts:
        hx, route, base = _route(h, mod, g, w_router, b_router, base)
        hxs.append(hx)
        routes.append(route)
    counts = base[0, :n_exp].astype(jnp.int32)

    padded = (counts + bm - 1) // bm * bm
    pend = jnp.cumsum(padded)
    pstart = pend - padded
    n_tok = sum(h.shape[0] * h.shape[1] for h, _ in parts)
    assert n_rows >= _moe_rows(n_tok, n_exp, bm) and n_rows % bm == 0
    nb = n_rows // bm
    blk_start = jnp.arange(nb, dtype=jnp.int32) * bm
    blk_e = jnp.minimum(jnp.sum((pend[None, :] <= blk_start[:, None]).astype(jnp.int32), axis=1), n_exp - 1)
    n_used = (pend[-1] // bm).astype(jnp.int32).reshape(1)
    eids = jnp.arange(n_exp, dtype=jnp.int32)

    dests = []
    for hx, route in zip(hxs, routes):
        e = route[..., ROUTE_E:ROUTE_E + TOP_K].astype(jnp.int32)
        rank = route[..., ROUTE_R:ROUTE_R + TOP_K].astype(jnp.int32)
        start = jnp.sum(jnp.where(e[..., None] == eids, pstart, 0), axis=-1)
        dest = (start + rank).reshape(-1, TOP_K)
        dests.append(dest)
        xs = _dispatch(hx.reshape(-1, d), dest, xs)

    y = _experts(blk_e, n_used, xs, wg, wu, wd, bg, bu, bd, layer, bm)
    return [_combine(h, mod, route, dest, y)
            for (h, mod), route, dest in zip(parts, routes, dests)], xs


def kernel(x, c, ctx, c_ctx, w_mod, b_mod, g_norm1, g_norm2, w_in, b_gate, g_qa, g_ka, lam_q1, lam_k1, lam_q2, lam_k2, g_sub, conv_w, conv_b, ln_g, ln_b, gn_g, gn_b, w_pa, w_pb, w_pc, w_out, w_router, b_router, w_gate_up, b_gate_up, w_down, b_down):
    bsz, n_x, d = x.shape
    n_c = ctx.shape[1]
    depth = w_mod.shape[0]
    da_dim = g_qa.shape[-1]
    dv = g_sub.shape[-1]
    heads = d // dv
    ret_kdim = gn_g.shape[-1] // 2
    n_exp = w_router.shape[-1]
    de = w_down.shape[-2]
    moe_bm = 512

    cols = {"d": d, "ret_qk": heads * ret_kdim}
    off = 0
    for name, width in (("aq", d), ("ak", d), ("av", d), ("conv", 2 * d), ("rq", heads * ret_kdim),
                        ("rk", heads * ret_kdim), ("rv", d), ("rgf", d), ("rgb", d), ("gate", 3 * d)):
        cols[name] = off
        off += width
    assert off == w_in.shape[-1]

    w_in_bf = w_in.astype(BF16)
    w_pa_bf, w_pb_bf, w_pc_bf, w_out_bf = (w.astype(BF16) for w in (w_pa, w_pb, w_pc, w_out))
    wg, wu = _split_gate_up(w_gate_up)
    wd = w_down.astype(BF16)
    bg = b_gate_up[..., 0::2].reshape(depth, n_exp, 1, de)
    bu = b_gate_up[..., 1::2].reshape(depth, n_exp, 1, de)
    bd = b_down.reshape(depth, n_exp, 1, d)

    rows = -(-(bsz + 1) // SUBLANES) * SUBLANES
    cvecs = jnp.zeros((rows, d), F32).at[:bsz].set(c).at[bsz].set(c_ctx)
    mods = _modulation(cvecs, w_mod, b_mod).reshape(depth, rows, N_MOD, d)

    tabs_x = _rope_tables(n_x, da_dim, ret_kdim)
    tabs_c = tuple(jnp.zeros((n_c, LANES), F32) for _ in range(4))
    ret_tabs = _retention_tables(heads, ret_kdim, dv)
    s_zero = jnp.zeros((bsz, 2, heads // 2, LANES, dv), F32)
    xs_rows = jnp.zeros((_moe_rows(bsz * (n_c + n_x), n_exp, moe_bm), d), F32)

    h_ctx = ctx
    for l in range(depth):
        need_ctx = l < depth - 1
        lam_init = 0.8 - 0.6 * math.exp(-0.3 * l)
        lam = (jnp.exp(jnp.sum(lam_q1[l] * lam_k1[l])) - jnp.exp(jnp.sum(lam_q2[l] * lam_k2[l])) + lam_init)
        scal = jnp.stack([lam, jnp.asarray(1.0 - lam_init, F32)]).astype(F32)
        mod_x = mods[l, :bsz]
        mod_c = mods[l, bsz:bsz + 1]

        proj_x = _in_proj(x, mod_x, g_norm1[l], w_in_bf, l)
        proj_c = _in_proj(h_ctx, mod_c, g_norm1[l], w_in_bf, l)
        aq_x, ak_x, rq_x, rk_x = _qk_prep(proj_x, g_qa[l], g_ka[l], tabs_x, cols, rope=True,
                                          da_dim=da_dim, ret_kdim=ret_kdim)
        aq_c, ak_c, rq_c, rk_c = _qk_prep(proj_c, g_qa[l], g_ka[l], tabs_c, cols, rope=False,
                                          da_dim=da_dim, ret_kdim=ret_kdim)
        oa_x = _attention(scal, aq_x, ak_x, proj_x, ak_c, proj_c, g_sub[l], cols, with_x=True, da_dim=da_dim)
        ob_x = _conformer_conv(proj_x, conv_w[l], conv_b[l], ln_g[l], ln_b[l], cols)
        of_c, ob_c, s_ctx = _retention(rq_c, rk_c, proj_c, s_zero, ret_tabs, gn_g[l], gn_b[l], cols)
        of_x, obk_x, _ = _retention(rq_x, rk_x, proj_x, s_ctx, ret_tabs, gn_g[l], gn_b[l], cols)
        x = _merge(x, mod_x, oa_x, ob_x, of_x, obk_x, proj_x, b_gate[l],
                   w_pa_bf, w_pb_bf, w_pc_bf, w_out_bf, l, cols)
        if need_ctx:
            oa_c = _attention(scal, aq_c, ak_c, proj_c, ak_c, proj_c, g_sub[l], cols, with_x=False, da_dim=da_dim)
            cb_c = _conformer_conv(proj_c, conv_w[l], conv_b[l], ln_g[l], ln_b[l], cols)
            h_ctx = _merge(h_ctx, mod_c, oa_c, cb_c, of_c, ob_c, proj_c, b_gate[l],
                           w_pa_bf, w_pb_bf, w_pc_bf, w_out_bf, l, cols)

        parts = [(h_ctx, mod_c), (x, mod_x)] if need_ctx else [(x, mod_x)]
        outs, xs_rows = _moe(parts, xs_rows, g_norm2[l], w_router[l], b_router[l], wg, wu, wd, bg, bu, bd,
                             l, moe_bm)
        if need_ctx:
            h_ctx, x = outs
        else:
            (x,) = outs
    return x
```

```python
import functools
import math

import jax
import jax.numpy as jnp
from jax import lax
from jax.experimental import pallas as pl
from jax.experimental.pallas import tpu as pltpu

F32 = jnp.float32
BF16 = jnp.bfloat16

GRID_W = 64
N_MOD = 6
EPS = 1e-6
ROPE_BASE = 10000.0
RET_CHUNK = 128
TOP_K = 4
SWIGLU_ALPHA = 1.702
SWIGLU_LIMIT = 7.0

LANES = 128
SUBLANES = 8
MIB = 2**20
V7X_VMEM_BYTES = 64 * MIB
V7X_VMEM_COMPILER_RESERVE = 8 * MIB
VMEM_LIMIT_CAP = V7X_VMEM_BYTES - V7X_VMEM_COMPILER_RESERVE
VMEM_LIMIT_FLOOR = 16 * MIB

NEG_INF = float("-inf")
LOG2_E = math.log2(math.e)


def _sigmoid(x):
    return 1.0 / (1.0 + jnp.exp(-x))


def _vmem_limit(pipelined_bytes, resident_bytes):
    need = 2 * pipelined_bytes + resident_bytes
    return int(min(max(need, VMEM_LIMIT_FLOOR), VMEM_LIMIT_CAP))


def _params(sem, pipelined_bytes, resident_bytes):
    return pltpu.CompilerParams(dimension_semantics=sem,
                                vmem_limit_bytes=_vmem_limit(pipelined_bytes, resident_bytes))


def _lane_iota(shape):
    return lax.broadcasted_iota(jnp.int32, shape, len(shape) - 1)


def _mod_kernel(c_ref, w_ref, b_ref, o_ref):
    c = c_ref[...]
    a = (c * _sigmoid(c)).astype(BF16)
    o_ref[...] = jnp.dot(a, w_ref[...].astype(BF16), preferred_element_type=F32) + b_ref[...]


def _modulation(cvecs, w_mod, b_mod):
    depth, d, cols = w_mod.shape
    rows = cvecs.shape[0]
    tn = 1536
    assert cols % tn == 0
    return pl.pallas_call(
        _mod_kernel,
        out_shape=jax.ShapeDtypeStruct((depth, rows, cols), F32),
        grid=(depth, cols // tn),
        in_specs=[pl.BlockSpec((rows, d), lambda l, j: (0, 0)),
                  pl.BlockSpec((None, d, tn), lambda l, j: (l, 0, j)),
                  pl.BlockSpec((None, 1, tn), lambda l, j: (l, 0, j))],
        out_specs=pl.BlockSpec((None, rows, tn), lambda l, j: (l, 0, j)),
        compiler_params=_params(("arbitrary", "arbitrary"), d * tn * 4 + rows * tn * 4, 4 * MIB),
        name="modulation",
    )(cvecs, w_mod, b_mod.reshape(depth, 1, cols))


def _inproj_kernel(h_ref, mod_ref, g_ref, w_ref, o_ref, xs_ref):
    @pl.when(pl.program_id(2) == 0)
    def _():
        h = h_ref[0]
        y = h * lax.rsqrt(jnp.mean(h * h, axis=-1, keepdims=True) + EPS) * g_ref[...]
        xs_ref[...] = (y * (1.0 + mod_ref[0, 1:2, :]) + mod_ref[0, 0:1, :]).astype(BF16)

    o_ref[0] = jnp.dot(xs_ref[...], w_ref[...], preferred_element_type=F32)


def _in_proj(h, mod, g, w_in_bf, layer):
    b, n, d = h.shape
    cols = w_in_bf.shape[-1]
    tm = min(n, 1024)
    tn = 3072
    assert n % tm == 0 and cols % tn == 0
    mb = mod.shape[0]
    return pl.pallas_call(
        _inproj_kernel,
        out_shape=jax.ShapeDtypeStruct((b, n, cols), F32),
        grid=(b, n // tm, cols // tn),
        in_specs=[pl.BlockSpec((1, tm, d), lambda bi, i, j: (bi, i, 0)),
                  pl.BlockSpec((1, N_MOD, d), lambda bi, i, j: (bi % mb, 0, 0)),
                  pl.BlockSpec((1, d), lambda bi, i, j: (0, 0)),
                  pl.BlockSpec((None, d, tn), lambda bi, i, j: (layer, 0, j))],
        out_specs=pl.BlockSpec((1, tm, tn), lambda bi, i, j: (bi, i, j)),
        scratch_shapes=[pltpu.VMEM((tm, d), BF16)],
        compiler_params=_params(("arbitrary", "arbitrary", "arbitrary"),
                                tm * d * 4 + d * tn * 2 + tm * tn * 4, tm * d * 2 + 3 * tm * d * 4),
        name="in_proj",
    )(h, mod, g.reshape(1, d), w_in_bf)


def _prep_kernel(aq_ref, ak_ref, rq_ref, rk_ref, gq_ref, gk_ref, ca_ref, sa_ref, cr_ref, sr_ref,
                 oq_ref, ok_ref, orq_ref, ork_ref, *, rope, da_dim, ret_kdim):
    tm = aq_ref.shape[1]
    lane = _lane_iota((tm, LANES))

    same_group = (lax.broadcasted_iota(jnp.int32, (LANES, LANES), 0) // da_dim
                  == lax.broadcasted_iota(jnp.int32, (LANES, LANES), 1) // da_dim)
    group_ones = jnp.where(same_group, 1.0, 0.0).astype(BF16)

    def rms(x, g):
        x2 = x * x
        hi = x2.astype(BF16)
        rest = (x2 - hi.astype(F32)).astype(BF16)
        ss = (jnp.dot(hi, group_ones, preferred_element_type=F32)
              + jnp.dot(rest, group_ones, preferred_element_type=F32))
        return x * lax.rsqrt(ss * (1.0 / da_dim) + EPS) * g

    def rot(x, c, s, half):
        first = (lane % (2 * half)) < half
        partner = jnp.where(first, pltpu.roll(x, LANES - half, 1), pltpu.roll(x, half, 1))
        return x * c + partner * s

    q_scale = da_dim ** -0.5 * LOG2_E
    k_scale = ret_kdim ** -0.5
    for j in range(aq_ref.shape[2] // LANES):
        sl = slice(j * LANES, (j + 1) * LANES)
        q = rms(aq_ref[0, :, sl], gq_ref[...])
        k = rms(ak_ref[0, :, sl], gk_ref[...])
        if rope:
            q = rot(q, ca_ref[...], sa_ref[...], da_dim // 4)
            k = rot(k, ca_ref[...], sa_ref[...], da_dim // 4)
        oq_ref[0, :, sl] = (q * q_scale).astype(BF16)
        ok_ref[0, :, sl] = k.astype(BF16)
    for j in range(rq_ref.shape[2] // LANES):
        sl = slice(j * LANES, (j + 1) * LANES)
        q = rq_ref[0, :, sl]
        k = rk_ref[0, :, sl]
        if rope:
            q = rot(q, cr_ref[...], sr_ref[...], ret_kdim // 2)
            k = rot(k, cr_ref[...], sr_ref[...], ret_kdim // 2)
        orq_ref[0, :, sl] = q
        ork_ref[0, :, sl] = k * k_scale


def _qk_prep(proj, g_q, g_k, tabs, cols, *, rope, da_dim, ret_kdim):
    b, n, _ = proj.shape
    d = cols["d"]
    rw = cols["ret_qk"]
    tm = min(n, 1024)
    assert n % tm == 0
    gq = jnp.tile(g_q, LANES // da_dim).reshape(1, LANES)
    gk = jnp.tile(g_k, LANES // da_dim).reshape(1, LANES)
    tab_spec = pl.BlockSpec((tm, LANES), lambda bi, i: (i, 0))
    vec_spec = pl.BlockSpec((1, LANES), lambda bi, i: (0, 0))

    def col_spec(width, off):
        assert off % width == 0
        return pl.BlockSpec((1, tm, width), lambda bi, i: (bi, i, off // width))

    kern = functools.partial(_prep_kernel, rope=rope, da_dim=da_dim, ret_kdim=ret_kdim)
    return pl.pallas_call(
        kern,
        out_shape=(jax.ShapeDtypeStruct((b, n, d), BF16), jax.ShapeDtypeStruct((b, n, d), BF16),
                   jax.ShapeDtypeStruct((b, n, rw), F32), jax.ShapeDtypeStruct((b, n, rw), F32)),
        grid=(b, n // tm),
        in_specs=[col_spec(d, cols["aq"]), col_spec(d, cols["ak"]),
                  col_spec(rw, cols["rq"]), col_spec(rw, cols["rk"]),
                  vec_spec, vec_spec, tab_spec, tab_spec, tab_spec, tab_spec],
        out_specs=(pl.BlockSpec((1, tm, d), lambda bi, i: (bi, i, 0)),
                   pl.BlockSpec((1, tm, d), lambda bi, i: (bi, i, 0)),
                   pl.BlockSpec((1, tm, rw), lambda bi, i: (bi, i, 0)),
                   pl.BlockSpec((1, tm, rw), lambda bi, i: (bi, i, 0))),
        compiler_params=_params(("arbitrary", "arbitrary"),
                                tm * (2 * d + 2 * rw) * 4 + tm * (2 * d * 2 + 2 * rw * 4)
                                + 4 * tm * LANES * 4, 8 * MIB),
        name="qk_prep",
    )(proj, proj, proj, proj, gq, gk, *tabs)


def _rope_tables(n_tok, da_dim, ret_kdim):
    t = jnp.arange(n_tok)

    def angles(pos, dim):
        inv = ROPE_BASE ** (-jnp.arange(0, dim, 2, dtype=F32) / dim)
        return pos.astype(F32)[:, None] * inv[None, :]

    def cs(ang):
        c, s = jnp.cos(ang), jnp.sin(ang)
        return jnp.concatenate([c, c], axis=-1), jnp.concatenate([-s, s], axis=-1)

    half = da_dim // 2
    cr_, sr_ = cs(angles(t // GRID_W, half))
    cc_, sc_ = cs(angles(t % GRID_W, half))
    ca = jnp.tile(jnp.concatenate([cr_, cc_], axis=-1), (1, LANES // da_dim))
    sa = jnp.tile(jnp.concatenate([sr_, sc_], axis=-1), (1, LANES // da_dim))
    c1, s1 = cs(angles(t, ret_kdim))
    cr = jnp.tile(c1, (1, LANES // ret_kdim))
    sr = jnp.tile(s1, (1, LANES // ret_kdim))
    return ca, sa, cr, sr


ATTN_MAX_KEY_TILE = 768
ATTN_ONES_ROWS = 16


def _attn_kernel(scal_ref, q_ref, kx_ref, vx_ref, kc_ref, vc_ref, gsub_ref, o_ref,
                 k_all, v_all, s0_ref, s1_ref, s2_ref, p0_ref, p1_ref, p2_ref, a0_ref, a1_ref, a2_ref,
                 m_ref, acc_ref, *, tq, tk, with_x, da_dim):
    dv = vc_ref.shape[2]
    nc = kc_ref.shape[1]
    nx = kx_ref.shape[1] if with_x else 0
    n_steps = (nx + nc) // tk
    n_tiles = q_ref.shape[1] // tq

    k_all[nx:nx + nc, :] = kc_ref[0]
    v_all[0:dv, nx:nx + nc] = vc_ref[0].T.astype(BF16)
    v_all[dv:, :] = jnp.ones((v_all.shape[0] - dv, nx + nc), BF16)
    if with_x:
        chunk = min(512, nx)

        def fill(j, c):
            off = pl.multiple_of(j * chunk, chunk)
            k_all[pl.ds(off, chunk), :] = kx_ref[0, pl.ds(off, chunk), :]
            v_all[0:dv, pl.ds(off, chunk)] = vx_ref[0, pl.ds(off, chunk), :].T.astype(BF16)
            return c

        lax.fori_loop(0, nx // chunk, fill, 0)

    s_bufs, p_bufs, a_bufs = (s0_ref, s1_ref, s2_ref), (p0_ref, p1_ref, p2_ref), (a0_ref, a1_ref, a2_ref)
    lo = _lane_iota((tq, LANES)) < da_dim

    def buf(j):
        return 2 if (n_steps % 2 == 1 and n_steps > 1 and j == n_steps - 1) else j % 2

    def q_start(t):
        return t * tq if isinstance(t, int) else pl.multiple_of(t * tq, tq)

    def qk(t, j):
        q = q_ref[0, pl.ds(q_start(t), tq), :]
        zero = jnp.zeros_like(q)
        q2 = jnp.concatenate([jnp.where(lo, q, zero), jnp.where(lo, zero, q)], axis=0)
        s_bufs[buf(j)][...] = lax.dot_general(k_all[j * tk:(j + 1) * tk, :], q2,
                                              (((1,), (1,)), ((), ())), preferred_element_type=F32)

    def softmax(j):
        b = buf(j)
        s = s_bufs[b][...]
        col_max = jnp.max(s, axis=0, keepdims=True)
        if j == 0:
            m_new = col_max
        else:
            m_prev = m_ref[...]
            m_new = jnp.maximum(m_prev, col_max)
            a_bufs[b][...] = jnp.exp2(m_prev - m_new)
        p_bufs[b][...] = jnp.exp2(s - m_new).astype(BF16)
        m_ref[...] = m_new

    def pv(j):
        b = buf(j)
        r = jnp.dot(v_all[:, j * tk:(j + 1) * tk], p_bufs[b][...], preferred_element_type=F32)
        if j == 0:
            acc_ref[...] = r
        else:
            acc_ref[...] = a_bufs[b][...] * acc_ref[...] + r

    def finalize(t):
        acc = acc_ref[...]
        o0 = acc[0:dv, 0:tq] / acc[dv:dv + 1, 0:tq]
        o1 = acc[0:dv, tq:2 * tq] / acc[dv:dv + 1, tq:2 * tq]
        o = o0 - scal_ref[0] * o1
        y = o * lax.rsqrt(jnp.mean(o * o, axis=0, keepdims=True) + EPS) * gsub_ref[...]
        o_ref[0, pl.ds(q_start(t), tq), :] = (y * scal_ref[1]).T.astype(BF16)

    if n_steps == 1:
        def lone(t, c):
            qk(t, 0)
            softmax(0)
            pv(0)
            finalize(t)
            return c

        lax.fori_loop(0, n_tiles, lone, 0)
    else:
        qk(0, 0)
        last = buf(n_steps - 1)
        p_bufs[last][...] = jnp.zeros(p_bufs[last].shape, BF16)
        a_bufs[last][...] = jnp.ones(a_bufs[last].shape, F32)
        acc_ref[...] = jnp.ones(acc_ref.shape, F32)

        def tile(t, c):
            qk(t, 1)
            softmax(0)
            pv(n_steps - 1)
            finalize(jnp.maximum(t - 1, 0))
            for j in range(1, n_steps - 1):
                qk(t, j + 1)
                softmax(j)
                pv(j - 1)
            qk(jnp.minimum(t + 1, n_tiles - 1), 0)
            softmax(n_steps - 1)
            pv(n_steps - 2)
            return c

        lax.fori_loop(0, n_tiles, tile, 0)
        pv(n_steps - 1)
        finalize(n_tiles - 1)


def _attention(scal, q, kx, proj_x, kc, proj_c, g_sub, cols, *, with_x, da_dim):
    b, nq, d = q.shape
    dv = g_sub.shape[0]
    heads = d // dv
    nc = kc.shape[1]
    nx = kx.shape[1]
    tq = min(nq, 256)
    n_keys = (nx if with_x else 0) + nc
    tk = max(t for t in range(LANES, ATTN_MAX_KEY_TILE + 1, LANES) if n_keys % t == 0)
    assert nq % tq == 0 and 2 * da_dim == LANES and dv == LANES
    v_blk = cols["av"] // dv
    kern = functools.partial(_attn_kernel, tq=tq, tk=tk, with_x=with_x, da_dim=da_dim)
    rows = 2 * tq
    pipelined = 2 * nq * dv * 2 + nx * dv * (2 + 4) + nc * dv * (2 + 4)
    resident = (n_keys * 3 * dv * 2 + 3 * rows * tk * (4 + 2) + 4 * rows * LANES * 4
                + rows * 2 * dv * 4 + 3 * rows * tk * 4)
    return pl.pallas_call(
        kern,
        out_shape=jax.ShapeDtypeStruct((b, nq, d), BF16),
        grid=(b, heads),
        in_specs=[pl.BlockSpec(memory_space=pltpu.SMEM),
                  pl.BlockSpec((1, nq, dv), lambda bi, h: (bi, 0, h)),
                  pl.BlockSpec((1, nx, dv), lambda bi, h: (bi, 0, h)),
                  pl.BlockSpec((1, nx, dv), lambda bi, h: (bi, 0, v_blk + h)),
                  pl.BlockSpec((1, nc, dv), lambda bi, h: (bi, 0, h)),
                  pl.BlockSpec((1, nc, dv), lambda bi, h: (bi, 0, v_blk + h)),
                  pl.BlockSpec((dv, 1), lambda bi, h: (0, 0))],
        out_specs=pl.BlockSpec((1, nq, dv), lambda bi, h: (bi, 0, h)),
        scratch_shapes=[pltpu.VMEM((n_keys, dv), BF16), pltpu.VMEM((dv + ATTN_ONES_ROWS, n_keys), BF16)]
        + [pltpu.VMEM((tk, rows), F32)] * 3 + [pltpu.VMEM((tk, rows), BF16)] * 3
        + [pltpu.VMEM((1, rows), F32)] * 4 + [pltpu.VMEM((dv + ATTN_ONES_ROWS, rows), F32)],
        compiler_params=_params(("arbitrary", "arbitrary"), pipelined, resident),
        name="diff_attention" if with_x else "diff_attention_ctx",
    )(scal, q, kx, proj_x, kc, proj_c, g_sub.reshape(dv, 1))


CONV_HALO = 16
CONV_ROW_CHUNK = 64


def _conv_kernel(a_ref, g_ref, ap_ref, gp_ref, an_ref, gn_ref, w_ref, cb_ref, lg_ref, lb_ref,
                 o_ref, ext_ref, cv_ref, sh_ref):
    i = pl.program_id(1)
    last = pl.num_programs(1) - 1
    tm = a_ref.shape[1]
    ch = a_ref.shape[2]
    width = w_ref.shape[0]
    pad = width // 2

    def glu(a, g):
        return a * _sigmoid(g)

    ext_ref[0:CONV_HALO, :] = jnp.where(i > 0, glu(ap_ref[0], gp_ref[0]), 0.0)
    ext_ref[CONV_HALO:CONV_HALO + tm, :] = glu(a_ref[0], g_ref[0])
    ext_ref[CONV_HALO + tm:CONV_HALO + tm + CONV_HALO, :] = jnp.where(i < last, glu(an_ref[0], gn_ref[0]), 0.0)

    span = sh_ref.shape[1]
    for c in range(ch // LANES):
        sl = slice(c * LANES, (c + 1) * LANES)
        for s in range(SUBLANES):
            sh_ref[s] = ext_ref[s:s + span, sl]
        def rows(i, carry, sl=sl):
            r0 = pl.multiple_of(i * CONV_ROW_CHUNK, CONV_ROW_CHUNK)
            acc = jnp.zeros((CONV_ROW_CHUNK, LANES), F32)
            for j in range(width):
                start = CONV_HALO - pad + j
                base = start - start % SUBLANES
                acc = acc + w_ref[j:j + 1, sl] * sh_ref[start % SUBLANES, pl.ds(base + r0, CONV_ROW_CHUNK), :]
            cv_ref[pl.ds(r0, CONV_ROW_CHUNK), sl] = acc + cb_ref[:, sl]
            return carry

        lax.fori_loop(0, tm // CONV_ROW_CHUNK, rows, 0)

    v = cv_ref[...]
    mu = jnp.mean(v, axis=-1, keepdims=True)
    vc = v - mu
    var = jnp.mean(vc * vc, axis=-1, keepdims=True)
    y = vc * lax.rsqrt(var + EPS) * lg_ref[...] + lb_ref[...]
    o_ref[0] = (y * _sigmoid(y)).astype(BF16)


def _conformer_conv(proj, conv_w, conv_b, ln_g, ln_b, cols):
    b, n, _ = proj.shape
    width, ch = conv_w.shape
    tm = min(n, 512)
    assert n % tm == 0 and tm % CONV_HALO == 0 and width // 2 < CONV_HALO
    a_blk = cols["conv"] // ch
    g_blk = a_blk + 1
    r = tm // CONV_HALO
    n_halo = n // CONV_HALO

    def cur(blk):
        return pl.BlockSpec((1, tm, ch), lambda bi, i: (bi, i, blk))

    def prev(blk):
        return pl.BlockSpec((1, CONV_HALO, ch), lambda bi, i: (bi, jnp.maximum(i * r - 1, 0), blk))

    def nxt(blk):
        return pl.BlockSpec((1, CONV_HALO, ch), lambda bi, i: (bi, jnp.minimum((i + 1) * r, n_halo - 1), blk))

    vec = pl.BlockSpec((1, ch), lambda bi, i: (0, 0))
    return pl.pallas_call(
        _conv_kernel,
        out_shape=jax.ShapeDtypeStruct((b, n, ch), BF16),
        grid=(b, n // tm),
        in_specs=[cur(a_blk), cur(g_blk), prev(a_blk), prev(g_blk), nxt(a_blk), nxt(g_blk),
                  pl.BlockSpec((width, ch), lambda bi, i: (0, 0)), vec, vec, vec],
        out_specs=pl.BlockSpec((1, tm, ch), lambda bi, i: (bi, i, 0)),
        scratch_shapes=[pltpu.VMEM((tm + 2 * CONV_HALO, ch), F32), pltpu.VMEM((tm, ch), F32),
                        pltpu.VMEM((SUBLANES, tm + 2 * CONV_HALO - SUBLANES, LANES), F32)],
        compiler_params=_params(("arbitrary", "arbitrary"),
                                (2 * tm + 4 * CONV_HALO) * ch * 4 + tm * ch * 2 + 40 * ch * 4,
                                (2 * tm + 2 * CONV_HALO) * ch * 4 + 4 * tm * ch * 4
                                + SUBLANES * (tm + 2 * CONV_HALO) * LANES * 4),
        name="conformer_conv",
    )(proj, proj, proj, proj, proj, proj, conv_w, conv_b.reshape(1, ch), ln_g.reshape(1, ch),
      ln_b.reshape(1, ch))


def _ret_kernel(qf_ref, kf_ref, vf_ref, gf_ref, qb_ref, kb_ref, vb_ref, gb_ref, s0_ref,
                inner_ref, xi_ref, zeta_ref, gc_ref, gng_ref, gnb_ref,
                of_ref, ob_ref, sfin_ref, s_ref, *, kdim):
    i = pl.program_id(1)
    c = qf_ref.shape[1]
    dv = gng_ref.shape[1]
    n_pair = qf_ref.shape[2] // LANES

    @pl.when(i == 0)
    def _():
        s_ref[...] = s0_ref[0]

    lo = _lane_iota((c, LANES)) < kdim
    dirs = ((qf_ref, kf_ref, vf_ref, gf_ref, of_ref), (qb_ref, kb_ref, vb_ref, gb_ref, ob_ref))
    for d, (q_ref, k_ref, v_ref, g_ref, o_ref) in enumerate(dirs):
        for p in range(n_pair):
            sl = slice(p * LANES, (p + 1) * LANES)
            q = q_ref[0, :, sl]
            k = k_ref[0, :, sl]
            kz = k * zeta_ref[d, p]
            kb16 = k.astype(BF16)
            s_pair = s_ref[d, p]
            s16 = s_pair.astype(BF16)
            upd = jnp.zeros((LANES, dv), F32)
            for hh in range(2):
                h = 2 * p + hh
                hsl = slice(h * dv, (h + 1) * dv)
                keep = lo if hh == 0 else jnp.logical_not(lo)
                qh = jnp.where(keep, q, 0.0).astype(BF16)
                v16 = v_ref[0, :, hsl].astype(BF16)
                att = lax.dot_general(qh, kb16, (((1,), (1,)), ((), ())), preferred_element_type=F32)
                att = (att * inner_ref[d, h]).astype(BF16)
                o = (jnp.dot(att, v16, preferred_element_type=F32)
                     + jnp.dot(qh, s16, preferred_element_type=F32) * xi_ref[d, h])
                kzh = jnp.where(keep, kz, 0.0).astype(BF16)
                upd = upd + lax.dot_general(kzh, v16, (((0,), (0,)), ((), ())), preferred_element_type=F32)
                mu = jnp.mean(o, axis=-1, keepdims=True)
                oc = o - mu
                var = jnp.mean(oc * oc, axis=-1, keepdims=True)
                y = oc * lax.rsqrt(var + EPS) * gng_ref[...] + gnb_ref[...]
                g = g_ref[0, :, hsl]
                o_ref[0, :, hsl] = (g * _sigmoid(g)) * y
            s_ref[d, p] = s_pair * gc_ref[p] + upd

    sfin_ref[0] = s_ref[...]


def _retention(rq, rk, proj, s0, tabs, gn_g, gn_b, cols):
    b, n, rw = rq.shape
    dv = gn_g.shape[0]
    d = cols["d"]
    c = RET_CHUNK
    nch = n // c
    kdim = dv // 2
    assert n % c == 0 and c == LANES and 2 * kdim == LANES
    n_pair = rw // LANES
    inner, xi, zeta, gc = tabs
    v_blk, gf_blk, gb_blk = cols["rv"] // d, cols["rgf"] // d, cols["rgb"] // d

    def fwd(width, blk):
        return pl.BlockSpec((1, c, width), lambda bi, i: (bi, i, blk))

    def bwd(width, blk):
        return pl.BlockSpec((1, c, width), lambda bi, i: (bi, nch - 1 - i, blk))

    def whole(a):
        return pl.BlockSpec(a.shape, lambda bi, i: (0,) * a.ndim)

    state = pl.BlockSpec((1, 2, n_pair, LANES, dv), lambda bi, i: (bi, 0, 0, 0, 0))
    vec = pl.BlockSpec((1, dv), lambda bi, i: (0, 0))
    kern = functools.partial(_ret_kernel, kdim=kdim)
    pipelined = 2 * c * (2 * rw + 2 * d) * 4 + 2 * c * d * 4 + 2 * 2 * n_pair * LANES * dv * 4
    resident = (inner.size + xi.size + zeta.size + gc.size) * 4 * 2 + 2 * n_pair * LANES * dv * 4 + 8 * MIB
    return pl.pallas_call(
        kern,
        out_shape=(jax.ShapeDtypeStruct((b, n, d), F32), jax.ShapeDtypeStruct((b, n, d), F32),
                   jax.ShapeDtypeStruct((b, 2, n_pair, LANES, dv), F32)),
        grid=(b, nch),
        in_specs=[fwd(rw, 0), fwd(rw, 0), fwd(d, v_blk), fwd(d, gf_blk),
                  bwd(rw, 0), bwd(rw, 0), bwd(d, v_blk), bwd(d, gb_blk),
                  state, whole(inner), whole(xi), whole(zeta), whole(gc), vec, vec],
        out_specs=(pl.BlockSpec((1, c, d), lambda bi, i: (bi, i, 0)),
                   pl.BlockSpec((1, c, d), lambda bi, i: (bi, nch - 1 - i, 0)),
                   state),
        scratch_shapes=[pltpu.VMEM((2, n_pair, LANES, dv), F32)],
        compiler_params=_params(("arbitrary", "arbitrary"), pipelined, resident),
        name="retention",
    )(rq, rk, proj, proj, rq, rk, proj, proj, s0, inner, xi, zeta, gc,
      gn_g.reshape(1, dv), gn_b.reshape(1, dv))


def _retention_tables(heads, kdim, dv):
    c = RET_CHUNK
    lg = jnp.log1p(-jnp.exp2(-5.0 - jnp.arange(heads, dtype=F32)))
    pos = jnp.arange(c, dtype=F32)
    diff = pos[:, None] - pos[None, :]
    inner_f = jnp.where(diff[None] >= 0, jnp.exp(jnp.maximum(diff, 0.0)[None] * lg[:, None, None]), 0.0)
    inner = jnp.stack([inner_f, jnp.swapaxes(inner_f, 1, 2)])
    xi_f = jnp.exp((pos + 1.0)[None, :] * lg[:, None])
    zeta_f = jnp.exp((c - 1.0 - pos)[None, :] * lg[:, None])
    xi = jnp.stack([xi_f, xi_f[:, ::-1]])
    zeta = jnp.stack([zeta_f, zeta_f[:, ::-1]])
    xi = jnp.broadcast_to(xi[..., None], (2, heads, c, dv))
    zeta = jnp.broadcast_to(zeta[..., None], (2, heads, c, kdim))
    zeta = zeta.reshape(2, heads // 2, 2, c, kdim).transpose(0, 1, 3, 2, 4).reshape(2, heads // 2, c, 2 * kdim)
    gc = jnp.exp(c * lg)
    gc = jnp.broadcast_to(gc[:, None, None], (heads, kdim, dv)).reshape(heads // 2, 2 * kdim, dv)
    return inner.astype(F32), xi.astype(F32), zeta.astype(F32), gc.astype(F32)


def _merge_kernel(h_ref, mod_ref, oa_ref, ob_ref, of_ref, obk_ref, gate_ref, bg_ref,
                  wa_ref, wb_ref, wc_ref, wo_ref, o_ref):
    d = h_ref.shape[2]
    g = _sigmoid(gate_ref[0] + bg_ref[...])
    oc = (of_ref[0] + obk_ref[0]).astype(BF16)
    y = g[:, 0:d] * jnp.dot(oa_ref[0], wa_ref[...], preferred_element_type=F32)
    y = y + g[:, d:2 * d] * jnp.dot(ob_ref[0], wb_ref[...], preferred_element_type=F32)
    y = y + g[:, 2 * d:3 * d] * jnp.dot(oc, wc_ref[...], preferred_element_type=F32)
    m = jnp.dot(y.astype(BF16), wo_ref[...], preferred_element_type=F32)
    o_ref[0] = h_ref[0] + mod_ref[0, 2:3, :] * m


def _merge(h, mod, oa, ob, of, obk, proj, b_gate, w_pa, w_pb, w_pc, w_out, layer, cols):
    b, n, d = h.shape
    tm = min(n, 512)
    assert n % tm == 0
    mb = mod.shape[0]
    gate_blk = cols["gate"] // (3 * d)
    row = lambda width: pl.BlockSpec((1, tm, width), lambda bi, i: (bi, i, 0))
    wspec = pl.BlockSpec((None, d, d), lambda bi, i: (layer, 0, 0))
    return pl.pallas_call(
        _merge_kernel,
        out_shape=jax.ShapeDtypeStruct((b, n, d), F32),
        grid=(b, n // tm),
        in_specs=[row(d), pl.BlockSpec((1, N_MOD, d), lambda bi, i: (bi % mb, 0, 0)),
                  row(d), row(d), row(d), row(d),
                  pl.BlockSpec((1, tm, 3 * d), lambda bi, i: (bi, i, gate_blk)),
                  pl.BlockSpec((1, 3 * d), lambda bi, i: (0, 0)),
                  wspec, wspec, wspec, wspec],
        out_specs=row(d),
        compiler_params=_params(("arbitrary", "arbitrary"),
                                tm * d * (4 + 2 + 2 + 4 + 4 + 12 + 4) + 4 * d * d * 2, 10 * tm * d * 4),
        name="merge",
    )(h, mod, oa, ob, of, obk, proj, b_gate.reshape(1, 3 * d), w_pa, w_pb, w_pc, w_out)


ROUTE_E, ROUTE_G, ROUTE_R = 0, 4, 8


def _route_kernel(h_ref, mod_ref, g_ref, wr_ref, br_ref, tri_ref, base_ref,
                  hx_ref, route_ref, cnt_ref, run_ref, *, n_exp):
    first = jnp.logical_and(pl.program_id(0) == 0, pl.program_id(1) == 0)

    @pl.when(first)
    def _():
        run_ref[...] = base_ref[...]

    h = h_ref[0]
    tm = h.shape[0]
    y = h * lax.rsqrt(jnp.mean(h * h, axis=-1, keepdims=True) + EPS) * g_ref[...]
    hx = y * (1.0 + mod_ref[0, 4:5, :]) + mod_ref[0, 3:4, :]
    hx_ref[0] = hx

    lane = _lane_iota((tm, LANES))
    lane_f = lane.astype(F32)
    hx_hi = hx.astype(BF16)
    hx_lo = (hx - hx_hi.astype(F32)).astype(BF16)
    logits = (jnp.dot(hx_hi, wr_ref[0], preferred_element_type=F32)
              + jnp.dot(hx_hi, wr_ref[1], preferred_element_type=F32)
              + jnp.dot(hx_lo, wr_ref[0], preferred_element_type=F32)) + br_ref[...]
    logits = jnp.where(lane < n_exp, logits, NEG_INF)

    sels, vals = [], []
    for _ in range(TOP_K):
        mx = jnp.max(logits, axis=-1, keepdims=True)
        idx = jnp.min(jnp.where(logits == mx, lane_f, float(LANES)), axis=-1, keepdims=True)
        sel = lane_f == idx
        sels.append((sel, idx))
        vals.append(mx)
        logits = jnp.where(sel, NEG_INF, logits)

    es = [jnp.exp(v - vals[0]) for v in vals]
    denom = es[0]
    for e in es[1:]:
        denom = denom + e

    onehot = jnp.zeros((tm, LANES), F32)
    for sel, _ in sels:
        onehot = onehot + jnp.where(sel, 1.0, 0.0)
    before = run_ref[...] + jnp.dot(tri_ref[...], onehot.astype(BF16), preferred_element_type=F32)

    route = jnp.zeros((tm, LANES), F32)
    for k, (sel, idx) in enumerate(sels):
        rank = jnp.sum(jnp.where(sel, before, 0.0), axis=-1, keepdims=True)
        route = jnp.where(lane == ROUTE_E + k, idx, route)
        route = jnp.where(lane == ROUTE_G + k, es[k] / denom, route)
        route = jnp.where(lane == ROUTE_R + k, rank, route)
    route_ref[0] = route

    run_ref[...] = run_ref[...] + jnp.sum(onehot, axis=0, keepdims=True)
    cnt_ref[...] = run_ref[...]


def _route(h, mod, g, w_router, b_router, base):
    b, n, d = h.shape
    n_exp = w_router.shape[-1]
    tm = min(n, 256)
    assert n % tm == 0 and n_exp <= LANES
    mb = mod.shape[0]
    wr = jnp.zeros((d, LANES), F32).at[:, :n_exp].set(w_router)
    wr_hi = wr.astype(BF16)
    wr = jnp.stack([wr_hi, (wr - wr_hi.astype(F32)).astype(BF16)])
    br = jnp.zeros((1, LANES), F32).at[0, :n_exp].set(b_router)
    tri = (jnp.arange(tm)[:, None] > jnp.arange(tm)[None, :]).astype(BF16)
    row = lambda width: pl.BlockSpec((1, tm, width), lambda bi, i: (bi, i, 0))
    const = lambda shape: pl.BlockSpec(shape, lambda bi, i: (0,) * len(shape))
    kern = functools.partial(_route_kernel, n_exp=n_exp)
    return pl.pallas_call(
        kern,
        out_shape=(jax.ShapeDtypeStruct((b, n, d), F32), jax.ShapeDtypeStruct((b, n, LANES), F32),
                   jax.ShapeDtypeStruct((1, LANES), F32)),
        grid=(b, n // tm),
        in_specs=[row(d), pl.BlockSpec((1, N_MOD, d), lambda bi, i: (bi % mb, 0, 0)), const((1, d)),
                  const((2, d, LANES)), const((1, LANES)), const((tm, tm)), const((1, LANES))],
        out_specs=(row(d), row(LANES), const((1, LANES))),
        scratch_shapes=[pltpu.VMEM((1, LANES), F32)],
        compiler_params=_params(("arbitrary", "arbitrary"),
                                2 * tm * d * 4 + tm * LANES * 4 + d * LANES * 4 + tm * tm * 2, 8 * tm * d * 4),
        name="moe_route",
    )(h, mod, g.reshape(1, d), wr, br, tri, base)


DMA_ISSUE_UNROLL = 8


def _tile_dest(dest, tm):
    t = dest.shape[0]
    return dest.reshape(t // tm, tm, TOP_K).transpose(0, 2, 1).reshape(t // tm, 1, TOP_K * tm)


def _dispatch_kernel(dest_ref, hx_ref, xs_in_ref, xs_ref, sem):
    del xs_in_ref
    tm = hx_ref.shape[0]

    def copy(k, t):
        return pltpu.make_async_copy(hx_ref.at[pl.ds(t, 1)],
                                     xs_ref.at[pl.ds(dest_ref[0, 0, k * tm + t], 1)], sem)

    for t in range(tm):
        for k in range(TOP_K):
            copy(k, t).start(priority=(t * TOP_K + k) % 2)
    for k in range(TOP_K):
        def wait(t, c, k=k):
            copy(k, t).wait()
            return c

        lax.fori_loop(0, tm, wait, 0, unroll=DMA_ISSUE_UNROLL)


def _dispatch(hx_flat, dest, xs):
    t, d = hx_flat.shape
    tm = min(t, 256)
    assert t % tm == 0
    dest3 = _tile_dest(dest, tm)
    return pl.pallas_call(
        _dispatch_kernel,
        out_shape=jax.ShapeDtypeStruct(xs.shape, xs.dtype),
        grid=(t // tm,),
        in_specs=[pl.BlockSpec((1, 1, tm * TOP_K), lambda i: (i, 0, 0), memory_space=pltpu.SMEM),
                  pl.BlockSpec((tm, d), lambda i: (i, 0)),
                  pl.BlockSpec(memory_space=pl.ANY)],
        out_specs=pl.BlockSpec(memory_space=pl.ANY),
        scratch_shapes=[pltpu.SemaphoreType.DMA],
        input_output_aliases={2: 0},
        compiler_params=pltpu.CompilerParams(dimension_semantics=("arbitrary",),
                                             vmem_limit_bytes=_vmem_limit(tm * d * 4, 4 * MIB),
                                             has_side_effects=True),
        name="moe_dispatch",
    )(dest3, hx_flat, xs)


def _split_kernel(w_ref, sel_ref, g_ref, u_ref):
    blk = sel_ref.shape[0]
    half = blk // 2
    for i in range(w_ref.shape[1] // blk):
        r = jnp.dot(w_ref[:, i * blk:(i + 1) * blk].astype(BF16), sel_ref[...], preferred_element_type=F32)
        g_ref[:, i * half:(i + 1) * half] = r[:, :half].astype(BF16)
        u_ref[:, i * half:(i + 1) * half] = r[:, half:].astype(BF16)


def _split_gate_up(w_gate_up):
    depth, n_exp, d, de2 = w_gate_up.shape
    de = de2 // 2
    tr = min(d, 512)
    assert d % tr == 0
    blk = 2 * LANES
    assert de2 % blk == 0
    order = jnp.concatenate([jnp.arange(0, blk, 2), jnp.arange(1, blk, 2)])
    sel = (jnp.arange(blk)[:, None] == order[None, :]).astype(BF16)
    out = jax.ShapeDtypeStruct((depth, n_exp, d, de), BF16)
    ospec = pl.BlockSpec((None, None, tr, de), lambda l, e, r: (l, e, r, 0))
    return pl.pallas_call(
        _split_kernel,
        out_shape=(out, out),
        grid=(depth, n_exp, d // tr),
        in_specs=[pl.BlockSpec((None, None, tr, de2), lambda l, e, r: (l, e, r, 0)),
                  pl.BlockSpec((blk, blk), lambda l, e, r: (0, 0))],
        out_specs=(ospec, ospec),
        compiler_params=_params(("arbitrary", "arbitrary", "arbitrary"),
                                tr * de2 * 4 + blk * blk * 2 + 2 * tr * de * 2, tr * de2 * (2 + 4 + 4)),
        name="split_gate_up",
    )(w_gate_up, sel)


def _expert_kernel(blk_e_ref, n_used_ref, x_ref, wg_ref, wu_ref, wd_ref, bg_ref, bu_ref, bd_ref, y_ref):
    del blk_e_ref
    used = pl.program_id(0) < n_used_ref[0]

    @pl.when(used)
    def _():
        x = x_ref[...].astype(BF16)
        gl = jnp.dot(x, wg_ref[...], preferred_element_type=F32) + bg_ref[...]
        up = jnp.dot(x, wu_ref[...], preferred_element_type=F32) + bu_ref[...]
        gl = jnp.minimum(gl, SWIGLU_LIMIT)
        up = jnp.clip(up, -SWIGLU_LIMIT, SWIGLU_LIMIT)
        act = (up + 1.0) * (gl * _sigmoid(SWIGLU_ALPHA * gl))
        y_ref[...] = jnp.dot(act.astype(BF16), wd_ref[...], preferred_element_type=F32) + bd_ref[...]

    @pl.when(jnp.logical_not(used))
    def _():
        y_ref[...] = jnp.zeros(y_ref.shape, F32)


def _experts(blk_e, n_used, xs, wg, wu, wd, bg, bu, bd, layer, bm):
    n_rows, d = xs.shape
    de = wg.shape[-1]
    nb = n_rows // bm
    wspec = lambda a, c: pl.BlockSpec((None, None, a, c), lambda i, be, nu: (layer, be[i], 0, 0))
    return pl.pallas_call(
        _expert_kernel,
        out_shape=jax.ShapeDtypeStruct((n_rows, d), F32),
        grid_spec=pltpu.PrefetchScalarGridSpec(
            num_scalar_prefetch=2,
            grid=(nb,),
            in_specs=[pl.BlockSpec((bm, d), lambda i, be, nu: (i, 0)),
                      wspec(d, de), wspec(d, de), wspec(de, d),
                      wspec(1, de), wspec(1, de), wspec(1, d)],
            out_specs=pl.BlockSpec((bm, d), lambda i, be, nu: (i, 0))),
        compiler_params=_params(("arbitrary",), 2 * bm * d * 4 + 3 * d * de * 2, 8 * bm * de * 4),
        name="moe_experts",
    )(blk_e, n_used, xs, wg, wu, wd, bg, bu, bd)


def _combine_kernel(dest_ref, dnext_ref, h_ref, mod_ref, route_ref, y_ref, o_ref, ybuf0, ybuf1, sem):
    tm = h_ref.shape[1]
    n_copy = TOP_K * tm
    step = pl.program_id(0) * pl.num_programs(1) + pl.program_id(1)
    n_step = pl.num_programs(0) * pl.num_programs(1)
    bufs = (ybuf0, ybuf1)

    def copy(idx_ref, r, slot):
        return pltpu.make_async_copy(y_ref.at[pl.ds(idx_ref[0, 0, r], 1)], bufs[slot].at[pl.ds(r, 1)],
                                     sem.at[slot])

    def fetch(idx_ref, slot):
        for r in range(n_copy):
            copy(idx_ref, r, slot).start(priority=r % 2)

    def finish(slot):
        def wait(r, c):
            copy(dest_ref, r, slot).wait()
            return c

        lax.fori_loop(0, n_copy, wait, 0, unroll=DMA_ISSUE_UNROLL)
        route = route_ref[0]
        f = jnp.zeros((tm, h_ref.shape[2]), F32)
        for k in range(TOP_K):
            f = f + route[:, ROUTE_G + k:ROUTE_G + k + 1] * bufs[slot][k * tm:(k + 1) * tm, :]
        o_ref[0] = h_ref[0] + mod_ref[0, 5:6, :] * f

    @pl.when(step == 0)
    def _():
        fetch(dest_ref, 0)

    for slot in range(2):
        @pl.when(step % 2 == slot)
        def _(slot=slot):
            @pl.when(step + 1 < n_step)
            def _():
                fetch(dnext_ref, 1 - slot)

            finish(slot)


def _combine(h, mod, route, dest, y):
    b, n, d = h.shape
    tm = min(n, 256)
    assert n % tm == 0
    mb = mod.shape[0]
    nt = n // tm
    dest3 = _tile_dest(dest, tm)
    row = lambda width: pl.BlockSpec((1, tm, width), lambda bi, i: (bi, i, 0))
    return pl.pallas_call(
        _combine_kernel,
        out_shape=jax.ShapeDtypeStruct((b, n, d), F32),
        grid=(b, nt),
        in_specs=[pl.BlockSpec((1, 1, tm * TOP_K), lambda bi, i: (bi * nt + i, 0, 0), memory_space=pltpu.SMEM),
                  pl.BlockSpec((1, 1, tm * TOP_K), lambda bi, i: (jnp.minimum(bi * nt + i + 1, b * nt - 1), 0, 0),
                               memory_space=pltpu.SMEM),
                  row(d), pl.BlockSpec((1, N_MOD, d), lambda bi, i: (bi % mb, 0, 0)), row(LANES),
                  pl.BlockSpec(memory_space=pl.ANY)],
        out_specs=row(d),
        scratch_shapes=[pltpu.VMEM((TOP_K * tm, d), F32), pltpu.VMEM((TOP_K * tm, d), F32),
                        pltpu.SemaphoreType.DMA((2,))],
        compiler_params=_params(("arbitrary", "arbitrary"), 2 * tm * d * 4 + tm * LANES * 4,
                                2 * TOP_K * tm * d * 4 + 4 * tm * d * 4),
        name="moe_combine",
    )(dest3, dest3, h, mod, route, y)


def _moe_rows(n_tok, n_exp, bm):
    return -(-(n_tok * TOP_K + n_exp * (bm - 1)) // bm) * bm


def _moe(parts, xs, g, w_router, b_router, wg, wu, wd, bg, bu, bd, layer, bm):
    n_exp = w_router.shape[-1]
    n_rows, d = xs.shape
    base = jnp.zeros((1, LANES), F32)
    hxs, routes = [], []
    for h, mod in parts:
        hx, route, base = _route(h, mod, g, w_router, b_router, base)
        hxs.append(hx)
        routes.append(route)
    counts = base[0, :n_exp].astype(jnp.int32)

    padded = (counts + bm - 1) // bm * bm
    pend = jnp.cumsum(padded)
    pstart = pend - padded
    n_tok = sum(h.shape[0] * h.shape[1] for h, _ in parts)
    assert n_rows >= _moe_rows(n_tok, n_exp, bm) and n_rows % bm == 0
    nb = n_rows // bm
    blk_start = jnp.arange(nb, dtype=jnp.int32) * bm
    blk_e = jnp.minimum(jnp.sum((pend[None, :] <= blk_start[:, None]).astype(jnp.int32), axis=1), n_exp - 1)
    n_used = (pend[-1] // bm).astype(jnp.int32).reshape(1)
    eids = jnp.arange(n_exp, dtype=jnp.int32)

    dests = []
    for hx, route in zip(hxs, routes):
        e = route[..., ROUTE_E:ROUTE_E + TOP_K].astype(jnp.int32)
        rank = route[..., ROUTE_R:ROUTE_R + TOP_K].astype(jnp.int32)
        start = jnp.sum(jnp.where(e[..., None] == eids, pstart, 0), axis=-1)
        dest = (start + rank).reshape(-1, TOP_K)
        dests.append(dest)
        xs = _dispatch(hx.reshape(-1, d), dest, xs)

    y = _experts(blk_e, n_used, xs, wg, wu, wd, bg, bu, bd, layer, bm)
    return [_combine(h, mod, route, dest, y)
            for (h, mod), route, dest in zip(parts, routes, dests)], xs


def kernel(x, c, ctx, c_ctx, w_mod, b_mod, g_norm1, g_norm2, w_in, b_gate, g_qa, g_ka, lam_q1, lam_k1, lam_q2, lam_k2, g_sub, conv_w, conv_b, ln_g, ln_b, gn_g, gn_b, w_pa, w_pb, w_pc, w_out, w_router, b_router, w_gate_up, b_gate_up, w_down, b_down):
    bsz, n_x, d = x.shape
    n_c = ctx.shape[1]
    depth = w_mod.shape[0]
    da_dim = g_qa.shape[-1]
    dv = g_sub.shape[-1]
    heads = d // dv
    ret_kdim = gn_g.shape[-1] // 2
    n_exp = w_router.shape[-1]
    de = w_down.shape[-2]
    moe_bm = 512

    cols = {"d": d, "ret_qk": heads * ret_kdim}
    off = 0
    for name, width in (("aq", d), ("ak", d), ("av", d), ("conv", 2 * d), ("rq", heads * ret_kdim),
                        ("rk", heads * ret_kdim), ("rv", d), ("rgf", d), ("rgb", d), ("gate", 3 * d)):
        cols[name] = off
        off += width
    assert off == w_in.shape[-1]

    w_in_bf = w_in.astype(BF16)
    w_pa_bf, w_pb_bf, w_pc_bf, w_out_bf = (w.astype(BF16) for w in (w_pa, w_pb, w_pc, w_out))
    wg, wu = _split_gate_up(w_gate_up)
    wd = w_down.astype(BF16)
    bg = b_gate_up[..., 0::2].reshape(depth, n_exp, 1, de)
    bu = b_gate_up[..., 1::2].reshape(depth, n_exp, 1, de)
    bd = b_down.reshape(depth, n_exp, 1, d)

    rows = -(-(bsz + 1) // SUBLANES) * SUBLANES
    cvecs = jnp.zeros((rows, d), F32).at[:bsz].set(c).at[bsz].set(c_ctx)
    mods = _modulation(cvecs, w_mod, b_mod).reshape(depth, rows, N_MOD, d)

    tabs_x = _rope_tables(n_x, da_dim, ret_kdim)
    tabs_c = tuple(jnp.zeros((n_c, LANES), F32) for _ in range(4))
    ret_tabs = _retention_tables(heads, ret_kdim, dv)
    s_zero = jnp.zeros((bsz, 2, heads // 2, LANES, dv), F32)
    xs_rows = jnp.zeros((_moe_rows(bsz * (n_c + n_x), n_exp, moe_bm), d), F32)

    h_ctx = ctx
    for l in range(depth):
        need_ctx = l < depth - 1
        lam_init = 0.8 - 0.6 * math.exp(-0.3 * l)
        lam = (jnp.exp(jnp.sum(lam_q1[l] * lam_k1[l])) - jnp.exp(jnp.sum(lam_q2[l] * lam_k2[l])) + lam_init)
        scal = jnp.stack([lam, jnp.asarray(1.0 - lam_init, F32)]).astype(F32)
        mod_x = mods[l, :bsz]
        mod_c = mods[l, bsz:bsz + 1]

        proj_x = _in_proj(x, mod_x, g_norm1[l], w_in_bf, l)
        proj_c = _in_proj(h_ctx, mod_c, g_norm1[l], w_in_bf, l)
        aq_x, ak_x, rq_x, rk_x = _qk_prep(proj_x, g_qa[l], g_ka[l], tabs_x, cols, rope=True,
                                          da_dim=da_dim, ret_kdim=ret_kdim)
        aq_c, ak_c, rq_c, rk_c = _qk_prep(proj_c, g_qa[l], g_ka[l], tabs_c, cols, rope=False,
                                          da_dim=da_dim, ret_kdim=ret_kdim)
        oa_x = _attention(scal, aq_x, ak_x, proj_x, ak_c, proj_c, g_sub[l], cols, with_x=True, da_dim=da_dim)
        ob_x = _conformer_conv(proj_x, conv_w[l], conv_b[l], ln_g[l], ln_b[l], cols)
        of_c, ob_c, s_ctx = _retention(rq_c, rk_c, proj_c, s_zero, ret_tabs, gn_g[l], gn_b[l], cols)
        of_x, obk_x, _ = _retention(rq_x, rk_x, proj_x, s_ctx, ret_tabs, gn_g[l], gn_b[l], cols)
        x = _merge(x, mod_x, oa_x, ob_x, of_x, obk_x, proj_x, b_gate[l],
                   w_pa_bf, w_pb_bf, w_pc_bf, w_out_bf, l, cols)
        if need_ctx:
            oa_c = _attention(scal, aq_c, ak_c, proj_c, ak_c, proj_c, g_sub[l], cols, with_x=False, da_dim=da_dim)
            cb_c = _conformer_conv(proj_c, conv_w[l], conv_b[l], ln_g[l], ln_b[l], cols)
            h_ctx = _merge(h_ctx, mod_c, oa_c, cb_c, of_c, ob_c, proj_c, b_gate[l],
                           w_pa_bf, w_pb_bf, w_pc_bf, w_out_bf, l, cols)

        parts = [(h_ctx, mod_c), (x, mod_x)] if need_ctx else [(x, mod_x)]
        outs, xs_rows = _moe(parts, xs_rows, g_norm2[l], w_router[l], b_router[l], wg, wu, wd, bg, bu, bd,
                             l, moe_bm)
        if need_ctx:
            h_ctx, x = outs
        else:
            (x,) = outs
    return x
```

```python
import functools
import math

import jax
import jax.numpy as jnp
from jax import lax
from jax.experimental import pallas as pl
from jax.experimental.pallas import tpu as pltpu

F32 = jnp.float32
BF16 = jnp.bfloat16

GRID_W = 64
N_MOD = 6
EPS = 1e-6
ROPE_BASE = 10000.0
RET_CHUNK = 128
TOP_K = 4
SWIGLU_ALPHA = 1.702
SWIGLU_LIMIT = 7.0

LANES = 128
SUBLANES = 8
MIB = 2**20
V7X_VMEM_BYTES = 64 * MIB
V7X_VMEM_COMPILER_RESERVE = 8 * MIB
VMEM_LIMIT_CAP = V7X_VMEM_BYTES - V7X_VMEM_COMPILER_RESERVE
VMEM_LIMIT_FLOOR = 16 * MIB

NEG_INF = float("-inf")
LOG2_E = math.log2(math.e)


def _sigmoid(x):
    return 1.0 / (1.0 + jnp.exp(-x))


def _vmem_limit(pipelined_bytes, resident_bytes):
    need = 2 * pipelined_bytes + resident_bytes
    return int(min(max(need, VMEM_LIMIT_FLOOR), VMEM_LIMIT_CAP))


def _params(sem, pipelined_bytes, resident_bytes):
    return pltpu.CompilerParams(dimension_semantics=sem,
                                vmem_limit_bytes=_vmem_limit(pipelined_bytes, resident_bytes))


def _lane_iota(shape):
    return lax.broadcasted_iota(jnp.int32, shape, len(shape) - 1)


def _mod_kernel(c_ref, w_ref, b_ref, o_ref):
    c = c_ref[...]
    a = (c * _sigmoid(c)).astype(BF16)
    o_ref[...] = jnp.dot(a, w_ref[...].astype(BF16), preferred_element_type=F32) + b_ref[...]


def _modulation(cvecs, w_mod, b_mod):
    depth, d, cols = w_mod.shape
    rows = cvecs.shape[0]
    tn = 1536
    assert cols % tn == 0
    return pl.pallas_call(
        _mod_kernel,
        out_shape=jax.ShapeDtypeStruct((depth, rows, cols), F32),
        grid=(depth, cols // tn),
        in_specs=[pl.BlockSpec((rows, d), lambda l, j: (0, 0)),
                  pl.BlockSpec((None, d, tn), lambda l, j: (l, 0, j)),
                  pl.BlockSpec((None, 1, tn), lambda l, j: (l, 0, j))],
        out_specs=pl.BlockSpec((None, rows, tn), lambda l, j: (l, 0, j)),
        compiler_params=_params(("arbitrary", "arbitrary"), d * tn * 4 + rows * tn * 4, 4 * MIB),
        name="modulation",
    )(cvecs, w_mod, b_mod.reshape(depth, 1, cols))


def _inproj_kernel(h_ref, mod_ref, g_ref, w_ref, o_ref, xs_ref):
    @pl.when(pl.program_id(2) == 0)
    def _():
        h = h_ref[0]
        y = h * lax.rsqrt(jnp.mean(h * h, axis=-1, keepdims=True) + EPS) * g_ref[...]
        xs_ref[...] = (y * (1.0 + mod_ref[0, 1:2, :]) + mod_ref[0, 0:1, :]).astype(BF16)

    o_ref[0] = jnp.dot(xs_ref[...], w_ref[...], preferred_element_type=F32)


def _in_proj(h, mod, g, w_in_bf, layer):
    b, n, d = h.shape
    cols = w_in_bf.shape[-1]
    tm = min(n, 1024)
    tn = 3072
    assert n % tm == 0 and cols % tn == 0
    mb = mod.shape[0]
    return pl.pallas_call(
        _inproj_kernel,
        out_shape=jax.ShapeDtypeStruct((b, n, cols), F32),
        grid=(b, n // tm, cols // tn),
        in_specs=[pl.BlockSpec((1, tm, d), lambda bi, i, j: (bi, i, 0)),
                  pl.BlockSpec((1, N_MOD, d), lambda bi, i, j: (bi % mb, 0, 0)),
                  pl.BlockSpec((1, d), lambda bi, i, j: (0, 0)),
                  pl.BlockSpec((None, d, tn), lambda bi, i, j: (layer, 0, j))],
        out_specs=pl.BlockSpec((1, tm, tn), lambda bi, i, j: (bi, i, j)),
        scratch_shapes=[pltpu.VMEM((tm, d), BF16)],
        compiler_params=_params(("arbitrary", "arbitrary", "arbitrary"),
                                tm * d * 4 + d * tn * 2 + tm * tn * 4, tm * d * 2 + 3 * tm * d * 4),
        name="in_proj",
    )(h, mod, g.reshape(1, d), w_in_bf)


def _prep_kernel(aq_ref, ak_ref, rq_ref, rk_ref, gq_ref, gk_ref, ca_ref, sa_ref, cr_ref, sr_ref,
                 oq_ref, ok_ref, orq_ref, ork_ref, *, rope, da_dim, ret_kdim):
    tm = aq_ref.shape[1]
    lane = _lane_iota((tm, LANES))

    same_group = (lax.broadcasted_iota(jnp.int32, (LANES, LANES), 0) // da_dim
                  == lax.broadcasted_iota(jnp.int32, (LANES, LANES), 1) // da_dim)
    group_ones = jnp.where(same_group, 1.0, 0.0).astype(BF16)

    def rms(x, g):
        x2 = x * x
        hi = x2.astype(BF16)
        rest = (x2 - hi.astype(F32)).astype(BF16)
        ss = (jnp.dot(hi, group_ones, preferred_element_type=F32)
              + jnp.dot(rest, group_ones, preferred_element_type=F32))
        return x * lax.rsqrt(ss * (1.0 / da_dim) + EPS) * g

    def rot(x, c, s, half):
        first = (lane % (2 * half)) < half
        partner = jnp.where(first, pltpu.roll(x, LANES - half, 1), pltpu.roll(x, half, 1))
        return x * c + partner * s

    q_scale = da_dim ** -0.5 * LOG2_E
    k_scale = ret_kdim ** -0.5
    for j in range(aq_ref.shape[2] // LANES):
        sl = slice(j * LANES, (j + 1) * LANES)
        q = rms(aq_ref[0, :, sl], gq_ref[...])
        k = rms(ak_ref[0, :, sl], gk_ref[...])
        if rope:
            q = rot(q, ca_ref[...], sa_ref[...], da_dim // 4)
            k = rot(k, ca_ref[...], sa_ref[...], da_dim // 4)
        oq_ref[0, :, sl] = (q * q_scale).astype(BF16)
        ok_ref[0, :, sl] = k.astype(BF16)
    for j in range(rq_ref.shape[2] // LANES):
        sl = slice(j * LANES, (j + 1) * LANES)
        q = rq_ref[0, :, sl]
        k = rk_ref[0, :, sl]
        if rope:
            q = rot(q, cr_ref[...], sr_ref[...], ret_kdim // 2)
            k = rot(k, cr_ref[...], sr_ref[...], ret_kdim // 2)
        orq_ref[0, :, sl] = q
        ork_ref[0, :, sl] = k * k_scale


def _qk_prep(proj, g_q, g_k, tabs, cols, *, rope, da_dim, ret_kdim):
    b, n, _ = proj.shape
    d = cols["d"]
    rw = cols["ret_qk"]
    tm = min(n, 1024)
    assert n % tm == 0
    gq = jnp.tile(g_q, LANES // da_dim).reshape(1, LANES)
    gk = jnp.tile(g_k, LANES // da_dim).reshape(1, LANES)
    tab_spec = pl.BlockSpec((tm, LANES), lambda bi, i: (i, 0))
    vec_spec = pl.BlockSpec((1, LANES), lambda bi, i: (0, 0))

    def col_spec(width, off):
        assert off % width == 0
        return pl.BlockSpec((1, tm, width), lambda bi, i: (bi, i, off // width))

    kern = functools.partial(_prep_kernel, rope=rope, da_dim=da_dim, ret_kdim=ret_kdim)
    return pl.pallas_call(
        kern,
        out_shape=(jax.ShapeDtypeStruct((b, n, d), BF16), jax.ShapeDtypeStruct((b, n, d), BF16),
                   jax.ShapeDtypeStruct((b, n, rw), F32), jax.ShapeDtypeStruct((b, n, rw), F32)),
        grid=(b, n // tm),
        in_specs=[col_spec(d, cols["aq"]), col_spec(d, cols["ak"]),
                  col_spec(rw, cols["rq"]), col_spec(rw, cols["rk"]),
                  vec_spec, vec_spec, tab_spec, tab_spec, tab_spec, tab_spec],
        out_specs=(pl.BlockSpec((1, tm, d), lambda bi, i: (bi, i, 0)),
                   pl.BlockSpec((1, tm, d), lambda bi, i: (bi, i, 0)),
                   pl.BlockSpec((1, tm, rw), lambda bi, i: (bi, i, 0)),
                   pl.BlockSpec((1, tm, rw), lambda bi, i: (bi, i, 0))),
        compiler_params=_params(("arbitrary", "arbitrary"),
                                tm * (2 * d + 2 * rw) * 4 + tm * (2 * d * 2 + 2 * rw * 4)
                                + 4 * tm * LANES * 4, 8 * MIB),
        name="qk_prep",
    )(proj, proj, proj, proj, gq, gk, *tabs)


def _rope_tables(n_tok, da_dim, ret_kdim):
    t = jnp.arange(n_tok)

    def angles(pos, dim):
        inv = ROPE_BASE ** (-jnp.arange(0, dim, 2, dtype=F32) / dim)
        return pos.astype(F32)[:, None] * inv[None, :]

    def cs(ang):
        c, s = jnp.cos(ang), jnp.sin(ang)
        return jnp.concatenate([c, c], axis=-1), jnp.concatenate([-s, s], axis=-1)

    half = da_dim // 2
    cr_, sr_ = cs(angles(t // GRID_W, half))
    cc_, sc_ = cs(angles(t % GRID_W, half))
    ca = jnp.tile(jnp.concatenate([cr_, cc_], axis=-1), (1, LANES // da_dim))
    sa = jnp.tile(jnp.concatenate([sr_, sc_], axis=-1), (1, LANES // da_dim))
    c1, s1 = cs(angles(t, ret_kdim))
    cr = jnp.tile(c1, (1, LANES // ret_kdim))
    sr = jnp.tile(s1, (1, LANES // ret_kdim))
    return ca, sa, cr, sr


ATTN_MAX_KEY_TILE = 768
ATTN_ONES_ROWS = 16


def _attn_kernel(scal_ref, q_ref, kx_ref, vx_ref, kc_ref, vc_ref, gsub_ref, o_ref,
                 k_all, v_all, s0_ref, s1_ref, s2_ref, p0_ref, p1_ref, p2_ref, a0_ref, a1_ref, a2_ref,
                 m_ref, acc_ref, *, tq, tk, with_x, da_dim):
    dv = vc_ref.shape[2]
    nc = kc_ref.shape[1]
    nx = kx_ref.shape[1] if with_x else 0
    n_steps = (nx + nc) // tk
    n_tiles = q_ref.shape[1] // tq

    k_all[nx:nx + nc, :] = kc_ref[0]
    v_all[0:dv, nx:nx + nc] = vc_ref[0].T.astype(BF16)
    v_all[dv:, :] = jnp.ones((v_all.shape[0] - dv, nx + nc), BF16)
    if with_x:
        chunk = min(512, nx)

        def fill(j, c):
            off = pl.multiple_of(j * chunk, chunk)
            k_all[pl.ds(off, chunk), :] = kx_ref[0, pl.ds(off, chunk), :]
            v_all[0:dv, pl.ds(off, chunk)] = vx_ref[0, pl.ds(off, chunk), :].T.astype(BF16)
            return c

        lax.fori_loop(0, nx // chunk, fill, 0)

    s_bufs, p_bufs, a_bufs = (s0_ref, s1_ref, s2_ref), (p0_ref, p1_ref, p2_ref), (a0_ref, a1_ref, a2_ref)
    lo = _lane_iota((tq, LANES)) < da_dim

    def buf(j):
        return 2 if (n_steps % 2 == 1 and n_steps > 1 and j == n_steps - 1) else j % 2

    def q_start(t):
        return t * tq if isinstance(t, int) else pl.multiple_of(t * tq, tq)

    def qk(t, j):
        q = q_ref[0, pl.ds(q_start(t), tq), :]
        zero = jnp.zeros_like(q)
        q2 = jnp.concatenate([jnp.where(lo, q, zero), jnp.where(lo, zero, q)], axis=0)
        s_bufs[buf(j)][...] = lax.dot_general(k_all[j * tk:(j + 1) * tk, :], q2,
                                              (((1,), (1,)), ((), ())), preferred_element_type=F32)

    def softmax(j):
        b = buf(j)
        s = s_bufs[b][...]
        col_max = jnp.max(s, axis=0, keepdims=True)
        if j == 0:
            m_new = col_max
        else:
            m_prev = m_ref[...]
            m_new = jnp.maximum(m_prev, col_max)
            a_bufs[b][...] = jnp.exp2(m_prev - m_new)
        p_bufs[b][...] = jnp.exp2(s - m_new).astype(BF16)
        m_ref[...] = m_new

    def pv(j):
        b = buf(j)
        r = jnp.dot(v_all[:, j * tk:(j + 1) * tk], p_bufs[b][...], preferred_element_type=F32)
        if j == 0:
            acc_ref[...] = r
        else:
            acc_ref[...] = a_bufs[b][...] * acc_ref[...] + r

    def finalize(t):
        acc = acc_ref[...]
        o0 = acc[0:dv, 0:tq] / acc[dv:dv + 1, 0:tq]
        o1 = acc[0:dv, tq:2 * tq] / acc[dv:dv + 1, tq:2 * tq]
        o = o0 - scal_ref[0] * o1
        y = o * lax.rsqrt(jnp.mean(o * o, axis=0, keepdims=True) + EPS) * gsub_ref[...]
        o_ref[0, pl.ds(q_start(t), tq), :] = (y * scal_ref[1]).T.astype(BF16)

    if n_steps == 1:
        def lone(t, c):
            qk(t, 0)
            softmax(0)
            pv(0)
            finalize(t)
            return c

        lax.fori_loop(0, n_tiles, lone, 0)
    else:
        qk(0, 0)
        last = buf(n_steps - 1)
        p_bufs[last][...] = jnp.zeros(p_bufs[last].shape, BF16)
        a_bufs[last][...] = jnp.ones(a_bufs[last].shape, F32)
        acc_ref[...] = jnp.ones(acc_ref.shape, F32)

        def tile(t, c):
            qk(t, 1)
            softmax(0)
            pv(n_steps - 1)
            finalize(jnp.maximum(t - 1, 0))
            for j in range(1, n_steps - 1):
                qk(t, j + 1)
                softmax(j)
                pv(j - 1)
            qk(jnp.minimum(t + 1, n_tiles - 1), 0)
            softmax(n_steps - 1)
            pv(n_steps - 2)
            return c

        lax.fori_loop(0, n_tiles, tile, 0)
        pv(n_steps - 1)
        finalize(n_tiles - 1)


def _attention(scal, q, kx, proj_x, kc, proj_c, g_sub, cols, *, with_x, da_dim):
    b, nq, d = q.shape
    dv = g_sub.shape[0]
    heads = d // dv
    nc = kc.shape[1]
    nx = kx.shape[1]
    tq = min(nq, 256)
    n_keys = (nx if with_x else 0) + nc
    tk = max(t for t in range(LANES, ATTN_MAX_KEY_TILE + 1, LANES) if n_keys % t == 0)
    assert nq % tq == 0 and 2 * da_dim == LANES and dv == LANES
    v_blk = cols["av"] // dv
    kern = functools.partial(_attn_kernel, tq=tq, tk=tk, with_x=with_x, da_dim=da_dim)
    rows = 2 * tq
    pipelined = 2 * nq * dv * 2 + nx * dv * (2 + 4) + nc * dv * (2 + 4)
    resident = (n_keys * 3 * dv * 2 + 3 * rows * tk * (4 + 2) + 4 * rows * LANES * 4
                + rows * 2 * dv * 4 + 3 * rows * tk * 4)
    return pl.pallas_call(
        kern,
        out_shape=jax.ShapeDtypeStruct((b, nq, d), BF16),
        grid=(b, heads),
        in_specs=[pl.BlockSpec(memory_space=pltpu.SMEM),
                  pl.BlockSpec((1, nq, dv), lambda bi, h: (bi, 0, h)),
                  pl.BlockSpec((1, nx, dv), lambda bi, h: (bi, 0, h)),
                  pl.BlockSpec((1, nx, dv), lambda bi, h: (bi, 0, v_blk + h)),
                  pl.BlockSpec((1, nc, dv), lambda bi, h: (bi, 0, h)),
                  pl.BlockSpec((1, nc, dv), lambda bi, h: (bi, 0, v_blk + h)),
                  pl.BlockSpec((dv, 1), lambda bi, h: (0, 0))],
        out_specs=pl.BlockSpec((1, nq, dv), lambda bi, h: (bi, 0, h)),
        scratch_shapes=[pltpu.VMEM((n_keys, dv), BF16), pltpu.VMEM((dv + ATTN_ONES_ROWS, n_keys), BF16)]
        + [pltpu.VMEM((tk, rows), F32)] * 3 + [pltpu.VMEM((tk, rows), BF16)] * 3
        + [pltpu.VMEM((1, rows), F32)] * 4 + [pltpu.VMEM((dv + ATTN_ONES_ROWS, rows), F32)],
        compiler_params=_params(("arbitrary", "arbitrary"), pipelined, resident),
        name="diff_attention" if with_x else "diff_attention_ctx",
    )(scal, q, kx, proj_x, kc, proj_c, g_sub.reshape(dv, 1))


CONV_HALO = 16
CONV_ROW_CHUNK = 64


def _conv_kernel(a_ref, g_ref, ap_ref, gp_ref, an_ref, gn_ref, w_ref, cb_ref, lg_ref, lb_ref,
                 o_ref, ext_ref, cv_ref, sh_ref):
    i = pl.program_id(1)
    last = pl.num_programs(1) - 1
    tm = a_ref.shape[1]
    ch = a_ref.shape[2]
    width = w_ref.shape[0]
    pad = width // 2

    def glu(a, g):
        return a * _sigmoid(g)

    ext_ref[0:CONV_HALO, :] = jnp.where(i > 0, glu(ap_ref[0], gp_ref[0]), 0.0)
    ext_ref[CONV_HALO:CONV_HALO + tm, :] = glu(a_ref[0], g_ref[0])
    ext_ref[CONV_HALO + tm:CONV_HALO + tm + CONV_HALO, :] = jnp.where(i < last, glu(an_ref[0], gn_ref[0]), 0.0)

    span = sh_ref.shape[1]
    for c in range(ch // LANES):
        sl = slice(c * LANES, (c + 1) * LANES)
        for s in range(SUBLANES):
            sh_ref[s] = ext_ref[s:s + span, sl]
        def rows(i, carry, sl=sl):
            r0 = pl.multiple_of(i * CONV_ROW_CHUNK, CONV_ROW_CHUNK)
            acc = jnp.zeros((CONV_ROW_CHUNK, LANES), F32)
            for j in range(width):
                start = CONV_HALO - pad + j
                base = start - start % SUBLANES
                acc = acc + w_ref[j:j + 1, sl] * sh_ref[start % SUBLANES, pl.ds(base + r0, CONV_ROW_CHUNK), :]
            cv_ref[pl.ds(r0, CONV_ROW_CHUNK), sl] = acc + cb_ref[:, sl]
            return carry

        lax.fori_loop(0, tm // CONV_ROW_CHUNK, rows, 0)

    v = cv_ref[...]
    mu = jnp.mean(v, axis=-1, keepdims=True)
    vc = v - mu
    var = jnp.mean(vc * vc, axis=-1, keepdims=True)
    y = vc * lax.rsqrt(var + EPS) * lg_ref[...] + lb_ref[...]
    o_ref[0] = (y * _sigmoid(y)).astype(BF16)


def _conformer_conv(proj, conv_w, conv_b, ln_g, ln_b, cols):
    b, n, _ = proj.shape
    width, ch = conv_w.shape
    tm = min(n, 512)
    assert n % tm == 0 and tm % CONV_HALO == 0 and width // 2 < CONV_HALO
    a_blk = cols["conv"] // ch
    g_blk = a_blk + 1
    r = tm // CONV_HALO
    n_halo = n // CONV_HALO

    def cur(blk):
        return pl.BlockSpec((1, tm, ch), lambda bi, i: (bi, i, blk))

    def prev(blk):
        return pl.BlockSpec((1, CONV_HALO, ch), lambda bi, i: (bi, jnp.maximum(i * r - 1, 0), blk))

    def nxt(blk):
        return pl.BlockSpec((1, CONV_HALO, ch), lambda bi, i: (bi, jnp.minimum((i + 1) * r, n_halo - 1), blk))

    vec = pl.BlockSpec((1, ch), lambda bi, i: (0, 0))
    return pl.pallas_call(
        _conv_kernel,
        out_shape=jax.ShapeDtypeStruct((b, n, ch), BF16),
        grid=(b, n // tm),
        in_specs=[cur(a_blk), cur(g_blk), prev(a_blk), prev(g_blk), nxt(a_blk), nxt(g_blk),
                  pl.BlockSpec((width, ch), lambda bi, i: (0, 0)), vec, vec, vec],
        out_specs=pl.BlockSpec((1, tm, ch), lambda bi, i: (bi, i, 0)),
        scratch_shapes=[pltpu.VMEM((tm + 2 * CONV_HALO, ch), F32), pltpu.VMEM((tm, ch), F32),
                        pltpu.VMEM((SUBLANES, tm + 2 * CONV_HALO - SUBLANES, LANES), F32)],
        compiler_params=_params(("arbitrary", "arbitrary"),
                                (2 * tm + 4 * CONV_HALO) * ch * 4 + tm * ch * 2 + 40 * ch * 4,
                                (2 * tm + 2 * CONV_HALO) * ch * 4 + 4 * tm * ch * 4
                                + SUBLANES * (tm + 2 * CONV_HALO) * LANES * 4),
        name="conformer_conv",
    )(proj, proj, proj, proj, proj, proj, conv_w, conv_b.reshape(1, ch), ln_g.reshape(1, ch),
      ln_b.reshape(1, ch))


def _ret_kernel(qf_ref, kf_ref, vf_ref, gf_ref, qb_ref, kb_ref, vb_ref, gb_ref, s0_ref,
                inner_ref, xi_ref, zeta_ref, gc_ref, gng_ref, gnb_ref,
                of_ref, ob_ref, sfin_ref, s_ref, *, kdim):
    i = pl.program_id(1)
    c = qf_ref.shape[1]
    dv = gng_ref.shape[1]
    n_pair = qf_ref.shape[2] // LANES

    @pl.when(i == 0)
    def _():
        s_ref[...] = s0_ref[0]

    lo = _lane_iota((c, LANES)) < kdim
    dirs = ((qf_ref, kf_ref, vf_ref, gf_ref, of_ref), (qb_ref, kb_ref, vb_ref, gb_ref, ob_ref))
    for d, (q_ref, k_ref, v_ref, g_ref, o_ref) in enumerate(dirs):
        for p in range(n_pair):
            sl = slice(p * LANES, (p + 1) * LANES)
            q = q_ref[0, :, sl]
            k = k_ref[0, :, sl]
            kz = k * zeta_ref[d, p]
            kb16 = k.astype(BF16)
            s_pair = s_ref[d, p]
            s16 = s_pair.astype(BF16)
            upd = jnp.zeros((LANES, dv), F32)
            for hh in range(2):
                h = 2 * p + hh
                hsl = slice(h * dv, (h + 1) * dv)
                keep = lo if hh == 0 else jnp.logical_not(lo)
                qh = jnp.where(keep, q, 0.0).astype(BF16)
                v16 = v_ref[0, :, hsl].astype(BF16)
                att = lax.dot_general(qh, kb16, (((1,), (1,)), ((), ())), preferred_element_type=F32)
                att = (att * inner_ref[d, h]).astype(BF16)
                o = (jnp.dot(att, v16, preferred_element_type=F32)
                     + jnp.dot(qh, s16, preferred_element_type=F32) * xi_ref[d, h])
                kzh = jnp.where(keep, kz, 0.0).astype(BF16)
                upd = upd + lax.dot_general(kzh, v16, (((0,), (0,)), ((), ())), preferred_element_type=F32)
                mu = jnp.mean(o, axis=-1, keepdims=True)
                oc = o - mu
                var = jnp.mean(oc * oc, axis=-1, keepdims=True)
                y = oc * lax.rsqrt(var + EPS) * gng_ref[...] + gnb_ref[...]
                g = g_ref[0, :, hsl]
                o_ref[0, :, hsl] = (g * _sigmoid(g)) * y
            s_ref[d, p] = s_pair * gc_ref[p] + upd

    sfin_ref[0] = s_ref[...]


def _retention(rq, rk, proj, s0, tabs, gn_g, gn_b, cols):
    b, n, rw = rq.shape
    dv = gn_g.shape[0]
    d = cols["d"]
    c = RET_CHUNK
    nch = n // c
    kdim = dv // 2
    assert n % c == 0 and c == LANES and 2 * kdim == LANES
    n_pair = rw // LANES
    inner, xi, zeta, gc = tabs
    v_blk, gf_blk, gb_blk = cols["rv"] // d, cols["rgf"] // d, cols["rgb"] // d

    def fwd(width, blk):
        return pl.BlockSpec((1, c, width), lambda bi, i: (bi, i, blk))

    def bwd(width, blk):
        return pl.BlockSpec((1, c, width), lambda bi, i: (bi, nch - 1 - i, blk))

    def whole(a):
        return pl.BlockSpec(a.shape, lambda bi, i: (0,) * a.ndim)

    state = pl.BlockSpec((1, 2, n_pair, LANES, dv), lambda bi, i: (bi, 0, 0, 0, 0))
    vec = pl.BlockSpec((1, dv), lambda bi, i: (0, 0))
    kern = functools.partial(_ret_kernel, kdim=kdim)
    pipelined = 2 * c * (2 * rw + 2 * d) * 4 + 2 * c * d * 4 + 2 * 2 * n_pair * LANES * dv * 4
    resident = (inner.size + xi.size + zeta.size + gc.size) * 4 * 2 + 2 * n_pair * LANES * dv * 4 + 8 * MIB
    return pl.pallas_call(
        kern,
        out_shape=(jax.ShapeDtypeStruct((b, n, d), F32), jax.ShapeDtypeStruct((b, n, d), F32),
                   jax.ShapeDtypeStruct((b, 2, n_pair, LANES, dv), F32)),
        grid=(b, nch),
        in_specs=[fwd(rw, 0), fwd(rw, 0), fwd(d, v_blk), fwd(d, gf_blk),
                  bwd(rw, 0), bwd(rw, 0), bwd(d, v_blk), bwd(d, gb_blk),
                  state, whole(inner), whole(xi), whole(zeta), whole(gc), vec, vec],
        out_specs=(pl.BlockSpec((1, c, d), lambda bi, i: (bi, i, 0)),
                   pl.BlockSpec((1, c, d), lambda bi, i: (bi, nch - 1 - i, 0)),
                   state),
        scratch_shapes=[pltpu.VMEM((2, n_pair, LANES, dv), F32)],
        compiler_params=_params(("arbitrary", "arbitrary"), pipelined, resident),
        name="retention",
    )(rq, rk, proj, proj, rq, rk, proj, proj, s0, inner, xi, zeta, gc,
      gn_g.reshape(1, dv), gn_b.reshape(1, dv))


def _retention_tables(heads, kdim, dv):
    c = RET_CHUNK
    lg = jnp.log1p(-jnp.exp2(-5.0 - jnp.arange(heads, dtype=F32)))
    pos = jnp.arange(c, dtype=F32)
    diff = pos[:, None] - pos[None, :]
    inner_f = jnp.where(diff[None] >= 0, jnp.exp(jnp.maximum(diff, 0.0)[None] * lg[:, None, None]), 0.0)
    inner = jnp.stack([inner_f, jnp.swapaxes(inner_f, 1, 2)])
    xi_f = jnp.exp((pos + 1.0)[None, :] * lg[:, None])
    zeta_f = jnp.exp((c - 1.0 - pos)[None, :] * lg[:, None])
    xi = jnp.stack([xi_f, xi_f[:, ::-1]])
    zeta = jnp.stack([zeta_f, zeta_f[:, ::-1]])
    xi = jnp.broadcast_to(xi[..., None], (2, heads, c, dv))
    zeta = jnp.broadcast_to(zeta[..., None], (2, heads, c, kdim))
    zeta = zeta.reshape(2, heads // 2, 2, c, kdim).transpose(0, 1, 3, 2, 4).reshape(2, heads // 2, c, 2 * kdim)
    gc = jnp.exp(c * lg)
    gc = jnp.broadcast_to(gc[:, None, None], (heads, kdim, dv)).reshape(heads // 2, 2 * kdim, dv)
    return inner.astype(F32), xi.astype(F32), zeta.astype(F32), gc.astype(F32)


def _merge_kernel(h_ref, mod_ref, oa_ref, ob_ref, of_ref, obk_ref, gate_ref, bg_ref,
                  wa_ref, wb_ref, wc_ref, wo_ref, o_ref):
    d = h_ref.shape[2]
    g = _sigmoid(gate_ref[0] + bg_ref[...])
    oc = (of_ref[0] + obk_ref[0]).astype(BF16)
    y = g[:, 0:d] * jnp.dot(oa_ref[0], wa_ref[...], preferred_element_type=F32)
    y = y + g[:, d:2 * d] * jnp.dot(ob_ref[0], wb_ref[...], preferred_element_type=F32)
    y = y + g[:, 2 * d:3 * d] * jnp.dot(oc, wc_ref[...], preferred_element_type=F32)
    m = jnp.dot(y.astype(BF16), wo_ref[...], preferred_element_type=F32)
    o_ref[0] = h_ref[0] + mod_ref[0, 2:3, :] * m


def _merge(h, mod, oa, ob, of, obk, proj, b_gate, w_pa, w_pb, w_pc, w_out, layer, cols):
    b, n, d = h.shape
    tm = min(n, 512)
    assert n % tm == 0
    mb = mod.shape[0]
    gate_blk = cols["gate"] // (3 * d)
    row = lambda width: pl.BlockSpec((1, tm, width), lambda bi, i: (bi, i, 0))
    wspec = pl.BlockSpec((None, d, d), lambda bi, i: (layer, 0, 0))
    return pl.pallas_call(
        _merge_kernel,
        out_shape=jax.ShapeDtypeStruct((b, n, d), F32),
        grid=(b, n // tm),
        in_specs=[row(d), pl.BlockSpec((1, N_MOD, d), lambda bi, i: (bi % mb, 0, 0)),
                  row(d), row(d), row(d), row(d),
                  pl.BlockSpec((1, tm, 3 * d), lambda bi, i: (bi, i, gate_blk)),
                  pl.BlockSpec((1, 3 * d), lambda bi, i: (0, 0)),
                  wspec, wspec, wspec, wspec],
        out_specs=row(d),
        compiler_params=_params(("arbitrary", "arbitrary"),
                                tm * d * (4 + 2 + 2 + 4 + 4 + 12 + 4) + 4 * d * d * 2, 10 * tm * d * 4),
        name="merge",
    )(h, mod, oa, ob, of, obk, proj, b_gate.reshape(1, 3 * d), w_pa, w_pb, w_pc, w_out)


ROUTE_E, ROUTE_G, ROUTE_R = 0, 4, 8


def _route_kernel(h_ref, mod_ref, g_ref, wr_ref, br_ref, tri_ref, base_ref,
                  hx_ref, route_ref, cnt_ref, run_ref, *, n_exp):
    first = jnp.logical_and(pl.program_id(0) == 0, pl.program_id(1) == 0)

    @pl.when(first)
    def _():
        run_ref[...] = base_ref[...]

    h = h_ref[0]
    tm = h.shape[0]
    y = h * lax.rsqrt(jnp.mean(h * h, axis=-1, keepdims=True) + EPS) * g_ref[...]
    hx = y * (1.0 + mod_ref[0, 4:5, :]) + mod_ref[0, 3:4, :]
    hx_ref[0] = hx

    lane = _lane_iota((tm, LANES))
    lane_f = lane.astype(F32)
    hx_hi = hx.astype(BF16)
    hx_lo = (hx - hx_hi.astype(F32)).astype(BF16)
    logits = (jnp.dot(hx_hi, wr_ref[0], preferred_element_type=F32)
              + jnp.dot(hx_hi, wr_ref[1], preferred_element_type=F32)
              + jnp.dot(hx_lo, wr_ref[0], preferred_element_type=F32)) + br_ref[...]
    logits = jnp.where(lane < n_exp, logits, NEG_INF)

    sels, vals = [], []
    for _ in range(TOP_K):
        mx = jnp.max(logits, axis=-1, keepdims=True)
        idx = jnp.min(jnp.where(logits == mx, lane_f, float(LANES)), axis=-1, keepdims=True)
        sel = lane_f == idx
        sels.append((sel, idx))
        vals.append(mx)
        logits = jnp.where(sel, NEG_INF, logits)

    es = [jnp.exp(v - vals[0]) for v in vals]
    denom = es[0]
    for e in es[1:]:
        denom = denom + e

    onehot = jnp.zeros((tm, LANES), F32)
    for sel, _ in sels:
        onehot = onehot + jnp.where(sel, 1.0, 0.0)
    before = run_ref[...] + jnp.dot(tri_ref[...], onehot.astype(BF16), preferred_element_type=F32)

    route = jnp.zeros((tm, LANES), F32)
    for k, (sel, idx) in enumerate(sels):
        rank = jnp.sum(jnp.where(sel, before, 0.0), axis=-1, keepdims=True)
        route = jnp.where(lane == ROUTE_E + k, idx, route)
        route = jnp.where(lane == ROUTE_G + k, es[k] / denom, route)
        route = jnp.where(lane == ROUTE_R + k, rank, route)
    route_ref[0] = route

    run_ref[...] = run_ref[...] + jnp.sum(onehot, axis=0, keepdims=True)
    cnt_ref[...] = run_ref[...]


def _route(h, mod, g, w_router, b_router, base):
    b, n, d = h.shape
    n_exp = w_router.shape[-1]
    tm = min(n, 256)
    assert n % tm == 0 and n_exp <= LANES
    mb = mod.shape[0]
    wr = jnp.zeros((d, LANES), F32).at[:, :n_exp].set(w_router)
    wr_hi = wr.astype(BF16)
    wr = jnp.stack([wr_hi, (wr - wr_hi.astype(F32)).astype(BF16)])
    br = jnp.zeros((1, LANES), F32).at[0, :n_exp].set(b_router)
    tri = (jnp.arange(tm)[:, None] > jnp.arange(tm)[None, :]).astype(BF16)
    row = lambda width: pl.BlockSpec((1, tm, width), lambda bi, i: (bi, i, 0))
    const = lambda shape: pl.BlockSpec(shape, lambda bi, i: (0,) * len(shape))
    kern = functools.partial(_route_kernel, n_exp=n_exp)
    return pl.pallas_call(
        kern,
        out_shape=(jax.ShapeDtypeStruct((b, n, d), F32), jax.ShapeDtypeStruct((b, n, LANES), F32),
                   jax.ShapeDtypeStruct((1, LANES), F32)),
        grid=(b, n // tm),
        in_specs=[row(d), pl.BlockSpec((1, N_MOD, d), lambda bi, i: (bi % mb, 0, 0)), const((1, d)),
                  const((2, d, LANES)), const((1, LANES)), const((tm, tm)), const((1, LANES))],
        out_specs=(row(d), row(LANES), const((1, LANES))),
        scratch_shapes=[pltpu.VMEM((1, LANES), F32)],
        compiler_params=_params(("arbitrary", "arbitrary"),
                                2 * tm * d * 4 + tm * LANES * 4 + d * LANES * 4 + tm * tm * 2, 8 * tm * d * 4),
        name="moe_route",
    )(h, mod, g.reshape(1, d), wr, br, tri, base)


DMA_ISSUE_UNROLL = 8


def _tile_dest(dest, tm):
    t = dest.shape[0]
    return dest.reshape(t // tm, tm, TOP_K).transpose(0, 2, 1).reshape(t // tm, 1, TOP_K * tm)


def _dispatch_kernel(tile_ref, sub_ref, hx_ref, xs_in_ref, xs_ref, sem):
    del xs_in_ref
    tm = hx_ref.shape[0]

    def copy(k, t):
        r = k * tm + t
        return pltpu.make_async_copy(hx_ref.at[pl.ds(t, 1)],
                                     xs_ref.at[tile_ref[0, 0, r], pl.ds(sub_ref[0, 0, r], 1)], sem)

    for t in range(tm):
        for k in range(TOP_K):
            copy(k, t).start(priority=(t * TOP_K + k) % 2)
    for k in range(TOP_K):
        def wait(t, c, k=k):
            copy(k, t).wait()
            return c

        lax.fori_loop(0, tm, wait, 0, unroll=DMA_ISSUE_UNROLL)


def _dispatch(hx_flat, dest, xs):
    t, d = hx_flat.shape
    tm = min(t, 256)
    assert t % tm == 0
    dest3 = _tile_dest(dest, tm)
    n_rows = xs.shape[0]
    assert n_rows % SUBLANES == 0
    xs3 = xs.reshape(n_rows // SUBLANES, SUBLANES, d)
    idx = pl.BlockSpec((1, 1, tm * TOP_K), lambda i: (i, 0, 0), memory_space=pltpu.SMEM)
    out = pl.pallas_call(
        _dispatch_kernel,
        out_shape=jax.ShapeDtypeStruct(xs3.shape, xs3.dtype),
        grid=(t // tm,),
        in_specs=[idx, idx,
                  pl.BlockSpec((tm, d), lambda i: (i, 0)),
                  pl.BlockSpec(memory_space=pl.ANY)],
        out_specs=pl.BlockSpec(memory_space=pl.ANY),
        scratch_shapes=[pltpu.SemaphoreType.DMA],
        input_output_aliases={3: 0},
        compiler_params=pltpu.CompilerParams(dimension_semantics=("arbitrary",),
                                             vmem_limit_bytes=_vmem_limit(tm * d * 4, 4 * MIB),
                                             has_side_effects=True),
        name="moe_dispatch",
    )(dest3 // SUBLANES, dest3 % SUBLANES, hx_flat, xs3)
    return out.reshape(n_rows, d)


def _split_kernel(w_ref, sel_ref, g_ref, u_ref):
    blk = sel_ref.shape[0]
    half = blk // 2
    for i in range(w_ref.shape[1] // blk):
        r = jnp.dot(w_ref[:, i * blk:(i + 1) * blk].astype(BF16), sel_ref[...], preferred_element_type=F32)
        g_ref[:, i * half:(i + 1) * half] = r[:, :half].astype(BF16)
        u_ref[:, i * half:(i + 1) * half] = r[:, half:].astype(BF16)


def _split_gate_up(w_gate_up):
    depth, n_exp, d, de2 = w_gate_up.shape
    de = de2 // 2
    tr = min(d, 512)
    assert d % tr == 0
    blk = 2 * LANES
    assert de2 % blk == 0
    order = jnp.concatenate([jnp.arange(0, blk, 2), jnp.arange(1, blk, 2)])
    sel = (jnp.arange(blk)[:, None] == order[None, :]).astype(BF16)
    out = jax.ShapeDtypeStruct((depth, n_exp, d, de), BF16)
    ospec = pl.BlockSpec((None, None, tr, de), lambda l, e, r: (l, e, r, 0))
    return pl.pallas_call(
        _split_kernel,
        out_shape=(out, out),
        grid=(depth, n_exp, d // tr),
        in_specs=[pl.BlockSpec((None, None, tr, de2), lambda l, e, r: (l, e, r, 0)),
                  pl.BlockSpec((blk, blk), lambda l, e, r: (0, 0))],
        out_specs=(ospec, ospec),
        compiler_params=_params(("arbitrary", "arbitrary", "arbitrary"),
                                tr * de2 * 4 + blk * blk * 2 + 2 * tr * de * 2, tr * de2 * (2 + 4 + 4)),
        name="split_gate_up",
    )(w_gate_up, sel)


def _expert_kernel(blk_e_ref, n_used_ref, x_ref, wg_ref, wu_ref, wd_ref, bg_ref, bu_ref, bd_ref, y_ref):
    del blk_e_ref
    used = pl.program_id(0) < n_used_ref[0]

    @pl.when(used)
    def _():
        x = x_ref[...].astype(BF16)
        gl = jnp.dot(x, wg_ref[...], preferred_element_type=F32) + bg_ref[...]
        up = jnp.dot(x, wu_ref[...], preferred_element_type=F32) + bu_ref[...]
        gl = jnp.minimum(gl, SWIGLU_LIMIT)
        up = jnp.clip(up, -SWIGLU_LIMIT, SWIGLU_LIMIT)
        act = (up + 1.0) * (gl * _sigmoid(SWIGLU_ALPHA * gl))
        y_ref[...] = jnp.dot(act.astype(BF16), wd_ref[...], preferred_element_type=F32) + bd_ref[...]

    @pl.when(jnp.logical_not(used))
    def _():
        y_ref[...] = jnp.zeros(y_ref.shape, F32)


def _experts(blk_e, n_used, xs, wg, wu, wd, bg, bu, bd, layer, bm):
    n_rows, d = xs.shape
    de = wg.shape[-1]
    nb = n_rows // bm
    wspec = lambda a, c: pl.BlockSpec((None, None, a, c), lambda i, be, nu: (layer, be[i], 0, 0))
    return pl.pallas_call(
        _expert_kernel,
        out_shape=jax.ShapeDtypeStruct((n_rows, d), F32),
        grid_spec=pltpu.PrefetchScalarGridSpec(
            num_scalar_prefetch=2,
            grid=(nb,),
            in_specs=[pl.BlockSpec((bm, d), lambda i, be, nu: (i, 0)),
                      wspec(d, de), wspec(d, de), wspec(de, d),
                      wspec(1, de), wspec(1, de), wspec(1, d)],
            out_specs=pl.BlockSpec((bm, d), lambda i, be, nu: (i, 0))),
        compiler_params=_params(("arbitrary",), 2 * bm * d * 4 + 3 * d * de * 2, 8 * bm * de * 4),
        name="moe_experts",
    )(blk_e, n_used, xs, wg, wu, wd, bg, bu, bd)


def _combine_kernel(dest_ref, dnext_ref, h_ref, mod_ref, route_ref, y_ref, o_ref, ybuf0, ybuf1, sem):
    tm = h_ref.shape[1]
    n_copy = TOP_K * tm
    step = pl.program_id(0) * pl.num_programs(1) + pl.program_id(1)
    n_step = pl.num_programs(0) * pl.num_programs(1)
    bufs = (ybuf0, ybuf1)

    def copy(idx_ref, r, slot):
        return pltpu.make_async_copy(y_ref.at[pl.ds(idx_ref[0, 0, r], 1)], bufs[slot].at[pl.ds(r, 1)],
                                     sem.at[slot])

    def fetch(idx_ref, slot):
        for r in range(n_copy):
            copy(idx_ref, r, slot).start(priority=r % 2)

    def finish(slot):
        def wait(r, c):
            copy(dest_ref, r, slot).wait()
            return c

        lax.fori_loop(0, n_copy, wait, 0, unroll=DMA_ISSUE_UNROLL)
        route = route_ref[0]
        f = jnp.zeros((tm, h_ref.shape[2]), F32)
        for k in range(TOP_K):
            f = f + route[:, ROUTE_G + k:ROUTE_G + k + 1] * bufs[slot][k * tm:(k + 1) * tm, :]
        o_ref[0] = h_ref[0] + mod_ref[0, 5:6, :] * f

    @pl.when(step == 0)
    def _():
        fetch(dest_ref, 0)

    for slot in range(2):
        @pl.when(step % 2 == slot)
        def _(slot=slot):
            @pl.when(step + 1 < n_step)
            def _():
                fetch(dnext_ref, 1 - slot)

            finish(slot)


def _combine(h, mod, route, dest, y):
    b, n, d = h.shape
    tm = min(n, 256)
    assert n % tm == 0
    mb = mod.shape[0]
    nt = n // tm
    dest3 = _tile_dest(dest, tm)
    row = lambda width: pl.BlockSpec((1, tm, width), lambda bi, i: (bi, i, 0))
    return pl.pallas_call(
        _combine_kernel,
        out_shape=jax.ShapeDtypeStruct((b, n, d), F32),
        grid=(b, nt),
        in_specs=[pl.BlockSpec((1, 1, tm * TOP_K), lambda bi, i: (bi * nt + i, 0, 0), memory_space=pltpu.SMEM),
                  pl.BlockSpec((1, 1, tm * TOP_K), lambda bi, i: (jnp.minimum(bi * nt + i + 1, b * nt - 1), 0, 0),
                               memory_space=pltpu.SMEM),
                  row(d), pl.BlockSpec((1, N_MOD, d), lambda bi, i: (bi % mb, 0, 0)), row(LANES),
                  pl.BlockSpec(memory_space=pl.ANY)],
        out_specs=row(d),
        scratch_shapes=[pltpu.VMEM((TOP_K * tm, d), F32), pltpu.VMEM((TOP_K * tm, d), F32),
                        pltpu.SemaphoreType.DMA((2,))],
        compiler_params=_params(("arbitrary", "arbitrary"), 2 * tm * d * 4 + tm * LANES * 4,
                                2 * TOP_K * tm * d * 4 + 4 * tm * d * 4),
        name="moe_combine",
    )(dest3, dest3, h, mod, route, y)


def _moe_rows(n_tok, n_exp, bm):
    return -(-(n_tok * TOP_K + n_exp * (bm - 1)) // bm) * bm


def _moe(parts, xs, g, w_router, b_router, wg, wu, wd, bg, bu, bd, layer, bm):
    n_exp = w_router.shape[-1]
    n_rows, d = xs.shape
    base = jnp.zeros((1, LANES), F32)
    hxs, routes = [], []
    for h, mod in parts:
        hx, route, base = _route(h, mod, g, w_router, b_router, base)
        hxs.append(hx)
        routes.append(route)
    counts = base[0, :n_exp].astype(jnp.int32)

    padded = (counts + bm - 1) // bm * bm
    pend = jnp.cumsum(padded)
    pstart = pend - padded
    n_tok = sum(h.shape[0] * h.shape[1] for h, _ in parts)
    assert n_rows >= _moe_rows(n_tok, n_exp, bm) and n_rows % bm == 0
    nb = n_rows // bm
    blk_start = jnp.arange(nb, dtype=jnp.int32) * bm
    blk_e = jnp.minimum(jnp.sum((pend[None, :] <= blk_start[:, None]).astype(jnp.int32), axis=1), n_exp - 1)
    n_used = (pend[-1] // bm).astype(jnp.int32).reshape(1)
    eids = jnp.arange(n_exp, dtype=jnp.int32)

    dests = []
    for hx, route in zip(hxs, routes):
        e = route[..., ROUTE_E:ROUTE_E + TOP_K].astype(jnp.int32)
        rank = route[..., ROUTE_R:ROUTE_R + TOP_K].astype(jnp.int32)
        start = jnp.sum(jnp.where(e[..., None] == eids, pstart, 0), axis=-1)
        dest = (start + rank).reshape(-1, TOP_K)
        dests.append(dest)
        xs = _dispatch(hx.reshape(-1, d), dest, xs)

    y = _experts(blk_e, n_used, xs, wg, wu, wd, bg, bu, bd, layer, bm)
    return [_combine(h, mod, route, dest, y)
            for (h, mod), route, dest in zip(parts, routes, dests)], xs


def kernel(x, c, ctx, c_ctx, w_mod, b_mod, g_norm1, g_norm2, w_in, b_gate, g_qa, g_ka, lam_q1, lam_k1, lam_q2, lam_k2, g_sub, conv_w, conv_b, ln_g, ln_b, gn_g, gn_b, w_pa, w_pb, w_pc, w_out, w_router, b_router, w_gate_up, b_gate_up, w_down, b_down):
    bsz, n_x, d = x.shape
    n_c = ctx.shape[1]
    depth = w_mod.shape[0]
    da_dim = g_qa.shape[-1]
    dv = g_sub.shape[-1]
    heads = d // dv
    ret_kdim = gn_g.shape[-1] // 2
    n_exp = w_router.shape[-1]
    de = w_down.shape[-2]
    moe_bm = 512

    cols = {"d": d, "ret_qk": heads * ret_kdim}
    off = 0
    for name, width in (("aq", d), ("ak", d), ("av", d), ("conv", 2 * d), ("rq", heads * ret_kdim),
                        ("rk", heads * ret_kdim), ("rv", d), ("rgf", d), ("rgb", d), ("gate", 3 * d)):
        cols[name] = off
        off += width
    assert off == w_in.shape[-1]

    w_in_bf = w_in.astype(BF16)
    w_pa_bf, w_pb_bf, w_pc_bf, w_out_bf = (w.astype(BF16) for w in (w_pa, w_pb, w_pc, w_out))
    wg, wu = _split_gate_up(w_gate_up)
    wd = w_down.astype(BF16)
    bg = b_gate_up[..., 0::2].reshape(depth, n_exp, 1, de)
    bu = b_gate_up[..., 1::2].reshape(depth, n_exp, 1, de)
    bd = b_down.reshape(depth, n_exp, 1, d)

    rows = -(-(bsz + 1) // SUBLANES) * SUBLANES
    cvecs = jnp.zeros((rows, d), F32).at[:bsz].set(c).at[bsz].set(c_ctx)
    mods = _modulation(cvecs, w_mod, b_mod).reshape(depth, rows, N_MOD, d)

    tabs_x = _rope_tables(n_x, da_dim, ret_kdim)
    tabs_c = tuple(jnp.zeros((n_c, LANES), F32) for _ in range(4))
    ret_tabs = _retention_tables(heads, ret_kdim, dv)
    s_zero = jnp.zeros((bsz, 2, heads // 2, LANES, dv), F32)
    xs_rows = jnp.zeros((_moe_rows(bsz * (n_c + n_x), n_exp, moe_bm), d), F32)

    h_ctx = ctx
    for l in range(depth):
        need_ctx = l < depth - 1
        lam_init = 0.8 - 0.6 * math.exp(-0.3 * l)
        lam = (jnp.exp(jnp.sum(lam_q1[l] * lam_k1[l])) - jnp.exp(jnp.sum(lam_q2[l] * lam_k2[l])) + lam_init)
        scal = jnp.stack([lam, jnp.asarray(1.0 - lam_init, F32)]).astype(F32)
        mod_x = mods[l, :bsz]
        mod_c = mods[l, bsz:bsz + 1]

        proj_x = _in_proj(x, mod_x, g_norm1[l], w_in_bf, l)
        proj_c = _in_proj(h_ctx, mod_c, g_norm1[l], w_in_bf, l)
        aq_x, ak_x, rq_x, rk_x = _qk_prep(proj_x, g_qa[l], g_ka[l], tabs_x, cols, rope=True,
                                          da_dim=da_dim, ret_kdim=ret_kdim)
        aq_c, ak_c, rq_c, rk_c = _qk_prep(proj_c, g_qa[l], g_ka[l], tabs_c, cols, rope=False,
                                          da_dim=da_dim, ret_kdim=ret_kdim)
        oa_x = _attention(scal, aq_x, ak_x, proj_x, ak_c, proj_c, g_sub[l], cols, with_x=True, da_dim=da_dim)
        ob_x = _conformer_conv(proj_x, conv_w[l], conv_b[l], ln_g[l], ln_b[l], cols)
        of_c, ob_c, s_ctx = _retention(rq_c, rk_c, proj_c, s_zero, ret_tabs, gn_g[l], gn_b[l], cols)
        of_x, obk_x, _ = _retention(rq_x, rk_x, proj_x, s_ctx, ret_tabs, gn_g[l], gn_b[l], cols)
        x = _merge(x, mod_x, oa_x, ob_x, of_x, obk_x, proj_x, b_gate[l],
                   w_pa_bf, w_pb_bf, w_pc_bf, w_out_bf, l, cols)
        if need_ctx:
            oa_c = _attention(scal, aq_c, ak_c, proj_c, ak_c, proj_c, g_sub[l], cols, with_x=False, da_dim=da_dim)
            cb_c = _conformer_conv(proj_c, conv_w[l], conv_b[l], ln_g[l], ln_b[l], cols)
            h_ctx = _merge(h_ctx, mod_c, oa_c, cb_c, of_c, ob_c, proj_c, b_gate[l],
                           w_pa_bf, w_pb_bf, w_pc_bf, w_out_bf, l, cols)

        parts = [(h_ctx, mod_c), (x, mod_x)] if need_ctx else [(x, mod_x)]
        outs, xs_rows = _moe(parts, xs_rows, g_norm2[l], w_router[l], b_router[l], wg, wu, wd, bg, bu, bd,
                             l, moe_bm)
        if need_ctx:
            h_ctx, x = outs
        else:
            (x,) = outs
    return x
```
